```python
import math
import jax, jax.numpy as jnp
from jax import lax
import numpy as np


D_MODEL = 2048
BATCH = 1
SEQ = 16384
DEPTH = 2

HEAD_DIM = 128
RET_HEADS = 6
MLSTM_HEADS = 6
RET_WIDTH = RET_HEADS * HEAD_DIM
MLSTM_WIDTH = MLSTM_HEADS * HEAD_DIM
S5_WIDTH = D_MODEL - RET_WIDTH - MLSTM_WIDTH
S5_GROUP = 16
S5_GROUPS = S5_WIDTH // S5_GROUP
S5_STATE = 64
CONV_WIDTH = 4
CHUNK = 128
D_FF = 4 * D_MODEL
PLE_DIM = 256
ROPE_BASE = 10000.0
LN_EPS = 1e-5
ALPHA = (2 * DEPTH) ** 0.25
BETA = (8 * DEPTH) ** -0.25
IN_SPLITS = tuple(int(v) for v in np.cumsum([RET_WIDTH] * 4 + [MLSTM_WIDTH] * 4 + [MLSTM_HEADS] * 2))
IN_WIDTH = IN_SPLITS[-1] + S5_WIDTH

kernel_name = 'hybrid_retention_mlstm_s5_block'


def layer_norm(x, w, b):
    xf = x.astype(jnp.float32)
    mu = jnp.mean(xf, -1, keepdims=True)
    var = jnp.mean(jnp.square(xf - mu), -1, keepdims=True)
    return (xf - mu) * lax.rsqrt(var + LN_EPS) * w.astype(jnp.float32) + b.astype(jnp.float32)


def head_norm(y, w):
    mu = jnp.mean(y, -1, keepdims=True)
    var = jnp.mean(jnp.square(y - mu), -1, keepdims=True)
    yn = (y - mu) * lax.rsqrt(var + LN_EPS)
    return yn.reshape(y.shape[0], y.shape[1], -1) * w.astype(jnp.float32)


def apply_rotary(t, cos, sin):
    t1, t2 = jnp.split(t, 2, axis=-1)
    return jnp.concatenate([t1 * cos - t2 * sin, t1 * sin + t2 * cos], -1)


def to_chunks(t):
    b, l, h = t.shape[:3]
    t = t.reshape((b, l // CHUNK, CHUNK, h) + t.shape[3:])
    return t.transpose((1, 0, 3, 2) + tuple(range(4, t.ndim)))


def from_chunks(t):
    nc, b, h, c, d = t.shape
    return t.transpose(1, 0, 3, 2, 4).reshape(b, nc * c, h, d)


def retention(q, k, v):
    lg = jnp.log(1.0 - 2.0 ** (-5.0 - jnp.arange(RET_HEADS, dtype=jnp.float32)))
    idx = jnp.arange(CHUNK, dtype=jnp.float32)
    rel = idx[:, None] - idx[None, :]
    decay = jnp.where(rel >= 0, jnp.exp(lg[:, None, None] * jnp.maximum(rel, 0.0)), 0.0)
    q_dec = jnp.exp(lg[:, None] * (idx + 1.0))[:, :, None]
    k_dec = jnp.exp(lg[:, None] * (CHUNK - 1.0 - idx))[:, :, None]
    c_dec = jnp.exp(lg * CHUNK)[:, None, None]

    def step(state, inp):
        qc, kc, vc = inp
        scores = jnp.einsum('bhid,bhjd->bhij', qc, kc) * decay
        out = (jnp.einsum('bhij,bhje->bhie', scores, vc)
               + jnp.einsum('bhid,bhde->bhie', qc, state) * q_dec)
        state = state * c_dec + jnp.einsum('bhjd,bhje->bhde', kc * k_dec, vc)
        return state, out

    state0 = jnp.zeros((q.shape[0], RET_HEADS, HEAD_DIM, HEAD_DIM), jnp.float32)
    _, out = lax.scan(step, state0, (to_chunks(q), to_chunks(k), to_chunks(v)))
    return from_chunks(out)


def mlstm(q, k, v, log_i, log_f):
    causal = jnp.tril(jnp.ones((CHUNK, CHUNK), bool))

    def step(carry, inp):
        c_st, n_st, m_st = carry
        qc, kc, vc, ic, fc = inp
        bcum = jnp.cumsum(fc, axis=-1)
        log_d = jnp.where(causal, bcum[..., :, None] - bcum[..., None, :] + ic[..., None, :], -jnp.inf)
        log_inter = bcum + m_st[..., None]
        m_row = jnp.maximum(jnp.max(log_d, -1), log_inter)
        s = jnp.einsum('bhid,bhjd->bhij', qc, kc) * jnp.exp(log_d - m_row[..., None])
        w_inter = jnp.exp(log_inter - m_row)
        num = (jnp.einsum('bhij,bhje->bhie', s, vc)
               + w_inter[..., None] * jnp.einsum('bhid,bhde->bhie', qc, c_st))
        den = jnp.sum(s, -1) + w_inter * jnp.einsum('bhid,bhd->bhi', qc, n_st)
        h = num / jnp.maximum(jnp.abs(den), jnp.exp(-m_row))[..., None]
        b_last = bcum[..., -1]
        log_w = b_last[..., None] - bcum + ic
        m_new = jnp.maximum(b_last + m_st, jnp.max(log_w, -1))
        keep = jnp.exp(b_last + m_st - m_new)
        kw = kc * jnp.exp(log_w - m_new[..., None])[..., None]
        c_st = keep[..., None, None] * c_st + jnp.einsum('bhjd,bhje->bhde', kw, vc)
        n_st = keep[..., None] * n_st + jnp.sum(kw, axis=2)
        return (c_st, n_st, m_new), h

    b = q.shape[0]
    carry0 = (jnp.zeros((b, MLSTM_HEADS, HEAD_DIM, HEAD_DIM), jnp.float32),
              jnp.zeros((b, MLSTM_HEADS, HEAD_DIM), jnp.float32),
              jnp.zeros((b, MLSTM_HEADS), jnp.float32))
    _, h = lax.scan(step, carry0, (to_chunks(q), to_chunks(k), to_chunks(v), to_chunks(log_i), to_chunks(log_f)))
    return from_chunks(h)


def s5_ssm(u, lam_re, lam_im, log_dt, b_re, b_im, c_re, c_im, d_skip):
    f32 = jnp.float32
    bsz, l, _ = u.shape
    lam_re, lam_im = lam_re.astype(f32), lam_im.astype(f32)
    b_re, b_im, c_re, c_im = b_re.astype(f32), b_im.astype(f32), c_re.astype(f32), c_im.astype(f32)
    ug = u.reshape(bsz, l, S5_GROUPS, S5_GROUP)
    dt = jnp.exp(log_dt.astype(f32))[:, None]
    mag = jnp.exp(lam_re * dt)
    ang = lam_im * dt
    lb_re, lb_im = mag * jnp.cos(ang), mag * jnp.sin(ang)
    zr, zi = lb_re - 1.0, lb_im
    den = jnp.square(lam_re) + jnp.square(lam_im)
    w_re = (zr * lam_re + zi * lam_im) / den
    w_im = (zi * lam_re - zr * lam_im) / den
    bb_re = w_re[..., None] * b_re - w_im[..., None] * b_im
    bb_im = w_re[..., None] * b_im + w_im[..., None] * b_re
    bu_re = jnp.einsum('gph,blgh->blgp', bb_re, ug)
    bu_im = jnp.einsum('gph,blgh->blgp', bb_im, ug)
    a_re = jnp.broadcast_to(lb_re, bu_re.shape)
    a_im = jnp.broadcast_to(lb_im, bu_im.shape)

    def combine(e1, e2):
        a1r, a1i, b1r, b1i = e1
        a2r, a2i, b2r, b2i = e2
        return (a2r * a1r - a2i * a1i, a2r * a1i + a2i * a1r,
                a2r * b1r - a2i * b1i + b2r, a2r * b1i + a2i * b1r + b2i)

    _, _, x_re, x_im = lax.associative_scan(combine, (a_re, a_im, bu_re, bu_im), axis=1)
    y = jnp.einsum('ghp,blgp->blgh', c_re, x_re) - jnp.einsum('ghp,blgp->blgh', c_im, x_im)
    return y.reshape(bsz, l, S5_WIDTH) + d_skip.astype(f32) * u


def causal_conv(x, w, b):
    l = x.shape[1]
    xp = jnp.pad(x, ((0, 0), (CONV_WIDTH - 1, 0), (0, 0)))
    out = b.astype(jnp.float32)
    for tap in range(CONV_WIDTH):
        out = out + xp[:, tap:tap + l] * w[tap].astype(jnp.float32)
    return out


def hybrid_layer(x, cos, sin, p_l, w_in, conv_w, conv_b, i_bias, f_bias, ret_norm_w, mlstm_norm_w,
                 lam_re, lam_im, log_dt, b_re, b_im, c_re, c_im, s5_d, glu_w, glu_b,
                 w_out, ln1_w, ln1_b, w_up, w_down, w_gate, w_ple, ln2_w, ln2_b):
    f32 = jnp.float32
    bsz, l, _ = x.shape
    proj = jnp.matmul(x, w_in)
    rq, rk, rv, rg, mq, mk, mv, mo, mi, mf, su = jnp.split(proj, IN_SPLITS, axis=-1)

    def heads(t, h):
        return t.astype(f32).reshape(bsz, l, h, HEAD_DIM)

    q_r = apply_rotary(heads(rq, RET_HEADS), cos, sin)
    k_r = apply_rotary(heads(rk, RET_HEADS), cos, sin) * HEAD_DIM ** -0.5
    y_ret = head_norm(retention(q_r, k_r, heads(rv, RET_HEADS)), ret_norm_w) * jax.nn.silu(rg.astype(f32))

    qk = jax.nn.silu(causal_conv(jnp.concatenate([mq, mk], -1).astype(f32), conv_w, conv_b))
    q_m, k_m = jnp.split(qk, 2, axis=-1)
    log_i = mi.astype(f32) + i_bias.astype(f32)
    log_f = jax.nn.log_sigmoid(mf.astype(f32) + f_bias.astype(f32))
    h_m = mlstm(heads(q_m, MLSTM_HEADS), heads(k_m, MLSTM_HEADS) * HEAD_DIM ** -0.5,
                heads(mv, MLSTM_HEADS), log_i, log_f)
    y_m = head_norm(h_m, mlstm_norm_w) * jax.nn.sigmoid(mo.astype(f32))

    y_s = jax.nn.gelu(s5_ssm(su.astype(f32), lam_re, lam_im, log_dt, b_re, b_im, c_re, c_im, s5_d))
    y_s = y_s * jax.nn.sigmoid(jnp.matmul(y_s, glu_w) + glu_b)

    mix = jnp.matmul(jnp.concatenate([y_ret, y_m, y_s], -1), w_out)
    x = layer_norm(ALPHA * x + mix, ln1_w, ln1_b)

    ff = jnp.matmul(jnp.square(jax.nn.relu(jnp.matmul(x, w_up))), w_down)
    r = ALPHA * x + ff
    ple = jax.nn.sigmoid(jnp.matmul(r, w_gate)) * jnp.matmul(p_l, w_ple)
    return layer_norm(r + ple, ln2_w, ln2_b)


def setup_inputs(seed: int = 0) -> dict:
    key = jax.random.key(seed)
    ks = jax.random.split(key, 32)
    f32 = jnp.float32

    def nrm(k, shape, scale):
        return jax.random.normal(k, shape, f32) * scale

    col_scale = np.ones(IN_WIDTH, np.float32)
    col_scale[2 * RET_WIDTH:3 * RET_WIDTH] = BETA
    col_scale[4 * RET_WIDTH + 2 * MLSTM_WIDTH:4 * RET_WIDTH + 3 * MLSTM_WIDTH] = BETA
    positions = jnp.broadcast_to(jnp.arange(SEQ, dtype=jnp.int32), (BATCH, SEQ))
    return {
        'x': nrm(ks[0], (BATCH, SEQ, D_MODEL), 1.0),
        'p': nrm(ks[1], (DEPTH, BATCH, SEQ, PLE_DIM), 1.0),
        'positions': positions,
        'w_in': nrm(ks[2], (DEPTH, D_MODEL, IN_WIDTH), D_MODEL ** -0.5) * jnp.asarray(col_scale),
        'mlstm_conv_w': nrm(ks[3], (DEPTH, CONV_WIDTH, 2 * MLSTM_WIDTH), CONV_WIDTH ** -0.5),
        'mlstm_conv_b': nrm(ks[4], (DEPTH, 2 * MLSTM_WIDTH), 0.02),
        'mlstm_i_bias': nrm(ks[5], (DEPTH, MLSTM_HEADS), 0.1),
        'mlstm_f_bias': jnp.linspace(3.0, 6.0, MLSTM_HEADS, dtype=f32) + nrm(ks[6], (DEPTH, MLSTM_HEADS), 0.1),
        'ret_norm_w': 1.0 + nrm(ks[7], (DEPTH, RET_WIDTH), 0.02),
        'mlstm_norm_w': 1.0 + nrm(ks[8], (DEPTH, MLSTM_WIDTH), 0.02),
        's5_lambda_re': -0.5 + nrm(ks[9], (DEPTH, S5_GROUPS, S5_STATE), 0.01),
        's5_lambda_im': jnp.pi * jnp.arange(S5_STATE, dtype=f32) + nrm(ks[10], (DEPTH, S5_GROUPS, S5_STATE), 0.01),
        's5_log_dt': jax.random.uniform(ks[11], (DEPTH, S5_GROUPS), f32, math.log(0.001), math.log(0.1)),
        's5_B_re': nrm(ks[12], (DEPTH, S5_GROUPS, S5_STATE, S5_GROUP), (2 * S5_GROUP) ** -0.5),
        's5_B_im': nrm(ks[13], (DEPTH, S5_GROUPS, S5_STATE, S5_GROUP), (2 * S5_GROUP) ** -0.5),
        's5_C_re': nrm(ks[14], (DEPTH, S5_GROUPS, S5_GROUP, S5_STATE), (2 * S5_STATE) ** -0.5),
        's5_C_im': nrm(ks[15], (DEPTH, S5_GROUPS, S5_GROUP, S5_STATE), (2 * S5_STATE) ** -0.5),
        's5_D': nrm(ks[16], (DEPTH, S5_WIDTH), 1.0),
        's5_glu_w': nrm(ks[17], (DEPTH, S5_WIDTH, S5_WIDTH), S5_WIDTH ** -0.5),
        's5_glu_b': nrm(ks[18], (DEPTH, S5_WIDTH), 0.02),
        'w_out': nrm(ks[19], (DEPTH, D_MODEL, D_MODEL), BETA * D_MODEL ** -0.5),
        'ln1_w': 1.0 + nrm(ks[20], (DEPTH, D_MODEL), 0.02),
        'ln1_b': nrm(ks[21], (DEPTH, D_MODEL), 0.02),
        'w_up': nrm(ks[22], (DEPTH, D_MODEL, D_FF), D_MODEL ** -0.5),
        'w_down': nrm(ks[23], (DEPTH, D_FF, D_MODEL), BETA * D_FF ** -0.5),
        'w_gate': nrm(ks[24], (DEPTH, D_MODEL, D_MODEL), D_MODEL ** -0.5),
        'w_ple': nrm(ks[25], (DEPTH, PLE_DIM, D_MODEL), BETA * PLE_DIM ** -0.5),
        'ln2_w': 1.0 + nrm(ks[26], (DEPTH, D_MODEL), 0.02),
        'ln2_b': nrm(ks[27], (DEPTH, D_MODEL), 0.02),
    }


def reference(x, p, positions, w_in, mlstm_conv_w, mlstm_conv_b, mlstm_i_bias, mlstm_f_bias,
              ret_norm_w, mlstm_norm_w, s5_lambda_re, s5_lambda_im, s5_log_dt, s5_B_re, s5_B_im,
              s5_C_re, s5_C_im, s5_D, s5_glu_w, s5_glu_b, w_out, ln1_w, ln1_b, w_up, w_down,
              w_gate, w_ple, ln2_w, ln2_b):
    f32 = jnp.float32
    inv_freq = ROPE_BASE ** (-jnp.arange(0, HEAD_DIM, 2, dtype=f32) / HEAD_DIM)
    ang = positions.astype(f32)[..., None] * inv_freq
    cos = jnp.cos(ang)[:, :, None, :]
    sin = jnp.sin(ang)[:, :, None, :]
    for i in range(DEPTH):
        x = hybrid_layer(x, cos, sin, p[i], w_in[i], mlstm_conv_w[i], mlstm_conv_b[i], mlstm_i_bias[i],
                         mlstm_f_bias[i], ret_norm_w[i], mlstm_norm_w[i], s5_lambda_re[i], s5_lambda_im[i],
                         s5_log_dt[i], s5_B_re[i], s5_B_im[i], s5_C_re[i], s5_C_im[i], s5_D[i],
                         s5_glu_w[i], s5_glu_b[i], w_out[i], ln1_w[i], ln1_b[i], w_up[i], w_down[i],
                         w_gate[i], w_ple[i], ln2_w[i], ln2_b[i])
    return x
```

```python
import functools
import math

import jax
import jax.numpy as jnp
import numpy as np
from jax import lax
from jax.experimental import pallas as pl
from jax.experimental.pallas import tpu as pltpu

F32 = jnp.float32
BF16 = jnp.bfloat16

D_MODEL = 2048
DEPTH = 2
HEAD_DIM = 128
RET_HEADS = 6
MLSTM_HEADS = 6
RET_WIDTH = RET_HEADS * HEAD_DIM
MLSTM_WIDTH = MLSTM_HEADS * HEAD_DIM
S5_WIDTH = D_MODEL - RET_WIDTH - MLSTM_WIDTH
S5_GROUP = 16
S5_GROUPS = S5_WIDTH // S5_GROUP
S5_STATE = 64
CONV_WIDTH = 4
CHUNK = 128
D_FF = 4 * D_MODEL
PLE_DIM = 256
ROPE_BASE = 10000.0
LN_EPS = 1e-5
ALPHA = (2 * DEPTH) ** 0.25
QK_SCALE = HEAD_DIM ** -0.5

LANES = 128
MAIN_WIDTH = 4 * RET_WIDTH + 4 * MLSTM_WIDTH
GATE_COL = MAIN_WIDTH + S5_WIDTH
PROJ_WIDTH = GATE_COL + 2 * LANES
S5_COLBLK = MAIN_WIDTH // LANES
GATE_COLBLK = GATE_COL // LANES
S5_T = 16
S5_GB = S5_WIDTH // LANES
S5_GPB = LANES // S5_GROUP
S5_SB = S5_GPB * S5_STATE
VMEM_LIMIT = 56 * 1024 * 1024


def _cparams(sem):
    return pltpu.CompilerParams(dimension_semantics=sem, vmem_limit_bytes=VMEM_LIMIT)


def _mm_kernel(a_ref, b_ref, o_ref, *scratch, nk, act):
    def finish(r):
        if act == "relu2":
            r = jnp.square(jnp.maximum(r, 0.0))
        o_ref[...] = r.astype(o_ref.dtype)

    if nk == 1:
        finish(jnp.dot(a_ref[...], b_ref[...], preferred_element_type=F32))
        return
    acc_ref, = scratch
    k = pl.program_id(2)

    @pl.when(k == 0)
    def _():
        acc_ref[...] = jnp.zeros_like(acc_ref)

    acc_ref[...] += jnp.dot(a_ref[...], b_ref[...], preferred_element_type=F32)

    @pl.when(k == nk - 1)
    def _():
        finish(acc_ref[...])


def matmul(a, b, *, out_dtype, tm, tn, tk=None, act=None):
    m, kdim = a.shape
    _, n = b.shape
    tk = kdim if tk is None else tk
    nk = kdim // tk
    assert m % tm == 0 and n % tn == 0 and kdim % tk == 0
    return pl.pallas_call(
        functools.partial(_mm_kernel, nk=nk, act=act),
        out_shape=jax.ShapeDtypeStruct((m, n), out_dtype),
        grid=(m // tm, n // tn, nk),
        in_specs=[pl.BlockSpec((tm, tk), lambda i, j, k: (i, k)),
                  pl.BlockSpec((tk, tn), lambda i, j, k: (k, j))],
        out_specs=pl.BlockSpec((tm, tn), lambda i, j, k: (i, j)),
        scratch_shapes=[] if nk == 1 else [pltpu.VMEM((tm, tn), F32)],
        compiler_params=_cparams(("parallel", "parallel", "arbitrary")),
    )(a, b)


def _rope_kernel(pos_ref, inv_ref, sign_ref, cos_ref, sin_ref):
    ang = pos_ref[...] * inv_ref[...]
    cos_ref[...] = jnp.cos(ang)
    sin_ref[...] = jnp.sin(ang) * sign_ref[...]


def rope_tables(pos_col, tl):
    l = pos_col.shape[0]
    half = np.arange(0, HEAD_DIM, 2, dtype=np.float32) / np.float32(HEAD_DIM)
    inv = (np.float32(ROPE_BASE) ** (-half)).astype(np.float32)
    inv2 = jnp.asarray(np.concatenate([inv, inv])[None, :])
    sign = jnp.asarray(np.concatenate([-np.ones(64, np.float32), np.ones(64, np.float32)])[None, :])
    row = pl.BlockSpec((1, LANES), lambda i: (0, 0))
    return pl.pallas_call(
        _rope_kernel,
        out_shape=[jax.ShapeDtypeStruct((l, LANES), F32)] * 2,
        grid=(l // tl,),
        in_specs=[pl.BlockSpec((tl, 1), lambda i: (i, 0)), row, row],
        out_specs=[pl.BlockSpec((tl, LANES), lambda i: (i, 0))] * 2,
        compiler_params=_cparams(("parallel",)),
    )(pos_col, inv2, sign)


def _head_norm(y):
    mu = jnp.mean(y, axis=-1, keepdims=True)
    d = y - mu
    var = jnp.mean(d * d, axis=-1, keepdims=True)
    return d * lax.rsqrt(var + LN_EPS)


def _sigmoid(x):
    return 1.0 / (1.0 + jnp.exp(-x))


def _dot_nt(a, b):
    return lax.dot_general(a, b, (((1,), (1,)), ((), ())), preferred_element_type=F32)


def _dot(a, b):
    return jnp.dot(a, b, preferred_element_type=F32)


def _ret_kernel(q_ref, k_ref, v_ref, g_ref, cos_ref, sin_ref, w_ref, o_ref, state_ref, decay_ref, *, nchunk):
    h = pl.program_id(0)
    rb = pl.program_id(1)
    hv = jnp.full((1, 1), h, jnp.int32).astype(F32)
    lg = jnp.log(1.0 - jnp.exp2(-5.0 - hv))

    @pl.when(rb == 0)
    def _():
        state_ref[...] = jnp.zeros_like(state_ref)
        ii = lax.broadcasted_iota(jnp.int32, (CHUNK, CHUNK), 0)
        jj = lax.broadcasted_iota(jnp.int32, (CHUNK, CHUNK), 1)
        rel = (ii - jj).astype(F32)
        decay_ref[...] = jnp.where(rel >= 0.0, jnp.exp(lg * jnp.maximum(rel, 0.0)), 0.0)

    idx = lax.broadcasted_iota(jnp.int32, (CHUNK, 1), 0).astype(F32)
    q_dec = jnp.exp(lg * (idx + 1.0))
    k_dec = jnp.exp(lg * (CHUNK - 1.0 - idx))
    c_dec = jnp.exp(lg * float(CHUNK))

    def chunk(ci, carry):
        rows = pl.ds(pl.multiple_of(ci * CHUNK, CHUNK), CHUNK)
        q = q_ref[rows, :]
        k = k_ref[rows, :]
        cos = cos_ref[rows, :]
        sin = sin_ref[rows, :]
        qr = q * cos + pltpu.roll(q, HEAD_DIM // 2, 1) * sin
        kr = (k * cos + pltpu.roll(k, HEAD_DIM // 2, 1) * sin) * QK_SCALE
        qb = qr.astype(BF16)
        vb = v_ref[rows, :].astype(BF16)
        scores = _dot_nt(qb, kr.astype(BF16)) * decay_ref[...]
        st = state_ref[...]
        out = _dot(scores.astype(BF16), vb) + _dot(qb, st.astype(BF16)) * q_dec
        kd_t = jnp.transpose(kr * k_dec).astype(BF16)
        state_ref[...] = st * c_dec + _dot(kd_t, vb)
        g = g_ref[rows, :]
        y = _head_norm(out) * w_ref[...] * (g * _sigmoid(g))
        o_ref[rows, :] = y.astype(o_ref.dtype)
        return carry

    lax.fori_loop(0, nchunk, chunk, 0)


def retention_block(proj, cos_t, sin_t, norm_w, tb):
    l = proj.shape[0]
    nh = RET_HEADS

    def col(off):
        return pl.BlockSpec((tb, LANES), lambda h, r, off=off: (r, off + h))

    tab = pl.BlockSpec((tb, LANES), lambda h, r: (r, 0))
    return pl.pallas_call(
        functools.partial(_ret_kernel, nchunk=tb // CHUNK),
        out_shape=jax.ShapeDtypeStruct((l, RET_WIDTH), BF16),
        grid=(nh, l // tb),
        in_specs=[col(0), col(nh), col(2 * nh), col(3 * nh), tab, tab,
                  pl.BlockSpec((1, LANES), lambda h, r: (0, h))],
        out_specs=pl.BlockSpec((tb, LANES), lambda h, r: (r, h)),
        scratch_shapes=[pltpu.VMEM((HEAD_DIM, HEAD_DIM), F32), pltpu.VMEM((CHUNK, CHUNK), F32)],
        compiler_params=_cparams(("parallel", "arbitrary")),
    )(proj, proj, proj, proj, cos_t, sin_t, norm_w)


def _log_sigmoid(x):
    return -(jnp.maximum(-x, 0.0) + jnp.log1p(jnp.exp(-jnp.abs(x))))


def _mlstm_kernel(q_ref, k_ref, v_ref, og_ref, gate_ref, cwq_ref, cwk_ref, cbq_ref, cbk_ref, gb_ref, nw_ref,
                  o_ref, c_st, n_st, m_st, qbuf, kbuf, qs, ks, gs, *, nchunk, tb):
    h = pl.program_id(0)
    rb = pl.program_id(1)
    pad = 8

    @pl.when(rb == 0)
    def _():
        c_st[...] = jnp.zeros_like(c_st)
        n_st[...] = jnp.zeros_like(n_st)
        m_st[...] = jnp.zeros_like(m_st)
        qbuf[0:pad, :] = jnp.zeros((pad, LANES), F32)
        kbuf[0:pad, :] = jnp.zeros((pad, LANES), F32)

    qbuf[pad:pad + tb, :] = q_ref[...]
    kbuf[pad:pad + tb, :] = k_ref[...]
    for buf, cw, cb, dst, scale in ((qbuf, cwq_ref, cbq_ref, qs, 1.0), (kbuf, cwk_ref, cbk_ref, ks, QK_SCALE)):
        acc = jnp.broadcast_to(cb[...], (tb, LANES))
        for tap in range(CONV_WIDTH):
            off = pad - (CONV_WIDTH - 1) + tap
            acc = acc + buf[off:off + tb, :] * cw[tap:tap + 1, :]
        act = acc * _sigmoid(acc)
        dst[...] = act * scale if scale != 1.0 else act
    qbuf[0:pad, :] = q_ref[tb - pad:tb, :]
    kbuf[0:pad, :] = k_ref[tb - pad:tb, :]

    lane_row = lax.broadcasted_iota(jnp.int32, (1, LANES), 1)
    graw = gate_ref[...] + gb_ref[...]
    gs[...] = jnp.where(lane_row < MLSTM_HEADS, graw, _log_sigmoid(graw))

    ii = lax.broadcasted_iota(jnp.int32, (CHUNK, CHUNK), 0)
    jj = lax.broadcasted_iota(jnp.int32, (CHUNK, CHUNK), 1)
    causal = ii >= jj
    tril = causal.astype(F32)

    def chunk(ci, carry):
        rows = pl.ds(pl.multiple_of(ci * CHUNK, CHUNK), CHUNK)
        qc = qs[rows, :]
        kc = ks[rows, :]
        vb = v_ref[rows, :].astype(BF16)
        g = gs[rows, :]
        gcum = jnp.dot(tril, g, precision=lax.Precision.HIGHEST, preferred_element_type=F32)
        ic_c = jnp.sum(jnp.where(jj == h, g, 0.0), axis=1, keepdims=True)
        bc_c = jnp.sum(jnp.where(jj == h + MLSTM_HEADS, gcum, 0.0), axis=1, keepdims=True)
        pt = jnp.transpose(jnp.where(jj == 0, ic_c, jnp.where(jj == 1, bc_c, 0.0)))
        ic_r = pt[0:1, :]
        bc_r = pt[1:2, :]
        m_prev = m_st[0:1, 0:1]
        log_d = jnp.where(causal, bc_c - bc_r + ic_r, -jnp.inf)
        log_inter = bc_c + m_prev
        m_row = jnp.maximum(jnp.max(log_d, axis=1, keepdims=True), log_inter)
        qb = qc.astype(BF16)
        s = _dot_nt(qb, kc.astype(BF16)) * jnp.exp(log_d - m_row)
        w_inter = jnp.exp(log_inter - m_row)
        cs = c_st[...]
        ns = n_st[0:1, :]
        num = _dot(s.astype(BF16), vb) + w_inter * _dot(qb, cs.astype(BF16))
        den = jnp.sum(s, axis=1, keepdims=True) + w_inter * jnp.sum(qc * ns, axis=1, keepdims=True)
        hout = num / jnp.maximum(jnp.abs(den), jnp.exp(-m_row))
        b_last = bc_r[:, CHUNK - 1:CHUNK]
        m_new = jnp.maximum(b_last + m_prev, jnp.max(b_last - bc_r + ic_r, axis=1, keepdims=True))
        keep = jnp.exp(b_last + m_prev - m_new)
        kw = kc * jnp.exp(b_last - bc_c + ic_c - m_new)
        c_st[...] = keep * cs + _dot(jnp.transpose(kw).astype(BF16), vb)
        n_st[0:1, :] = keep * ns + jnp.sum(kw, axis=0, keepdims=True)
        m_st[...] = jnp.broadcast_to(m_new, m_st.shape)
        y = _head_norm(hout) * nw_ref[...] * _sigmoid(og_ref[rows, :])
        o_ref[rows, :] = y.astype(o_ref.dtype)
        return carry

    lax.fori_loop(0, nchunk, chunk, 0)


def mlstm_block(proj, conv_w, conv_b, gate_bias, norm_w, tb):
    l = proj.shape[0]
    nh = MLSTM_HEADS
    base = 4 * RET_HEADS

    def col(off):
        return pl.BlockSpec((tb, LANES), lambda h, r, off=off: (r, off + h))

    def par(rows, off):
        return pl.BlockSpec((rows, LANES), lambda h, r, off=off: (0, off + h))

    return pl.pallas_call(
        functools.partial(_mlstm_kernel, nchunk=tb // CHUNK, tb=tb),
        out_shape=jax.ShapeDtypeStruct((l, MLSTM_WIDTH), BF16),
        grid=(nh, l // tb),
        in_specs=[col(base), col(base + nh), col(base + 2 * nh), col(base + 3 * nh),
                  pl.BlockSpec((tb, LANES), lambda h, r: (r, GATE_COLBLK)),
                  par(CONV_WIDTH, 0), par(CONV_WIDTH, nh), par(1, 0), par(1, nh),
                  pl.BlockSpec((1, LANES), lambda h, r: (0, 0)),
                  par(1, 0)],
        out_specs=pl.BlockSpec((tb, LANES), lambda h, r: (r, h)),
        scratch_shapes=[pltpu.VMEM((HEAD_DIM, HEAD_DIM), F32), pltpu.VMEM((8, LANES), F32),
                        pltpu.VMEM((8, LANES), F32),
                        pltpu.VMEM((tb + 8, LANES), F32), pltpu.VMEM((tb + 8, LANES), F32),
                        pltpu.VMEM((tb, LANES), F32), pltpu.VMEM((tb, LANES), F32), pltpu.VMEM((tb, LANES), F32)],
        compiler_params=_cparams(("parallel", "arbitrary")),
    )(proj, proj, proj, proj, proj, conv_w, conv_w, conv_b, conv_b, gate_bias, norm_w)


def s5_operators(lam_re, lam_im, log_dt, b_re, b_im, c_re, c_im):
    f32 = F32
    lam_re, lam_im = lam_re.astype(f32), lam_im.astype(f32)
    b_re, b_im, c_re, c_im = b_re.astype(f32), b_im.astype(f32), c_re.astype(f32), c_im.astype(f32)
    dt = jnp.exp(log_dt.astype(f32))[:, None]
    mag = jnp.exp(lam_re * dt)
    ang = lam_im * dt
    lb_re, lb_im = mag * jnp.cos(ang), mag * jnp.sin(ang)
    zr, zi = lb_re - 1.0, lb_im
    den = jnp.square(lam_re) + jnp.square(lam_im)
    w_re = (zr * lam_re + zi * lam_im) / den
    w_im = (zi * lam_re - zr * lam_im) / den
    bb_re = w_re[..., None] * b_re - w_im[..., None] * b_im
    bb_im = w_re[..., None] * b_im + w_im[..., None] * b_re
    d = jnp.arange(S5_T + 1, dtype=f32)[:, None, None]
    pm = jnp.exp(lam_re * dt * d)
    pa = lam_im * dt * d
    p_re, p_im = pm * jnp.cos(pa), pm * jnp.sin(pa)
    ab_re = p_re[..., None] * bb_re - p_im[..., None] * bb_im
    ab_im = p_re[..., None] * bb_im + p_im[..., None] * bb_re
    hp = lax.Precision.HIGHEST
    kern = (jnp.einsum('gop,dgpi->dgio', c_re, ab_re[:S5_T], precision=hp)
            - jnp.einsum('gop,dgpi->dgio', c_im, ab_im[:S5_T], precision=hp))
    gb, gp = S5_GB, S5_GPB
    eye = jnp.eye(gp, dtype=f32)

    def blockdiag(m):
        lead = m.shape[:-3]
        a, b = m.shape[-2:]
        m = m.reshape(lead + (gb, gp, a, b))
        out = jnp.einsum('...gkab,kj->...gkajb', m, eye)
        return out.reshape(lead + (gb, gp * a, gp * b))

    kblk = blockdiag(kern)
    s_idx = jnp.arange(S5_T)
    lag = s_idx[None, :] - s_idx[:, None]
    toep = jnp.where((lag >= 0)[:, :, None, None, None], kblk[jnp.clip(lag, 0, S5_T - 1)], 0.0)
    toep = toep.transpose(2, 0, 3, 1, 4).reshape(gb, S5_T * LANES, S5_T * LANES)
    inj_re = jnp.swapaxes(ab_re[:S5_T][::-1], -1, -2)
    inj_im = jnp.swapaxes(ab_im[:S5_T][::-1], -1, -2)
    v_re = blockdiag(inj_re).transpose(1, 0, 2, 3).reshape(gb, S5_T * LANES, S5_SB)
    v_im = blockdiag(inj_im).transpose(1, 0, 2, 3).reshape(gb, S5_T * LANES, S5_SB)
    e_re = c_re[None] * p_re[1:, :, None, :] - c_im[None] * p_im[1:, :, None, :]
    e_im = c_re[None] * p_im[1:, :, None, :] + c_im[None] * p_re[1:, :, None, :]
    o_re = blockdiag(jnp.swapaxes(e_re, -1, -2))
    o_im = blockdiag(jnp.swapaxes(-e_im, -1, -2))
    o_re = o_re.transpose(1, 2, 0, 3).reshape(gb, S5_SB, S5_T * LANES)
    o_im = o_im.transpose(1, 2, 0, 3).reshape(gb, S5_SB, S5_T * LANES)
    a_re = p_re[S5_T].reshape(1, S5_GROUPS * S5_STATE)
    a_im = p_im[S5_T].reshape(1, S5_GROUPS * S5_STATE)
    return (toep.astype(BF16), v_re.astype(BF16), v_im.astype(BF16), o_re.astype(BF16), o_im.astype(BF16),
            a_re, a_im)


def _gelu_tanh(x):
    c = math.sqrt(2.0 / math.pi)
    return 0.5 * x * (1.0 + jnp.tanh(c * (x + 0.044715 * (x * x * x))))


def _s5_kernel(u_ref, toep_ref, vre_ref, vim_ref, wre_ref, wim_ref, are_ref, aim_ref, d_ref, y_ref,
               car_re, car_im, z_re, z_im, x_re, x_im, *, tmc):
    rb = pl.program_id(1)

    @pl.when(rb == 0)
    def _():
        car_re[...] = jnp.zeros_like(car_re)
        car_im[...] = jnp.zeros_like(car_im)

    us = [u_ref[pl.ds(s, tmc, stride=S5_T), :] for s in range(S5_T)]
    ucat = jnp.concatenate([u.astype(BF16) for u in us], axis=1)
    z_re[...] = _dot(ucat, vre_ref[0])
    z_im[...] = _dot(ucat, vim_ref[0])
    a_r = are_ref[...]
    a_i = aim_ref[...]

    def step(c, carry):
        xr, xi = carry
        row = pl.ds(c, 1)
        x_re[row, :] = xr
        x_im[row, :] = xi
        return (a_r * xr - a_i * xi + z_re[row, :], a_r * xi + a_i * xr + z_im[row, :])

    xr, xi = lax.fori_loop(0, tmc, step, (car_re[0:1, :], car_im[0:1, :]))
    car_re[0:1, :] = xr
    car_im[0:1, :] = xi
    y = (_dot(ucat, toep_ref[0]) + _dot(x_re[...].astype(BF16), wre_ref[0])
         + _dot(x_im[...].astype(BF16), wim_ref[0]))
    for t in range(S5_T):
        yt = y[:, t * LANES:(t + 1) * LANES] + d_ref[...] * us[t]
        y_ref[pl.ds(t, tmc, stride=S5_T), :] = _gelu_tanh(yt)


def s5_block(proj, ops, d_skip, tb):
    l = proj.shape[0]
    toep, v_re, v_im, w_re, w_im, a_re, a_im = ops
    tmc = tb // S5_T
    kt = S5_T * LANES

    def per_gb(shape):
        return pl.BlockSpec((1,) + shape, lambda g, r: (g, 0, 0))

    lane_blk = pl.BlockSpec((1, S5_SB), lambda g, r: (0, g))
    return pl.pallas_call(
        functools.partial(_s5_kernel, tmc=tmc),
        out_shape=jax.ShapeDtypeStruct((l, S5_WIDTH), F32),
        grid=(S5_GB, l // tb),
        in_specs=[pl.BlockSpec((tb, LANES), lambda g, r: (r, S5_COLBLK + g)),
                  per_gb((kt, kt)), per_gb((kt, S5_SB)), per_gb((kt, S5_SB)),
                  per_gb((S5_SB, kt)), per_gb((S5_SB, kt)), lane_blk, lane_blk,
                  pl.BlockSpec((1, LANES), lambda g, r: (0, g))],
        out_specs=pl.BlockSpec((tb, LANES), lambda g, r: (r, g)),
        scratch_shapes=[pltpu.VMEM((8, S5_SB), F32), pltpu.VMEM((8, S5_SB), F32),
                        pltpu.VMEM((tmc, S5_SB), F32), pltpu.VMEM((tmc, S5_SB), F32),
                        pltpu.VMEM((tmc, S5_SB), F32), pltpu.VMEM((tmc, S5_SB), F32)],
        compiler_params=_cparams(("parallel", "arbitrary")),
    )(proj, toep, v_re, v_im, w_re, w_im, a_re, a_im, d_skip)


def _glu_kernel(y_ref, w_ref, b_ref, o_ref):
    y = y_ref[...]
    z = _dot(y.astype(BF16), w_ref[...]) + b_ref[...]
    o_ref[...] = (y * _sigmoid(z)).astype(o_ref.dtype)


def glu_block(y, w, b, tm):
    l, n = y.shape
    return pl.pallas_call(
        _glu_kernel,
        out_shape=jax.ShapeDtypeStruct((l, n), BF16),
        grid=(l // tm,),
        in_specs=[pl.BlockSpec((tm, n), lambda i: (i, 0)), pl.BlockSpec((n, n), lambda i: (0, 0)),
                  pl.BlockSpec((1, n), lambda i: (0, 0))],
        out_specs=pl.BlockSpec((tm, n), lambda i: (i, 0)),
        compiler_params=_cparams(("parallel",)),
    )(y, w, b)


def _layer_norm(z, w, b):
    mu = jnp.mean(z, axis=-1, keepdims=True)
    d = z - mu
    var = jnp.mean(d * d, axis=-1, keepdims=True)
    return d * lax.rsqrt(var + LN_EPS) * w + b


def _outproj_kernel(a1_ref, a2_ref, a3_ref, w1_ref, w2_ref, w3_ref, x_ref, lw_ref, lb_ref, o_ref, ob_ref):
    mix = _dot(a1_ref[...], w1_ref[...]) + _dot(a2_ref[...], w2_ref[...]) + _dot(a3_ref[...], w3_ref[...])
    x1 = _layer_norm(ALPHA * x_ref[...] + mix, lw_ref[...], lb_ref[...])
    o_ref[...] = x1
    ob_ref[...] = x1.astype(BF16)


def outproj_block(y_ret, y_m, y_s, w1, w2, w3, x, ln_w, ln_b, tm):
    l = x.shape[0]

    def rows(n):
        return pl.BlockSpec((tm, n), lambda i: (i, 0))

    def whole(a):
        return pl.BlockSpec(a.shape, lambda i: (0, 0))

    return pl.pallas_call(
        _outproj_kernel,
        out_shape=[jax.ShapeDtypeStruct((l, D_MODEL), F32), jax.ShapeDtypeStruct((l, D_MODEL), BF16)],
        grid=(l // tm,),
        in_specs=[rows(RET_WIDTH), rows(MLSTM_WIDTH), rows(S5_WIDTH), whole(w1), whole(w2), whole(w3),
                  rows(D_MODEL), whole(ln_w), whole(ln_b)],
        out_specs=[rows(D_MODEL), rows(D_MODEL)],
        compiler_params=_cparams(("parallel",)),
    )(y_ret, y_m, y_s, w1, w2, w3, x, ln_w, ln_b)


def _down_kernel(h_ref, w_ref, x1_ref, r_ref, rb_ref, acc_ref, *, nk):
    k = pl.program_id(1)

    @pl.when(k == 0)
    def _():
        acc_ref[...] = jnp.zeros_like(acc_ref)

    acc_ref[...] += _dot(h_ref[...], w_ref[...])

    @pl.when(k == nk - 1)
    def _():
        r = ALPHA * x1_ref[...] + acc_ref[...]
        r_ref[...] = r
        rb_ref[...] = r.astype(BF16)


def down_block(hid, w_down, x1, tm, tk):
    l, ff = hid.shape
    nk = ff // tk
    rows = pl.BlockSpec((tm, D_MODEL), lambda i, k: (i, 0))
    return pl.pallas_call(
        functools.partial(_down_kernel, nk=nk),
        out_shape=[jax.ShapeDtypeStruct((l, D_MODEL), F32), jax.ShapeDtypeStruct((l, D_MODEL), BF16)],
        grid=(l // tm, nk),
        in_specs=[pl.BlockSpec((tm, tk), lambda i, k: (i, k)), pl.BlockSpec((tk, D_MODEL), lambda i, k: (k, 0)), rows],
        out_specs=[rows, rows],
        scratch_shapes=[pltpu.VMEM((tm, D_MODEL), F32)],
        compiler_params=_cparams(("parallel", "arbitrary")),
    )(hid, w_down, x1)


def _final_kernel(r_ref, rb_ref, p_ref, wg_ref, wp_ref, lw_ref, lb_ref, o_ref, ob_ref):
    gate = _sigmoid(_dot(rb_ref[...], wg_ref[...]))
    ple = gate * _dot(p_ref[...], wp_ref[...])
    out = _layer_norm(r_ref[...] + ple, lw_ref[...], lb_ref[...])
    o_ref[...] = out
    ob_ref[...] = out.astype(BF16)


def final_block(r, rb, p_b, w_gate, w_ple, ln_w, ln_b, tm):
    l = r.shape[0]
    rows = pl.BlockSpec((tm, D_MODEL), lambda i: (i, 0))

    def whole(a):
        return pl.BlockSpec(a.shape, lambda i: (0, 0))

    return pl.pallas_call(
        _final_kernel,
        out_shape=[jax.ShapeDtypeStruct((l, D_MODEL), F32), jax.ShapeDtypeStruct((l, D_MODEL), BF16)],
        grid=(l // tm,),
        in_specs=[rows, rows, pl.BlockSpec((tm, PLE_DIM), lambda i: (i, 0)), whole(w_gate), whole(w_ple),
                  whole(ln_w), whole(ln_b)],
        out_specs=[rows, rows],
        compiler_params=_cparams(("parallel",)),
    )(r, rb, p_b, w_gate, w_ple, ln_w, ln_b)


def _pad_in_proj(w_in):
    main = w_in[:, :MAIN_WIDTH]
    gates = w_in[:, MAIN_WIDTH:MAIN_WIDTH + 2 * MLSTM_HEADS]
    s5 = w_in[:, MAIN_WIDTH + 2 * MLSTM_HEADS:]
    zeros = jnp.zeros((w_in.shape[0], PROJ_WIDTH - GATE_COL - 2 * MLSTM_HEADS), w_in.dtype)
    return jnp.concatenate([main, s5, gates, zeros], axis=1).astype(BF16)


def _row(v):
    return v.astype(F32).reshape(1, -1)


def _hybrid_layer(x, xb, cos_t, sin_t, p_b, w_in, conv_w, conv_b, i_bias, f_bias, ret_norm_w, mlstm_norm_w,
                  lam_re, lam_im, log_dt, b_re, b_im, c_re, c_im, s5_d, glu_w, glu_b,
                  w_out, ln1_w, ln1_b, w_up, w_down, w_gate, w_ple, ln2_w, ln2_b, tiles):
    proj = matmul(xb, _pad_in_proj(w_in), out_dtype=F32, tm=tiles["mm_m"], tn=768)
    y_ret = retention_block(proj, cos_t, sin_t, _row(ret_norm_w), tiles["mix"])
    gate_bias = jnp.concatenate([i_bias.astype(F32), f_bias.astype(F32),
                                 jnp.zeros((LANES - 2 * MLSTM_HEADS,), F32)]).reshape(1, LANES)
    y_m = mlstm_block(proj, conv_w.astype(F32), _row(conv_b), gate_bias, _row(mlstm_norm_w), tiles["mix"])
    ops = s5_operators(lam_re, lam_im, log_dt, b_re, b_im, c_re, c_im)
    y_s = s5_block(proj, ops, _row(s5_d), tiles["s5"])
    y_s = glu_block(y_s, glu_w.astype(BF16), _row(glu_b), tiles["row"])
    wo = w_out.astype(BF16)
    x1, x1b = outproj_block(y_ret, y_m, y_s, wo[:RET_WIDTH], wo[RET_WIDTH:RET_WIDTH + MLSTM_WIDTH],
                            wo[RET_WIDTH + MLSTM_WIDTH:], x, _row(ln1_w), _row(ln1_b), tiles["row"])
    hid = matmul(x1b, w_up.astype(BF16), out_dtype=BF16, tm=tiles["mm_m"], tn=1024, act="relu2")
    r, rb = down_block(hid, w_down.astype(BF16), x1, tiles["row"], 1024)
    return final_block(r, rb, p_b, w_gate.astype(BF16), w_ple.astype(BF16), _row(ln2_w), _row(ln2_b), tiles["row"])


def _tiles(l):
    return {"mm_m": min(1024, l), "mix": min(1024, l), "s5": min(2048, l), "row": min(512, l), "rope": min(2048, l)}


def kernel(x, p, positions, w_in, mlstm_conv_w, mlstm_conv_b, mlstm_i_bias, mlstm_f_bias, ret_norm_w, mlstm_norm_w, s5_lambda_re, s5_lambda_im, s5_log_dt, s5_B_re, s5_B_im, s5_C_re, s5_C_im, s5_D, s5_glu_w, s5_glu_b, w_out, ln1_w, ln1_b, w_up, w_down, w_gate, w_ple, ln2_w, ln2_b):
    bsz, l, _ = x.shape
    assert bsz == 1
    tiles = _tiles(l)
    cos_t, sin_t = rope_tables(positions.astype(F32).reshape(l, 1), tiles["rope"])
    xf = x.reshape(l, D_MODEL).astype(F32)
    xb = xf.astype(BF16)
    for i in range(w_in.shape[0]):
        xf, xb = _hybrid_layer(
            xf, xb, cos_t, sin_t, p[i].reshape(l, PLE_DIM).astype(BF16), w_in[i], mlstm_conv_w[i], mlstm_conv_b[i],
            mlstm_i_bias[i], mlstm_f_bias[i], ret_norm_w[i], mlstm_norm_w[i], s5_lambda_re[i], s5_lambda_im[i],
            s5_log_dt[i], s5_B_re[i], s5_B_im[i], s5_C_re[i], s5_C_im[i], s5_D[i], s5_glu_w[i], s5_glu_b[i],
            w_out[i], ln1_w[i], ln1_b[i], w_up[i], w_down[i], w_gate[i], w_ple[i], ln2_w[i], ln2_b[i], tiles)
    return xf.reshape(bsz, l, D_MODEL)
```

```python
import functools
import math

import jax
import jax.numpy as jnp
import numpy as np
from jax import lax
from jax.experimental import pallas as pl
from jax.experimental.pallas import tpu as pltpu

F32 = jnp.float32
BF16 = jnp.bfloat16

D_MODEL = 2048
DEPTH = 2
HEAD_DIM = 128
RET_HEADS = 6
MLSTM_HEADS = 6
RET_WIDTH = RET_HEADS * HEAD_DIM
MLSTM_WIDTH = MLSTM_HEADS * HEAD_DIM
S5_WIDTH = D_MODEL - RET_WIDTH - MLSTM_WIDTH
S5_GROUP = 16
S5_GROUPS = S5_WIDTH // S5_GROUP
S5_STATE = 64
CONV_WIDTH = 4
CHUNK = 128
D_FF = 4 * D_MODEL
PLE_DIM = 256
ROPE_BASE = 10000.0
LN_EPS = 1e-5
ALPHA = (2 * DEPTH) ** 0.25
QK_SCALE = HEAD_DIM ** -0.5

LANES = 128
SUBLANES = 8
MAIN_WIDTH = 4 * RET_WIDTH + 4 * MLSTM_WIDTH
GATE_COL = MAIN_WIDTH + S5_WIDTH
PROJ_WIDTH = GATE_COL + 2 * LANES
S5_COLBLK = MAIN_WIDTH // LANES
GATE_COLBLK = GATE_COL // LANES
S5_T = 16
S5_GB = S5_WIDTH // LANES
S5_GPB = LANES // S5_GROUP
S5_SB = S5_GPB * S5_STATE
S5_KT = S5_T * LANES
VMEM_LIMIT = 56 * 1024 * 1024


def _cparams(sem):
    return pltpu.CompilerParams(dimension_semantics=sem, vmem_limit_bytes=VMEM_LIMIT)


def _mm_kernel(a_ref, b_ref, o_ref, *scratch, nk, act):
    def finish(r):
        if act == "relu2":
            r = jnp.square(jnp.maximum(r, 0.0))
        o_ref[...] = r.astype(o_ref.dtype)

    if nk == 1:
        finish(jnp.dot(a_ref[...], b_ref[...], preferred_element_type=F32))
        return
    acc_ref, = scratch
    k = pl.program_id(2)

    @pl.when(k == 0)
    def _():
        acc_ref[...] = jnp.zeros_like(acc_ref)

    acc_ref[...] += jnp.dot(a_ref[...], b_ref[...], preferred_element_type=F32)

    @pl.when(k == nk - 1)
    def _():
        finish(acc_ref[...])


def matmul(a, b, *, out_dtype, tm, tn, tk=None, act=None):
    m, kdim = a.shape
    _, n = b.shape
    tk = kdim if tk is None else tk
    nk = kdim // tk
    assert m % tm == 0 and n % tn == 0 and kdim % tk == 0
    return pl.pallas_call(
        functools.partial(_mm_kernel, nk=nk, act=act),
        out_shape=jax.ShapeDtypeStruct((m, n), out_dtype),
        grid=(m // tm, n // tn, nk),
        in_specs=[pl.BlockSpec((tm, tk), lambda i, j, k: (i, k)),
                  pl.BlockSpec((tk, tn), lambda i, j, k: (k, j))],
        out_specs=pl.BlockSpec((tm, tn), lambda i, j, k: (i, j)),
        scratch_shapes=[] if nk == 1 else [pltpu.VMEM((tm, tn), F32)],
        compiler_params=_cparams(("parallel", "parallel", "arbitrary")),
    )(a, b)


def _rope_kernel(pos_ref, inv_ref, sign_ref, cos_ref, sin_ref):
    ang = pos_ref[...] * inv_ref[...]
    cos_ref[...] = jnp.cos(ang)
    sin_ref[...] = jnp.sin(ang) * sign_ref[...]


def rope_tables(pos_col, tl):
    l = pos_col.shape[0]
    half = np.arange(0, HEAD_DIM, 2, dtype=np.float32) / np.float32(HEAD_DIM)
    inv = (np.float32(ROPE_BASE) ** (-half)).astype(np.float32)
    inv2 = jnp.asarray(np.concatenate([inv, inv])[None, :])
    sign = jnp.asarray(np.concatenate([-np.ones(64, np.float32), np.ones(64, np.float32)])[None, :])
    row = pl.BlockSpec((1, LANES), lambda i: (0, 0))
    return pl.pallas_call(
        _rope_kernel,
        out_shape=[jax.ShapeDtypeStruct((l, LANES), F32)] * 2,
        grid=(l // tl,),
        in_specs=[pl.BlockSpec((tl, 1), lambda i: (i, 0)), row, row],
        out_specs=[pl.BlockSpec((tl, LANES), lambda i: (i, 0))] * 2,
        compiler_params=_cparams(("parallel",)),
    )(pos_col, inv2, sign)


def _head_norm(y):
    mu = jnp.mean(y, axis=-1, keepdims=True)
    d = y - mu
    var = jnp.mean(d * d, axis=-1, keepdims=True)
    return d * lax.rsqrt(var + LN_EPS)


def _sigmoid(x):
    return 1.0 / (1.0 + jnp.exp(-x))


def _dot_nt(a, b, precision=None):
    return lax.dot_general(a, b, (((1,), (1,)), ((), ())), precision=precision, preferred_element_type=F32)


def _dot(a, b):
    return jnp.dot(a, b, preferred_element_type=F32)


def _head_lanes(h):
    return slice(h * HEAD_DIM, (h + 1) * HEAD_DIM)


def _ret_log_gamma(h):
    return float(np.log(np.float32(1.0) - np.float32(2.0) ** np.float32(-5.0 - h)))


def _ret_kernel(q_ref, k_ref, v_ref, g_ref, cos_ref, sin_ref, w_ref, o_ref,
                state_ref, decay_ref, qdec_ref, kdec_ref, *, nchunk):
    rb = pl.program_id(0)

    @pl.when(rb == 0)
    def _():
        state_ref[...] = jnp.zeros_like(state_ref)
        ii = lax.broadcasted_iota(jnp.int32, (CHUNK, CHUNK), 0)
        jj = lax.broadcasted_iota(jnp.int32, (CHUNK, CHUNK), 1)
        rel = (ii - jj).astype(F32)
        idx = ii.astype(F32)
        for h in range(RET_HEADS):
            lg = _ret_log_gamma(h)
            decay_ref[h] = jnp.where(rel >= 0.0, jnp.exp(lg * jnp.maximum(rel, 0.0)), 0.0)
            qdec_ref[h] = jnp.exp(lg * (idx + 1.0))
            kdec_ref[h] = jnp.exp(lg * (CHUNK - 1.0 - idx))

    def chunk(ci, carry):
        rows = pl.ds(pl.multiple_of(ci * CHUNK, CHUNK), CHUNK)
        cos = cos_ref[rows, :]
        sin = sin_ref[rows, :]
        for h in range(RET_HEADS):
            ln = _head_lanes(h)
            q = q_ref[rows, ln]
            k = k_ref[rows, ln]
            qr = q * cos + pltpu.roll(q, HEAD_DIM // 2, 1) * sin
            kr = (k * cos + pltpu.roll(k, HEAD_DIM // 2, 1) * sin) * QK_SCALE
            qb = qr.astype(BF16)
            vb = v_ref[rows, ln].astype(BF16)
            scores = _dot_nt(qb, kr.astype(BF16)) * decay_ref[h]
            st = state_ref[h]
            out = _dot(scores.astype(BF16), vb) + _dot(qb, st.astype(BF16)) * qdec_ref[h]
            kd_t = jnp.transpose(kr * kdec_ref[h]).astype(BF16)
            state_ref[h] = st * math.exp(_ret_log_gamma(h) * CHUNK) + _dot(kd_t, vb)
            g = g_ref[rows, ln]
            y = _head_norm(out) * w_ref[:, ln] * (g * _sigmoid(g))
            o_ref[rows, ln] = y.astype(o_ref.dtype)
        return carry

    lax.fori_loop(0, nchunk, chunk, 0)


def retention_block(proj, cos_t, sin_t, norm_w, tb):
    l = proj.shape[0]

    def col(j):
        return pl.BlockSpec((tb, RET_WIDTH), lambda r, j=j: (r, j))

    tab = pl.BlockSpec((tb, LANES), lambda r: (r, 0))
    hsq = pltpu.VMEM((RET_HEADS, CHUNK, CHUNK), F32)
    return pl.pallas_call(
        functools.partial(_ret_kernel, nchunk=tb // CHUNK),
        out_shape=jax.ShapeDtypeStruct((l, RET_WIDTH), BF16),
        grid=(l // tb,),
        in_specs=[col(0), col(1), col(2), col(3), tab, tab, pl.BlockSpec((1, RET_WIDTH), lambda r: (0, 0))],
        out_specs=pl.BlockSpec((tb, RET_WIDTH), lambda r: (r, 0)),
        scratch_shapes=[hsq, hsq, hsq, hsq],
        compiler_params=_cparams(("arbitrary",)),
    )(proj, proj, proj, proj, cos_t, sin_t, norm_w)


def _log_sigmoid(x):
    return -(jnp.maximum(-x, 0.0) + jnp.log1p(jnp.exp(-jnp.abs(x))))


def _mlstm_kernel(q_ref, k_ref, v_ref, og_ref, gate_ref, cw_ref, cb_ref, gb_ref, nw_ref,
                  o_ref, c_st, n_st, m_st, qbuf, kbuf, qs, ks, gs, *, nchunk, tb):
    rb = pl.program_id(0)
    pad = SUBLANES
    nh = MLSTM_HEADS

    @pl.when(rb == 0)
    def _():
        c_st[...] = jnp.zeros_like(c_st)
        n_st[...] = jnp.zeros_like(n_st)
        m_st[...] = jnp.zeros_like(m_st)
        qbuf[0:pad, :] = jnp.zeros((pad, MLSTM_WIDTH), F32)
        kbuf[0:pad, :] = jnp.zeros((pad, MLSTM_WIDTH), F32)

    qbuf[pad:pad + tb, :] = q_ref[...]
    kbuf[pad:pad + tb, :] = k_ref[...]
    for buf, woff, dst, scale in ((qbuf, 0, qs, None), (kbuf, MLSTM_WIDTH, ks, QK_SCALE)):
        wl = slice(woff, woff + MLSTM_WIDTH)
        acc = jnp.broadcast_to(cb_ref[:, wl], (tb, MLSTM_WIDTH))
        for tap in range(CONV_WIDTH):
            off = pad - (CONV_WIDTH - 1) + tap
            acc = acc + buf[off:off + tb, :] * cw_ref[tap:tap + 1, wl]
        act = acc * _sigmoid(acc)
        dst[...] = act if scale is None else act * scale
    qbuf[0:pad, :] = q_ref[tb - pad:tb, :]
    kbuf[0:pad, :] = k_ref[tb - pad:tb, :]

    lane_row = lax.broadcasted_iota(jnp.int32, (1, LANES), 1)
    graw = gate_ref[...] + gb_ref[...]
    gs[...] = jnp.where(lane_row < nh, graw, _log_sigmoid(graw))

    ii = lax.broadcasted_iota(jnp.int32, (CHUNK, CHUNK), 0)
    jj = lax.broadcasted_iota(jnp.int32, (CHUNK, CHUNK), 1)
    causal = ii >= jj
    tril = causal.astype(F32)

    def chunk(ci, carry):
        rows = pl.ds(pl.multiple_of(ci * CHUNK, CHUNK), CHUNK)
        g = gs[rows, :]
        gcum = jnp.dot(tril, g, precision=lax.Precision.HIGHEST, preferred_element_type=F32)
        g_t = jnp.transpose(g)
        gcum_t = jnp.transpose(gcum)
        for h in range(nh):
            ln = _head_lanes(h)
            qc = qs[rows, ln]
            kc = ks[rows, ln]
            vb = v_ref[rows, ln].astype(BF16)
            ic_c = g[:, h:h + 1]
            bc_c = gcum[:, nh + h:nh + h + 1]
            ic_r = g_t[h:h + 1, :]
            bc_r = gcum_t[nh + h:nh + h + 1, :]
            m_prev = m_st[h:h + 1, 0:1]
            log_d = jnp.where(causal, bc_c - bc_r + ic_r, -jnp.inf)
            log_inter = bc_c + m_prev
            m_row = jnp.maximum(jnp.max(log_d, axis=1, keepdims=True), log_inter)
            qb = qc.astype(BF16)
            s = _dot_nt(qb, kc.astype(BF16)) * jnp.exp(log_d - m_row)
            w_inter = jnp.exp(log_inter - m_row)
            cs = c_st[h]
            ns = n_st[h:h + 1, :]
            num = _dot(s.astype(BF16), vb) + w_inter * _dot(qb, cs.astype(BF16))
            den = jnp.sum(s, axis=1, keepdims=True) + w_inter * jnp.sum(qc * ns, axis=1, keepdims=True)
            hout = num / jnp.maximum(jnp.abs(den), jnp.exp(-m_row))
            b_last = bc_r[:, CHUNK - 1:CHUNK]
            m_new = jnp.maximum(b_last + m_prev, jnp.max(b_last - bc_r + ic_r, axis=1, keepdims=True))
            keep = jnp.exp(b_last + m_prev - m_new)
            kw = kc * jnp.exp(b_last - bc_c + ic_c - m_new)
            c_st[h] = keep * cs + _dot(jnp.transpose(kw).astype(BF16), vb)
            n_st[h:h + 1, :] = keep * ns + jnp.sum(kw, axis=0, keepdims=True)
            m_st[h:h + 1, :] = jnp.broadcast_to(m_new, (1, LANES))
            y = _head_norm(hout) * nw_ref[:, ln] * _sigmoid(og_ref[rows, ln])
            o_ref[rows, ln] = y.astype(o_ref.dtype)
        return carry

    lax.fori_loop(0, nchunk, chunk, 0)


def mlstm_block(proj, conv_w, conv_b, gate_bias, norm_w, tb):
    l = proj.shape[0]
    base = (4 * RET_WIDTH) // MLSTM_WIDTH

    def col(j):
        return pl.BlockSpec((tb, MLSTM_WIDTH), lambda r, j=j: (r, base + j))

    def whole(a):
        return pl.BlockSpec(a.shape, lambda r: (0, 0))

    wide = pltpu.VMEM((tb, MLSTM_WIDTH), F32)
    hist = pltpu.VMEM((tb + SUBLANES, MLSTM_WIDTH), F32)
    return pl.pallas_call(
        functools.partial(_mlstm_kernel, nchunk=tb // CHUNK, tb=tb),
        out_shape=jax.ShapeDtypeStruct((l, MLSTM_WIDTH), BF16),
        grid=(l // tb,),
        in_specs=[col(0), col(1), col(2), col(3),
                  pl.BlockSpec((tb, LANES), lambda r: (r, GATE_COLBLK)),
                  whole(conv_w), whole(conv_b), whole(gate_bias), whole(norm_w)],
        out_specs=pl.BlockSpec((tb, MLSTM_WIDTH), lambda r: (r, 0)),
        scratch_shapes=[pltpu.VMEM((MLSTM_HEADS, HEAD_DIM, HEAD_DIM), F32),
                        pltpu.VMEM((SUBLANES, LANES), F32), pltpu.VMEM((SUBLANES, LANES), F32),
                        hist, hist, wide, wide, pltpu.VMEM((tb, LANES), F32)],
        compiler_params=_cparams(("arbitrary",)),
    )(proj, proj, proj, proj, proj, conv_w, conv_b, gate_bias, norm_w)


def _s5_prep_kernel(lre_ref, lim_ref, ldt_ref, btre_ref, btim_ref, ctre_ref, ctim_ref,
                    toep_ref, vre_ref, vim_ref, wre_ref, wim_ref, are_ref, aim_ref):
    lre = lre_ref[...]
    lim = lim_ref[...]
    dt = jnp.exp(ldt_ref[...])
    mag = jnp.exp(lre * dt)
    ang = lim * dt
    zr = mag * jnp.cos(ang) - 1.0
    zi = mag * jnp.sin(ang)
    den = lre * lre + lim * lim
    w_re = (zr * lre + zi * lim) / den
    w_im = (zi * lre - zr * lim) / den
    row_g = lax.broadcasted_iota(jnp.int32, (LANES, S5_SB), 0) // S5_GROUP
    col_g = lax.broadcasted_iota(jnp.int32, (LANES, S5_SB), 1) // S5_STATE
    same = row_g == col_g
    bt_re = btre_ref[...]
    bt_im = btim_ref[...]
    bb_re = jnp.where(same, w_re * bt_re - w_im * bt_im, 0.0)
    bb_im = jnp.where(same, w_re * bt_im + w_im * bt_re, 0.0)
    ct_re = jnp.where(same, ctre_ref[...], 0.0)
    ct_im = jnp.where(same, ctim_ref[...], 0.0)
    hp = lax.Precision.HIGHEST
    kblk = []
    for d in range(S5_T + 1):
        pm = jnp.exp(lre * dt * float(d))
        pa = lim * dt * float(d)
        p_re = pm * jnp.cos(pa)
        p_im = pm * jnp.sin(pa)
        if d < S5_T:
            ab_re = p_re * bb_re - p_im * bb_im
            ab_im = p_re * bb_im + p_im * bb_re
            srow = slice((S5_T - 1 - d) * LANES, (S5_T - d) * LANES)
            vre_ref[0, srow, :] = ab_re.astype(BF16)
            vim_ref[0, srow, :] = ab_im.astype(BF16)
            kblk.append((_dot_nt(ab_re, ct_re, hp) - _dot_nt(ab_im, ct_im, hp)).astype(BF16))
        if d >= 1:
            trow = slice((d - 1) * LANES, d * LANES)
            wre_ref[0, trow, :] = (ct_re * p_re - ct_im * p_im).astype(BF16)
            wim_ref[0, trow, :] = (-(ct_re * p_im + ct_im * p_re)).astype(BF16)
        if d == S5_T:
            are_ref[...] = p_re
            aim_ref[...] = p_im
    zero = jnp.zeros((LANES, LANES), BF16)
    for s in range(S5_T):
        for t in range(S5_T):
            toep_ref[0, s * LANES:(s + 1) * LANES, t * LANES:(t + 1) * LANES] = kblk[t - s] if t >= s else zero


def s5_operators(lam_re, lam_im, log_dt, b_re, b_im, c_re, c_im):
    f32 = F32
    nst = S5_GROUPS * S5_STATE

    def tiled(m):
        return jnp.tile(m.astype(f32).reshape(S5_WIDTH, S5_STATE), (1, S5_GPB))

    lre = lam_re.astype(f32).reshape(1, nst)
    lim = lam_im.astype(f32).reshape(1, nst)
    ldt = jnp.repeat(log_dt.astype(f32), S5_STATE).reshape(1, nst)
    bt_re = tiled(jnp.swapaxes(b_re, -1, -2))
    bt_im = tiled(jnp.swapaxes(b_im, -1, -2))
    ct_re = tiled(c_re)
    ct_im = tiled(c_im)
    lane_blk = pl.BlockSpec((1, S5_SB), lambda g: (0, g))
    par_blk = pl.BlockSpec((LANES, S5_SB), lambda g: (g, 0))

    def out_blk(rows, cols):
        return pl.BlockSpec((1, rows, cols), lambda g: (g, 0, 0))

    return pl.pallas_call(
        _s5_prep_kernel,
        out_shape=[jax.ShapeDtypeStruct((S5_GB, S5_KT, S5_KT), BF16)]
        + [jax.ShapeDtypeStruct((S5_GB, S5_KT, S5_SB), BF16)] * 4
        + [jax.ShapeDtypeStruct((1, nst), f32)] * 2,
        grid=(S5_GB,),
        in_specs=[lane_blk, lane_blk, lane_blk, par_blk, par_blk, par_blk, par_blk],
        out_specs=[out_blk(S5_KT, S5_KT)] + [out_blk(S5_KT, S5_SB)] * 4 + [lane_blk, lane_blk],
        compiler_params=_cparams(("parallel",)),
    )(lre, lim, ldt, bt_re, bt_im, ct_re, ct_im)


def _gelu_tanh(x):
    c = math.sqrt(2.0 / math.pi)
    return 0.5 * x * (1.0 + jnp.tanh(c * (x + 0.044715 * (x * x * x))))


def _s5_kernel(u_ref, toep_ref, vre_ref, vim_ref, wre_ref, wim_ref, are_ref, aim_ref, d_ref, y_ref,
               car_re, car_im, z_re, z_im, x_re, x_im, *, tmc):
    rb = pl.program_id(1)

    @pl.when(rb == 0)
    def _():
        car_re[...] = jnp.zeros_like(car_re)
        car_im[...] = jnp.zeros_like(car_im)

    us = [u_ref[pl.ds(s, tmc, stride=S5_T), :] for s in range(S5_T)]
    ucat = jnp.concatenate([u.astype(BF16) for u in us], axis=1)
    z_re[...] = _dot(ucat, vre_ref[0])
    z_im[...] = _dot(ucat, vim_ref[0])
    a_r = are_ref[...]
    a_i = aim_ref[...]

    def step(c, carry):
        xr, xi = carry
        row = pl.ds(c, 1)
        x_re[row, :] = xr
        x_im[row, :] = xi
        return (a_r * xr - a_i * xi + z_re[row, :], a_r * xi + a_i * xr + z_im[row, :])

    xr, xi = lax.fori_loop(0, tmc, step, (car_re[0:1, :], car_im[0:1, :]))
    car_re[0:1, :] = xr
    car_im[0:1, :] = xi
    y = (_dot(ucat, toep_ref[0]) + _dot_nt(x_re[...].astype(BF16), wre_ref[0])
         + _dot_nt(x_im[...].astype(BF16), wim_ref[0]))
    for t in range(S5_T):
        yt = y[:, t * LANES:(t + 1) * LANES] + d_ref[...] * us[t]
        y_ref[pl.ds(t, tmc, stride=S5_T), :] = _gelu_tanh(yt)


def s5_block(proj, ops, d_skip, tb):
    l = proj.shape[0]
    toep, v_re, v_im, w_re, w_im, a_re, a_im = ops
    tmc = tb // S5_T

    def per_gb(cols):
        return pl.BlockSpec((1, S5_KT, cols), lambda g, r: (g, 0, 0))

    lane_blk = pl.BlockSpec((1, S5_SB), lambda g, r: (0, g))
    fold = pltpu.VMEM((tmc, S5_SB), F32)
    carry = pltpu.VMEM((SUBLANES, S5_SB), F32)
    return pl.pallas_call(
        functools.partial(_s5_kernel, tmc=tmc),
        out_shape=jax.ShapeDtypeStruct((l, S5_WIDTH), F32),
        grid=(S5_GB, l // tb),
        in_specs=[pl.BlockSpec((tb, LANES), lambda g, r: (r, S5_COLBLK + g)),
                  per_gb(S5_KT), per_gb(S5_SB), per_gb(S5_SB), per_gb(S5_SB), per_gb(S5_SB),
                  lane_blk, lane_blk, pl.BlockSpec((1, LANES), lambda g, r: (0, g))],
        out_specs=pl.BlockSpec((tb, LANES), lambda g, r: (r, g)),
        scratch_shapes=[carry, carry, fold, fold, fold, fold],
        compiler_params=_cparams(("parallel", "arbitrary")),
    )(proj, toep, v_re, v_im, w_re, w_im, a_re, a_im, d_skip)


def _glu_kernel(y_ref, w_ref, b_ref, o_ref):
    y = y_ref[...]
    z = _dot(y.astype(BF16), w_ref[...]) + b_ref[...]
    o_ref[...] = (y * _sigmoid(z)).astype(o_ref.dtype)


def glu_block(y, w, b, tm):
    l, n = y.shape
    return pl.pallas_call(
        _glu_kernel,
        out_shape=jax.ShapeDtypeStruct((l, n), BF16),
        grid=(l // tm,),
        in_specs=[pl.BlockSpec((tm, n), lambda i: (i, 0)), pl.BlockSpec((n, n), lambda i: (0, 0)),
                  pl.BlockSpec((1, n), lambda i: (0, 0))],
        out_specs=pl.BlockSpec((tm, n), lambda i: (i, 0)),
        compiler_params=_cparams(("parallel",)),
    )(y, w, b)


def _layer_norm(z, w, b):
    mu = jnp.mean(z, axis=-1, keepdims=True)
    d = z - mu
    var = jnp.mean(d * d, axis=-1, keepdims=True)
    return d * lax.rsqrt(var + LN_EPS) * w + b


def _outproj_kernel(a1_ref, a2_ref, a3_ref, w1_ref, w2_ref, w3_ref, x_ref, lw_ref, lb_ref, o_ref, ob_ref):
    mix = _dot(a1_ref[...], w1_ref[...]) + _dot(a2_ref[...], w2_ref[...]) + _dot(a3_ref[...], w3_ref[...])
    x1 = _layer_norm(ALPHA * x_ref[...] + mix, lw_ref[...], lb_ref[...])
    o_ref[...] = x1
    ob_ref[...] = x1.astype(BF16)


def outproj_block(y_ret, y_m, y_s, w1, w2, w3, x, ln_w, ln_b, tm):
    l = x.shape[0]

    def rows(n):
        return pl.BlockSpec((tm, n), lambda i: (i, 0))

    def whole(a):
        return pl.BlockSpec(a.shape, lambda i: (0, 0))

    return pl.pallas_call(
        _outproj_kernel,
        out_shape=[jax.ShapeDtypeStruct((l, D_MODEL), F32), jax.ShapeDtypeStruct((l, D_MODEL), BF16)],
        grid=(l // tm,),
        in_specs=[rows(RET_WIDTH), rows(MLSTM_WIDTH), rows(S5_WIDTH), whole(w1), whole(w2), whole(w3),
                  rows(D_MODEL), whole(ln_w), whole(ln_b)],
        out_specs=[rows(D_MODEL), rows(D_MODEL)],
        compiler_params=_cparams(("parallel",)),
    )(y_ret, y_m, y_s, w1, w2, w3, x, ln_w, ln_b)


def _down_kernel(h_ref, w_ref, x1_ref, r_ref, rb_ref, acc_ref, *, nk):
    k = pl.program_id(1)

    @pl.when(k == 0)
    def _():
        acc_ref[...] = jnp.zeros_like(acc_ref)

    acc_ref[...] += _dot(h_ref[...], w_ref[...])

    @pl.when(k == nk - 1)
    def _():
        r = ALPHA * x1_ref[...] + acc_ref[...]
        r_ref[...] = r
        rb_ref[...] = r.astype(BF16)


def down_block(hid, w_down, x1, tm, tk):
    l, ff = hid.shape
    nk = ff // tk
    rows = pl.BlockSpec((tm, D_MODEL), lambda i, k: (i, 0))
    return pl.pallas_call(
        functools.partial(_down_kernel, nk=nk),
        out_shape=[jax.ShapeDtypeStruct((l, D_MODEL), F32), jax.ShapeDtypeStruct((l, D_MODEL), BF16)],
        grid=(l // tm, nk),
        in_specs=[pl.BlockSpec((tm, tk), lambda i, k: (i, k)), pl.BlockSpec((tk, D_MODEL), lambda i, k: (k, 0)), rows],
        out_specs=[rows, rows],
        scratch_shapes=[pltpu.VMEM((tm, D_MODEL), F32)],
        compiler_params=_cparams(("parallel", "arbitrary")),
    )(hid, w_down, x1)


def _final_kernel(r_ref, rb_ref, p_ref, wg_ref, wp_ref, lw_ref, lb_ref, o_ref, ob_ref):
    gate = _sigmoid(_dot(rb_ref[...], wg_ref[...]))
    ple = gate * _dot(p_ref[...], wp_ref[...])
    out = _layer_norm(r_ref[...] + ple, lw_ref[...], lb_ref[...])
    o_ref[...] = out
    ob_ref[...] = out.astype(BF16)


def final_block(r, rb, p_b, w_gate, w_ple, ln_w, ln_b, tm):
    l = r.shape[0]
    rows = pl.BlockSpec((tm, D_MODEL), lambda i: (i, 0))

    def whole(a):
        return pl.BlockSpec(a.shape, lambda i: (0, 0))

    return pl.pallas_call(
        _final_kernel,
        out_shape=[jax.ShapeDtypeStruct((l, D_MODEL), F32), jax.ShapeDtypeStruct((l, D_MODEL), BF16)],
        grid=(l // tm,),
        in_specs=[rows, rows, pl.BlockSpec((tm, PLE_DIM), lambda i: (i, 0)), whole(w_gate), whole(w_ple),
                  whole(ln_w), whole(ln_b)],
        out_specs=[rows, rows],
        compiler_params=_cparams(("parallel",)),
    )(r, rb, p_b, w_gate, w_ple, ln_w, ln_b)


def _pad_in_proj(w_in):
    main = w_in[:, :MAIN_WIDTH]
    gates = w_in[:, MAIN_WIDTH:MAIN_WIDTH + 2 * MLSTM_HEADS]
    s5 = w_in[:, MAIN_WIDTH + 2 * MLSTM_HEADS:]
    zeros = jnp.zeros((w_in.shape[0], PROJ_WIDTH - GATE_COL - 2 * MLSTM_HEADS), w_in.dtype)
    return jnp.concatenate([main, s5, gates, zeros], axis=1).astype(BF16)


def _row(v):
    return v.astype(F32).reshape(1, -1)


def _hybrid_layer(x, xb, cos_t, sin_t, p_b, w_in, conv_w, conv_b, i_bias, f_bias, ret_norm_w, mlstm_norm_w,
                  lam_re, lam_im, log_dt, b_re, b_im, c_re, c_im, s5_d, glu_w, glu_b,
                  w_out, ln1_w, ln1_b, w_up, w_down, w_gate, w_ple, ln2_w, ln2_b, tiles):
    proj = matmul(xb, _pad_in_proj(w_in), out_dtype=F32, tm=tiles["mm_m"], tn=768)
    y_ret = retention_block(proj, cos_t, sin_t, _row(ret_norm_w), tiles["mix"])
    gate_bias = jnp.concatenate([i_bias.astype(F32), f_bias.astype(F32),
                                 jnp.zeros((LANES - 2 * MLSTM_HEADS,), F32)]).reshape(1, LANES)
    y_m = mlstm_block(proj, conv_w.astype(F32), _row(conv_b), gate_bias, _row(mlstm_norm_w), tiles["mix"])
    ops = s5_operators(lam_re, lam_im, log_dt, b_re, b_im, c_re, c_im)
    y_s = s5_block(proj, ops, _row(s5_d), tiles["s5"])
    y_s = glu_block(y_s, glu_w.astype(BF16), _row(glu_b), tiles["row"])
    wo = w_out.astype(BF16)
    x1, x1b = outproj_block(y_ret, y_m, y_s, wo[:RET_WIDTH], wo[RET_WIDTH:RET_WIDTH + MLSTM_WIDTH],
                            wo[RET_WIDTH + MLSTM_WIDTH:], x, _row(ln1_w), _row(ln1_b), tiles["row"])
    hid = matmul(x1b, w_up.astype(BF16), out_dtype=BF16, tm=tiles["mm_m"], tn=1024, act="relu2")
    r, rb = down_block(hid, w_down.astype(BF16), x1, tiles["row"], 1024)
    return final_block(r, rb, p_b, w_gate.astype(BF16), w_ple.astype(BF16), _row(ln2_w), _row(ln2_b), tiles["row"])


def _tiles(l):
    return {"mm_m": min(1024, l), "mix": min(512, l), "s5": min(2048, l), "row": min(512, l), "rope": min(2048, l)}


def kernel(x, p, positions, w_in, mlstm_conv_w, mlstm_conv_b, mlstm_i_bias, mlstm_f_bias, ret_norm_w, mlstm_norm_w, s5_lambda_re, s5_lambda_im, s5_log_dt, s5_B_re, s5_B_im, s5_C_re, s5_C_im, s5_D, s5_glu_w, s5_glu_b, w_out, ln1_w, ln1_b, w_up, w_down, w_gate, w_ple, ln2_w, ln2_b):
    bsz, l, _ = x.shape
    assert bsz == 1
    tiles = _tiles(l)
    cos_t, sin_t = rope_tables(positions.astype(F32).reshape(l, 1), tiles["rope"])
    xf = x.reshape(l, D_MODEL).astype(F32)
    xb = xf.astype(BF16)
    for i in range(w_in.shape[0]):
        xf, xb = _hybrid_layer(
            xf, xb, cos_t, sin_t, p[i].reshape(l, PLE_DIM).astype(BF16), w_in[i], mlstm_conv_w[i], mlstm_conv_b[i],
            mlstm_i_bias[i], mlstm_f_bias[i], ret_norm_w[i], mlstm_norm_w[i], s5_lambda_re[i], s5_lambda_im[i],
            s5_log_dt[i], s5_B_re[i], s5_B_im[i], s5_C_re[i], s5_C_im[i], s5_D[i], s5_glu_w[i], s5_glu_b[i],
            w_out[i], ln1_w[i], ln1_b[i], w_up[i], w_down[i], w_gate[i], w_ple[i], ln2_w[i], ln2_b[i], tiles)
    return xf.reshape(bsz, l, D_MODEL)
```

```python
import functools
import math

import jax
import jax.numpy as jnp
import numpy as np
from jax import lax
from jax.experimental import pallas as pl
from jax.experimental.pallas import tpu as pltpu

F32 = jnp.float32
BF16 = jnp.bfloat16

D_MODEL = 2048
DEPTH = 2
HEAD_DIM = 128
RET_HEADS = 6
MLSTM_HEADS = 6
RET_WIDTH = RET_HEADS * HEAD_DIM
MLSTM_WIDTH = MLSTM_HEADS * HEAD_DIM
S5_WIDTH = D_MODEL - RET_WIDTH - MLSTM_WIDTH
S5_GROUP = 16
S5_GROUPS = S5_WIDTH // S5_GROUP
S5_STATE = 64
CONV_WIDTH = 4
CHUNK = 128
D_FF = 4 * D_MODEL
PLE_DIM = 256
ROPE_BASE = 10000.0
LN_EPS = 1e-5
ALPHA = (2 * DEPTH) ** 0.25
QK_SCALE = HEAD_DIM ** -0.5

LANES = 128
SUBLANES = 8
MAIN_WIDTH = 4 * RET_WIDTH + 4 * MLSTM_WIDTH
TAIL_COLS = 2 * MLSTM_HEADS + S5_WIDTH
TAIL_WIDTH = 5 * LANES
GATE_LANE0 = LANES - 2 * MLSTM_HEADS
GATE_COLBLK = S5_WIDTH // LANES
S5_T = 16
S5_GB = S5_WIDTH // LANES
S5_GPB = LANES // S5_GROUP
S5_SB = S5_GPB * S5_STATE
S5_KT = S5_T * LANES
VMEM_LIMIT = 56 * 1024 * 1024


def _cparams(sem):
    return pltpu.CompilerParams(dimension_semantics=sem, vmem_limit_bytes=VMEM_LIMIT)


def _mm_kernel(a_ref, b_ref, o_ref, *scratch, nk, act):
    def finish(r):
        if act == "relu2":
            r = jnp.square(jnp.maximum(r, 0.0))
        elif act == "rotate_tail":
            r = pltpu.roll(r, TAIL_WIDTH - 2 * MLSTM_HEADS, 1)
        o_ref[...] = r.astype(o_ref.dtype)

    if nk == 1:
        finish(jnp.dot(a_ref[...], b_ref[...], preferred_element_type=F32))
        return
    acc_ref, = scratch
    k = pl.program_id(2)

    @pl.when(k == 0)
    def _():
        acc_ref[...] = jnp.zeros_like(acc_ref)

    acc_ref[...] += jnp.dot(a_ref[...], b_ref[...], preferred_element_type=F32)

    @pl.when(k == nk - 1)
    def _():
        finish(acc_ref[...])


def matmul(a, b, layer, *, out_dtype, tm, tn, n=None, tk=None, act=None):
    m, kdim = a.shape
    n = b.shape[2] if n is None else n
    tk = kdim if tk is None else tk
    nk = kdim // tk
    assert m % tm == 0 and n % tn == 0 and kdim % tk == 0 and b.shape[1] == kdim
    return pl.pallas_call(
        functools.partial(_mm_kernel, nk=nk, act=act),
        out_shape=jax.ShapeDtypeStruct((m, n), out_dtype),
        grid=(m // tm, n // tn, nk),
        in_specs=[pl.BlockSpec((tm, tk), lambda i, j, k: (i, k)),
                  pl.BlockSpec((None, tk, tn), lambda i, j, k: (layer, k, j))],
        out_specs=pl.BlockSpec((tm, tn), lambda i, j, k: (i, j)),
        scratch_shapes=[] if nk == 1 else [pltpu.VMEM((tm, tn), F32)],
        compiler_params=_cparams(("parallel", "parallel", "arbitrary")),
    )(a, b)


def _rope_kernel(pos_ref, inv_ref, sign_ref, cos_ref, sin_ref):
    ang = pos_ref[...] * inv_ref[...]
    cos_ref[...] = jnp.cos(ang)
    sin_ref[...] = jnp.sin(ang) * sign_ref[...]


def rope_tables(pos_col, tl):
    l = pos_col.shape[0]
    half = np.arange(0, HEAD_DIM, 2, dtype=np.float32) / np.float32(HEAD_DIM)
    inv = (np.float32(ROPE_BASE) ** (-half)).astype(np.float32)
    inv2 = jnp.asarray(np.concatenate([inv, inv])[None, :])
    sign = jnp.asarray(np.concatenate([-np.ones(64, np.float32), np.ones(64, np.float32)])[None, :])
    row = pl.BlockSpec((1, LANES), lambda i: (0, 0))
    return pl.pallas_call(
        _rope_kernel,
        out_shape=[jax.ShapeDtypeStruct((l, LANES), F32)] * 2,
        grid=(l // tl,),
        in_specs=[pl.BlockSpec((tl, 1), lambda i: (i, 0)), row, row],
        out_specs=[pl.BlockSpec((tl, LANES), lambda i: (i, 0))] * 2,
        compiler_params=_cparams(("parallel",)),
    )(pos_col, inv2, sign)


def _head_norm(y):
    mu = jnp.mean(y, axis=-1, keepdims=True)
    d = y - mu
    var = jnp.mean(d * d, axis=-1, keepdims=True)
    return d * lax.rsqrt(var + LN_EPS)


def _sigmoid(x):
    return 1.0 / (1.0 + jnp.exp(-x))


def _dot_nt(a, b, precision=None):
    return lax.dot_general(a, b, (((1,), (1,)), ((), ())), precision=precision, preferred_element_type=F32)


def _dot(a, b):
    return jnp.dot(a, b, preferred_element_type=F32)


def _head_lanes(h):
    return slice(h * HEAD_DIM, (h + 1) * HEAD_DIM)


def _layer_spec(stacked, layer):
    return pl.BlockSpec((None,) + stacked.shape[1:], lambda *_: (layer, 0, 0))


def _ret_log_gamma(h):
    return float(np.log(np.float32(1.0) - np.float32(2.0) ** np.float32(-5.0 - h)))


def _ret_kernel(q_ref, k_ref, v_ref, g_ref, cos_ref, sin_ref, w_ref, o_ref,
                state_ref, decay_ref, qdec_ref, kdec_ref, *, nchunk):
    rb = pl.program_id(0)

    @pl.when(rb == 0)
    def _():
        state_ref[...] = jnp.zeros_like(state_ref)
        ii = lax.broadcasted_iota(jnp.int32, (CHUNK, CHUNK), 0)
        jj = lax.broadcasted_iota(jnp.int32, (CHUNK, CHUNK), 1)
        rel = (ii - jj).astype(F32)
        idx = ii.astype(F32)
        for h in range(RET_HEADS):
            lg = _ret_log_gamma(h)
            decay_ref[h] = jnp.where(rel >= 0.0, jnp.exp(lg * jnp.maximum(rel, 0.0)), 0.0)
            qdec_ref[h] = jnp.exp(lg * (idx + 1.0))
            kdec_ref[h] = jnp.exp(lg * (CHUNK - 1.0 - idx))

    def chunk(ci, carry):
        rows = pl.ds(pl.multiple_of(ci * CHUNK, CHUNK), CHUNK)
        cos = cos_ref[rows, :]
        sin = sin_ref[rows, :]
        for h in range(RET_HEADS):
            ln = _head_lanes(h)
            q = q_ref[rows, ln]
            k = k_ref[rows, ln]
            qr = q * cos + pltpu.roll(q, HEAD_DIM // 2, 1) * sin
            kr = (k * cos + pltpu.roll(k, HEAD_DIM // 2, 1) * sin) * QK_SCALE
            qb = qr.astype(BF16)
            vb = v_ref[rows, ln].astype(BF16)
            scores = _dot_nt(qb, kr.astype(BF16)) * decay_ref[h]
            st = state_ref[h]
            out = _dot(scores.astype(BF16), vb) + _dot(qb, st.astype(BF16)) * qdec_ref[h]
            kd_t = jnp.transpose(kr * kdec_ref[h]).astype(BF16)
            state_ref[h] = st * math.exp(_ret_log_gamma(h) * CHUNK) + _dot(kd_t, vb)
            g = g_ref[rows, ln]
            y = _head_norm(out) * w_ref[:, ln] * (g * _sigmoid(g))
            o_ref[rows, ln] = y.astype(o_ref.dtype)
        return carry

    lax.fori_loop(0, nchunk, chunk, 0)


def retention_block(proj, cos_t, sin_t, norm_w, layer, tb):
    l = proj.shape[0]

    def col(j):
        return pl.BlockSpec((tb, RET_WIDTH), lambda r, j=j: (r, j))

    tab = pl.BlockSpec((tb, LANES), lambda r: (r, 0))
    hsq = pltpu.VMEM((RET_HEADS, CHUNK, CHUNK), F32)
    return pl.pallas_call(
        functools.partial(_ret_kernel, nchunk=tb // CHUNK),
        out_shape=jax.ShapeDtypeStruct((l, RET_WIDTH), BF16),
        grid=(l // tb,),
        in_specs=[col(0), col(1), col(2), col(3), tab, tab, _layer_spec(norm_w, layer)],
        out_specs=pl.BlockSpec((tb, RET_WIDTH), lambda r: (r, 0)),
        scratch_shapes=[hsq, hsq, hsq, hsq],
        compiler_params=_cparams(("arbitrary",)),
    )(proj, proj, proj, proj, cos_t, sin_t, norm_w)


def _log_sigmoid(x):
    return -(jnp.maximum(-x, 0.0) + jnp.log1p(jnp.exp(-jnp.abs(x))))


def _mlstm_kernel(q_ref, k_ref, v_ref, og_ref, gate_ref, cw_ref, cb_ref, gb_ref, nw_ref,
                  o_ref, c_st, n_st, m_st, qbuf, kbuf, qs, ks, gs, *, nchunk, tb):
    rb = pl.program_id(0)
    pad = SUBLANES
    nh = MLSTM_HEADS

    @pl.when(rb == 0)
    def _():
        c_st[...] = jnp.zeros_like(c_st)
        n_st[...] = jnp.zeros_like(n_st)
        m_st[...] = jnp.zeros_like(m_st)
        qbuf[0:pad, :] = jnp.zeros((pad, MLSTM_WIDTH), F32)
        kbuf[0:pad, :] = jnp.zeros((pad, MLSTM_WIDTH), F32)

    qbuf[pad:pad + tb, :] = q_ref[...]
    kbuf[pad:pad + tb, :] = k_ref[...]
    for buf, woff, dst, scale in ((qbuf, 0, qs, None), (kbuf, MLSTM_WIDTH, ks, QK_SCALE)):
        wl = slice(woff, woff + MLSTM_WIDTH)
        acc = jnp.broadcast_to(cb_ref[:, wl], (tb, MLSTM_WIDTH))
        for tap in range(CONV_WIDTH):
            off = pad - (CONV_WIDTH - 1) + tap
            acc = acc + buf[off:off + tb, :] * cw_ref[tap:tap + 1, wl]
        act = acc * _sigmoid(acc)
        dst[...] = act if scale is None else act * scale
    qbuf[0:pad, :] = q_ref[tb - pad:tb, :]
    kbuf[0:pad, :] = k_ref[tb - pad:tb, :]

    lane_row = lax.broadcasted_iota(jnp.int32, (1, LANES), 1)
    graw = gate_ref[...] + gb_ref[...]
    gs[...] = jnp.where(lane_row < GATE_LANE0 + nh, graw, _log_sigmoid(graw))

    ii = lax.broadcasted_iota(jnp.int32, (CHUNK, CHUNK), 0)
    jj = lax.broadcasted_iota(jnp.int32, (CHUNK, CHUNK), 1)
    causal = ii >= jj
    tril = causal.astype(F32)

    def chunk(ci, carry):
        rows = pl.ds(pl.multiple_of(ci * CHUNK, CHUNK), CHUNK)
        g = gs[rows, :]
        gcum = jnp.dot(tril, g, precision=lax.Precision.HIGHEST, preferred_element_type=F32)
        g_t = jnp.transpose(g)
        gcum_t = jnp.transpose(gcum)
        for h in range(nh):
            ln = _head_lanes(h)
            qc = qs[rows, ln]
            kc = ks[rows, ln]
            vb = v_ref[rows, ln].astype(BF16)
            li = GATE_LANE0 + h
            lf = GATE_LANE0 + nh + h
            ic_c = g[:, li:li + 1]
            bc_c = gcum[:, lf:lf + 1]
            ic_r = g_t[li:li + 1, :]
            bc_r = gcum_t[lf:lf + 1, :]
            m_prev = m_st[h:h + 1, 0:1]
            log_d = jnp.where(causal, bc_c - bc_r + ic_r, -jnp.inf)
            log_inter = bc_c + m_prev
            m_row = jnp.maximum(jnp.max(log_d, axis=1, keepdims=True), log_inter)
            qb = qc.astype(BF16)
            s = _dot_nt(qb, kc.astype(BF16)) * jnp.exp(log_d - m_row)
            w_inter = jnp.exp(log_inter - m_row)
            cs = c_st[h]
            ns = n_st[h:h + 1, :]
            num = _dot(s.astype(BF16), vb) + w_inter * _dot(qb, cs.astype(BF16))
            den = jnp.sum(s, axis=1, keepdims=True) + w_inter * jnp.sum(qc * ns, axis=1, keepdims=True)
            hout = num / jnp.maximum(jnp.abs(den), jnp.exp(-m_row))
            b_last = bc_r[:, CHUNK - 1:CHUNK]
            m_new = jnp.maximum(b_last + m_prev, jnp.max(b_last - bc_r + ic_r, axis=1, keepdims=True))
            keep = jnp.exp(b_last + m_prev - m_new)
            kw = kc * jnp.exp(b_last - bc_c + ic_c - m_new)
            c_st[h] = keep * cs + _dot(jnp.transpose(kw).astype(BF16), vb)
            n_st[h:h + 1, :] = keep * ns + jnp.sum(kw, axis=0, keepdims=True)
            m_st[h:h + 1, :] = jnp.broadcast_to(m_new, (1, LANES))
            y = _head_norm(hout) * nw_ref[:, ln] * _sigmoid(og_ref[rows, ln])
            o_ref[rows, ln] = y.astype(o_ref.dtype)
        return carry

    lax.fori_loop(0, nchunk, chunk, 0)


def mlstm_block(proj, tail, conv_w, conv_b, gate_bias, norm_w, layer, tb):
    l = proj.shape[0]
    base = (4 * RET_WIDTH) // MLSTM_WIDTH

    def col(j):
        return pl.BlockSpec((tb, MLSTM_WIDTH), lambda r, j=j: (r, base + j))

    wide = pltpu.VMEM((tb, MLSTM_WIDTH), F32)
    hist = pltpu.VMEM((tb + SUBLANES, MLSTM_WIDTH), F32)
    return pl.pallas_call(
        functools.partial(_mlstm_kernel, nchunk=tb // CHUNK, tb=tb),
        out_shape=jax.ShapeDtypeStruct((l, MLSTM_WIDTH), BF16),
        grid=(l // tb,),
        in_specs=[col(0), col(1), col(2), col(3),
                  pl.BlockSpec((tb, LANES), lambda r: (r, GATE_COLBLK)),
                  _layer_spec(conv_w, layer), _layer_spec(conv_b, layer), _layer_spec(gate_bias, layer),
                  _layer_spec(norm_w, layer)],
        out_specs=pl.BlockSpec((tb, MLSTM_WIDTH), lambda r: (r, 0)),
        scratch_shapes=[pltpu.VMEM((MLSTM_HEADS, HEAD_DIM, HEAD_DIM), F32),
                        pltpu.VMEM((SUBLANES, LANES), F32), pltpu.VMEM((SUBLANES, LANES), F32),
                        hist, hist, wide, wide, pltpu.VMEM((tb, LANES), F32)],
        compiler_params=_cparams(("arbitrary",)),
    )(proj, proj, proj, proj, tail, conv_w, conv_b, gate_bias, norm_w)


def _s5_prep_kernel(lre_ref, lim_ref, ldt_ref, btre_ref, btim_ref, ctre_ref, ctim_ref,
                    toep_ref, vre_ref, vim_ref, wre_ref, wim_ref, are_ref, aim_ref):
    lre = lre_ref[...]
    lim = lim_ref[...]
    dt = jnp.exp(ldt_ref[...])
    mag = jnp.exp(lre * dt)
    ang = lim * dt
    zr = mag * jnp.cos(ang) - 1.0
    zi = mag * jnp.sin(ang)
    den = lre * lre + lim * lim
    w_re = (zr * lre + zi * lim) / den
    w_im = (zi * lre - zr * lim) / den
    row_g = lax.broadcasted_iota(jnp.int32, (LANES, S5_SB), 0) // S5_GROUP
    col_g = lax.broadcasted_iota(jnp.int32, (LANES, S5_SB), 1) // S5_STATE
    same = row_g == col_g
    bt_re = btre_ref[...]
    bt_im = btim_ref[...]
    bb_re = jnp.where(same, w_re * bt_re - w_im * bt_im, 0.0)
    bb_im = jnp.where(same, w_re * bt_im + w_im * bt_re, 0.0)
    ct_re = jnp.where(same, ctre_ref[...], 0.0)
    ct_im = jnp.where(same, ctim_ref[...], 0.0)
    hp = lax.Precision.HIGHEST
    kblk = []
    for d in range(S5_T + 1):
        pm = jnp.exp(lre * dt * float(d))
        pa = lim * dt * float(d)
        p_re = pm * jnp.cos(pa)
        p_im = pm * jnp.sin(pa)
        if d < S5_T:
            ab_re = p_re * bb_re - p_im * bb_im
            ab_im = p_re * bb_im + p_im * bb_re
            srow = slice((S5_T - 1 - d) * LANES, (S5_T - d) * LANES)
            vre_ref[0, srow, :] = ab_re.astype(BF16)
            vim_ref[0, srow, :] = ab_im.astype(BF16)
            kblk.append((_dot_nt(ab_re, ct_re, hp) - _dot_nt(ab_im, ct_im, hp)).astype(BF16))
        if d >= 1:
            trow = slice((d - 1) * LANES, d * LANES)
            wre_ref[0, trow, :] = (ct_re * p_re - ct_im * p_im).astype(BF16)
            wim_ref[0, trow, :] = (-(ct_re * p_im + ct_im * p_re)).astype(BF16)
        if d == S5_T:
            are_ref[...] = p_re
            aim_ref[...] = p_im
    zero = jnp.zeros((LANES, LANES), BF16)
    for s in range(S5_T):
        for t in range(S5_T):
            toep_ref[0, s * LANES:(s + 1) * LANES, t * LANES:(t + 1) * LANES] = kblk[t - s] if t >= s else zero


def s5_operator_inputs(lam_re, lam_im, log_dt, b_re, b_im, c_re, c_im):
    depth = lam_re.shape[0]
    nst = S5_GROUPS * S5_STATE

    def tiled(m):
        return jnp.tile(m.astype(F32).reshape(depth, S5_WIDTH, S5_STATE), (1, 1, S5_GPB))

    lre = lam_re.astype(F32).reshape(depth, 1, nst)
    lim = lam_im.astype(F32).reshape(depth, 1, nst)
    ldt = jnp.repeat(log_dt.astype(F32), S5_STATE, axis=-1).reshape(depth, 1, nst)
    return (lre, lim, ldt, tiled(jnp.swapaxes(b_re, -1, -2)), tiled(jnp.swapaxes(b_im, -1, -2)),
            tiled(c_re), tiled(c_im))


def s5_operators(prep_inputs, layer):
    nst = S5_GROUPS * S5_STATE
    lane_in = pl.BlockSpec((None, 1, S5_SB), lambda g: (layer, 0, g))
    par_in = pl.BlockSpec((None, LANES, S5_SB), lambda g: (layer, g, 0))
    lane_out = pl.BlockSpec((1, S5_SB), lambda g: (0, g))

    def out_blk(rows, cols):
        return pl.BlockSpec((1, rows, cols), lambda g: (g, 0, 0))

    return pl.pallas_call(
        _s5_prep_kernel,
        out_shape=[jax.ShapeDtypeStruct((S5_GB, S5_KT, S5_KT), BF16)]
        + [jax.ShapeDtypeStruct((S5_GB, S5_KT, S5_SB), BF16)] * 4
        + [jax.ShapeDtypeStruct((1, nst), F32)] * 2,
        grid=(S5_GB,),
        in_specs=[lane_in, lane_in, lane_in, par_in, par_in, par_in, par_in],
        out_specs=[out_blk(S5_KT, S5_KT)] + [out_blk(S5_KT, S5_SB)] * 4 + [lane_out, lane_out],
        compiler_params=_cparams(("parallel",)),
    )(*prep_inputs)


def _gelu_tanh(x):
    c = math.sqrt(2.0 / math.pi)
    return 0.5 * x * (1.0 + jnp.tanh(c * (x + 0.044715 * (x * x * x))))


def _s5_kernel(u_ref, toep_ref, vre_ref, vim_ref, wre_ref, wim_ref, are_ref, aim_ref, d_ref, y_ref,
               car_re, car_im, z_re, z_im, x_re, x_im, *, tmc):
    rb = pl.program_id(1)

    @pl.when(rb == 0)
    def _():
        car_re[...] = jnp.zeros_like(car_re)
        car_im[...] = jnp.zeros_like(car_im)

    us = [u_ref[pl.ds(s, tmc, stride=S5_T), :] for s in range(S5_T)]
    ucat = jnp.concatenate([u.astype(BF16) for u in us], axis=1)
    z_re[...] = _dot(ucat, vre_ref[0])
    z_im[...] = _dot(ucat, vim_ref[0])
    a_r = are_ref[...]
    a_i = aim_ref[...]

    def step(c, carry):
        xr, xi = carry
        row = pl.ds(c, 1)
        x_re[row, :] = xr
        x_im[row, :] = xi
        return (a_r * xr - a_i * xi + z_re[row, :], a_r * xi + a_i * xr + z_im[row, :])

    xr, xi = lax.fori_loop(0, tmc, step, (car_re[0:1, :], car_im[0:1, :]))
    car_re[0:1, :] = xr
    car_im[0:1, :] = xi
    y = (_dot(ucat, toep_ref[0]) + _dot_nt(x_re[...].astype(BF16), wre_ref[0])
         + _dot_nt(x_im[...].astype(BF16), wim_ref[0]))
    for t in range(S5_T):
        yt = y[:, t * LANES:(t + 1) * LANES] + d_ref[...] * us[t]
        y_ref[pl.ds(t, tmc, stride=S5_T), :] = _gelu_tanh(yt)


def s5_block(tail, ops, d_skip, layer, tb):
    l = tail.shape[0]
    toep, v_re, v_im, w_re, w_im, a_re, a_im = ops
    tmc = tb // S5_T

    def per_gb(cols):
        return pl.BlockSpec((1, S5_KT, cols), lambda g, r: (g, 0, 0))

    lane_blk = pl.BlockSpec((1, S5_SB), lambda g, r: (0, g))
    fold = pltpu.VMEM((tmc, S5_SB), F32)
    carry = pltpu.VMEM((SUBLANES, S5_SB), F32)
    return pl.pallas_call(
        functools.partial(_s5_kernel, tmc=tmc),
        out_shape=jax.ShapeDtypeStruct((l, S5_WIDTH), F32),
        grid=(S5_GB, l // tb),
        in_specs=[pl.BlockSpec((tb, LANES), lambda g, r: (r, g)),
                  per_gb(S5_KT), per_gb(S5_SB), per_gb(S5_SB), per_gb(S5_SB), per_gb(S5_SB),
                  lane_blk, lane_blk, pl.BlockSpec((None, 1, LANES), lambda g, r: (layer, 0, g))],
        out_specs=pl.BlockSpec((tb, LANES), lambda g, r: (r, g)),
        scratch_shapes=[carry, carry, fold, fold, fold, fold],
        compiler_params=_cparams(("parallel", "arbitrary")),
    )(tail, toep, v_re, v_im, w_re, w_im, a_re, a_im, d_skip)


def _glu_kernel(y_ref, w_ref, b_ref, o_ref):
    y = y_ref[...]
    z = _dot(y.astype(BF16), w_ref[...]) + b_ref[...]
    o_ref[...] = (y * _sigmoid(z)).astype(o_ref.dtype)


def glu_block(y, w, b, layer, tm):
    l, n = y.shape
    return pl.pallas_call(
        _glu_kernel,
        out_shape=jax.ShapeDtypeStruct((l, n), BF16),
        grid=(l // tm,),
        in_specs=[pl.BlockSpec((tm, n), lambda i: (i, 0)), _layer_spec(w, layer), _layer_spec(b, layer)],
        out_specs=pl.BlockSpec((tm, n), lambda i: (i, 0)),
        compiler_params=_cparams(("parallel",)),
    )(y, w, b)


def _layer_norm(z, w, b):
    mu = jnp.mean(z, axis=-1, keepdims=True)
    d = z - mu
    var = jnp.mean(d * d, axis=-1, keepdims=True)
    return d * lax.rsqrt(var + LN_EPS) * w + b


def _outproj_kernel(a1_ref, a2_ref, a3_ref, w1_ref, w2_ref, w3_ref, x_ref, lw_ref, lb_ref, o_ref, ob_ref):
    mix = _dot(a1_ref[...], w1_ref[...]) + _dot(a2_ref[...], w2_ref[...]) + _dot(a3_ref[...], w3_ref[...])
    x1 = _layer_norm(ALPHA * x_ref[...] + mix, lw_ref[...], lb_ref[...])
    o_ref[...] = x1
    ob_ref[...] = x1.astype(BF16)


def outproj_block(y_ret, y_m, y_s, w_out, x, ln_w, ln_b, layer, tm):
    l = x.shape[0]

    def rows(n):
        return pl.BlockSpec((tm, n), lambda i: (i, 0))

    def wrows(n, blk):
        return pl.BlockSpec((None, n, D_MODEL), lambda i: (layer, blk, 0))

    return pl.pallas_call(
        _outproj_kernel,
        out_shape=[jax.ShapeDtypeStruct((l, D_MODEL), F32), jax.ShapeDtypeStruct((l, D_MODEL), BF16)],
        grid=(l // tm,),
        in_specs=[rows(RET_WIDTH), rows(MLSTM_WIDTH), rows(S5_WIDTH),
                  wrows(RET_WIDTH, 0), wrows(MLSTM_WIDTH, 1), wrows(S5_WIDTH, (RET_WIDTH + MLSTM_WIDTH) // S5_WIDTH),
                  rows(D_MODEL), _layer_spec(ln_w, layer), _layer_spec(ln_b, layer)],
        out_specs=[rows(D_MODEL), rows(D_MODEL)],
        compiler_params=_cparams(("parallel",)),
    )(y_ret, y_m, y_s, w_out, w_out, w_out, x, ln_w, ln_b)


def _down_kernel(h_ref, w_ref, x1_ref, r_ref, rb_ref, acc_ref, *, nk):
    k = pl.program_id(1)

    @pl.when(k == 0)
    def _():
        acc_ref[...] = jnp.zeros_like(acc_ref)

    acc_ref[...] += _dot(h_ref[...], w_ref[...])

    @pl.when(k == nk - 1)
    def _():
        r = ALPHA * x1_ref[...] + acc_ref[...]
        r_ref[...] = r
        rb_ref[...] = r.astype(BF16)


def down_block(hid, w_down, x1, layer, tm, tk):
    l, ff = hid.shape
    nk = ff // tk
    rows = pl.BlockSpec((tm, D_MODEL), lambda i, k: (i, 0))
    return pl.pallas_call(
        functools.partial(_down_kernel, nk=nk),
        out_shape=[jax.ShapeDtypeStruct((l, D_MODEL), F32), jax.ShapeDtypeStruct((l, D_MODEL), BF16)],
        grid=(l // tm, nk),
        in_specs=[pl.BlockSpec((tm, tk), lambda i, k: (i, k)),
                  pl.BlockSpec((None, tk, D_MODEL), lambda i, k: (layer, k, 0)), rows],
        out_specs=[rows, rows],
        scratch_shapes=[pltpu.VMEM((tm, D_MODEL), F32)],
        compiler_params=_cparams(("parallel", "arbitrary")),
    )(hid, w_down, x1)


def _final_kernel(r_ref, rb_ref, p_ref, wg_ref, wp_ref, lw_ref, lb_ref, o_ref, ob_ref):
    gate = _sigmoid(_dot(rb_ref[...], wg_ref[...]))
    ple = gate * _dot(p_ref[...], wp_ref[...])
    out = _layer_norm(r_ref[...] + ple, lw_ref[...], lb_ref[...])
    o_ref[...] = out
    ob_ref[...] = out.astype(BF16)


def final_block(r, rb, p_b, w_gate, w_ple, ln_w, ln_b, layer, tm):
    l = r.shape[0]
    rows = pl.BlockSpec((tm, D_MODEL), lambda i: (i, 0))
    return pl.pallas_call(
        _final_kernel,
        out_shape=[jax.ShapeDtypeStruct((l, D_MODEL), F32), jax.ShapeDtypeStruct((l, D_MODEL), BF16)],
        grid=(l // tm,),
        in_specs=[rows, rows, pl.BlockSpec((None, tm, PLE_DIM), lambda i: (layer, i, 0)),
                  _layer_spec(w_gate, layer), _layer_spec(w_ple, layer), _layer_spec(ln_w, layer),
                  _layer_spec(ln_b, layer)],
        out_specs=[rows, rows],
        compiler_params=_cparams(("parallel",)),
    )(r, rb, p_b, w_gate, w_ple, ln_w, ln_b)


def _rows3(v):
    return v.astype(F32).reshape(v.shape[0], 1, -1)


def _tiles(l):
    return {"mm_m": min(1024, l), "mix": min(512, l), "s5": min(2048, l), "row": min(512, l), "rope": min(2048, l)}


def kernel(x, p, positions, w_in, mlstm_conv_w, mlstm_conv_b, mlstm_i_bias, mlstm_f_bias, ret_norm_w, mlstm_norm_w, s5_lambda_re, s5_lambda_im, s5_log_dt, s5_B_re, s5_B_im, s5_C_re, s5_C_im, s5_D, s5_glu_w, s5_glu_b, w_out, ln1_w, ln1_b, w_up, w_down, w_gate, w_ple, ln2_w, ln2_b):
    bsz, l, _ = x.shape
    depth = w_in.shape[0]
    assert bsz == 1
    t = _tiles(l)
    cos_t, sin_t = rope_tables(positions.astype(F32).reshape(l, 1), t["rope"])
    xf = x.reshape(l, D_MODEL).astype(F32)
    xb = xf.astype(BF16)
    p_b = p.reshape(depth, l, PLE_DIM).astype(BF16)

    w_in_b = w_in.astype(BF16)
    w_tail_b = jnp.pad(w_in[:, :, MAIN_WIDTH:], ((0, 0), (0, 0), (0, TAIL_WIDTH - TAIL_COLS))).astype(BF16)
    w_out_b, w_up_b, w_down_b = w_out.astype(BF16), w_up.astype(BF16), w_down.astype(BF16)
    w_gate_b, w_ple_b, glu_w_b = w_gate.astype(BF16), w_ple.astype(BF16), s5_glu_w.astype(BF16)
    gate_bias = jnp.concatenate([jnp.zeros((depth, GATE_LANE0), F32), mlstm_i_bias.astype(F32),
                                 mlstm_f_bias.astype(F32)], axis=1).reshape(depth, 1, LANES)
    conv_w = mlstm_conv_w.astype(F32)
    conv_b, ret_nw, mlstm_nw = _rows3(mlstm_conv_b), _rows3(ret_norm_w), _rows3(mlstm_norm_w)
    s5_d, glu_b = _rows3(s5_D), _rows3(s5_glu_b)
    ln1w, ln1b, ln2w, ln2b = _rows3(ln1_w), _rows3(ln1_b), _rows3(ln2_w), _rows3(ln2_b)
    s5_in = s5_operator_inputs(s5_lambda_re, s5_lambda_im, s5_log_dt, s5_B_re, s5_B_im, s5_C_re, s5_C_im)

    for i in range(depth):
        proj = matmul(xb, w_in_b, i, out_dtype=F32, tm=t["mm_m"], tn=768, n=MAIN_WIDTH)
        tail = matmul(xb, w_tail_b, i, out_dtype=F32, tm=t["mm_m"], tn=TAIL_WIDTH, act="rotate_tail")
        y_ret = retention_block(proj, cos_t, sin_t, ret_nw, i, t["mix"])
        y_m = mlstm_block(proj, tail, conv_w, conv_b, gate_bias, mlstm_nw, i, t["mix"])
        y_s = s5_block(tail, s5_operators(s5_in, i), s5_d, i, t["s5"])
        y_s = glu_block(y_s, glu_w_b, glu_b, i, t["row"])
        x1, x1b = outproj_block(y_ret, y_m, y_s, w_out_b, xf, ln1w, ln1b, i, t["row"])
        hid = matmul(x1b, w_up_b, i, out_dtype=BF16, tm=t["mm_m"], tn=1024, act="relu2")
        r, rb = down_block(hid, w_down_b, x1, i, t["row"], 1024)
        xf, xb = final_block(r, rb, p_b, w_gate_b, w_ple_b, ln2w, ln2b, i, t["row"])
    return xf.reshape(bsz, l, D_MODEL)
```

```python
import functools
import math

import jax
import jax.numpy as jnp
import numpy as np
from jax import lax
from jax.experimental import pallas as pl
from jax.experimental.pallas import tpu as pltpu

F32 = jnp.float32
BF16 = jnp.bfloat16

D_MODEL = 2048
DEPTH = 2
HEAD_DIM = 128
RET_HEADS = 6
MLSTM_HEADS = 6
RET_WIDTH = RET_HEADS * HEAD_DIM
MLSTM_WIDTH = MLSTM_HEADS * HEAD_DIM
S5_WIDTH = D_MODEL - RET_WIDTH - MLSTM_WIDTH
S5_GROUP = 16
S5_GROUPS = S5_WIDTH // S5_GROUP
S5_STATE = 64
CONV_WIDTH = 4
CHUNK = 128
D_FF = 4 * D_MODEL
PLE_DIM = 256
ROPE_BASE = 10000.0
LN_EPS = 1e-5
ALPHA = (2 * DEPTH) ** 0.25
QK_SCALE = HEAD_DIM ** -0.5

LANES = 128
SUBLANES = 8
MAIN_WIDTH = 4 * RET_WIDTH + 4 * MLSTM_WIDTH
TAIL_COLS = 2 * MLSTM_HEADS + S5_WIDTH
TAIL_WIDTH = 5 * LANES
GATE_LANE0 = LANES - 2 * MLSTM_HEADS
GATE_COLBLK = S5_WIDTH // LANES
CONV_ROWS = 64
S5_T = 16
S5_GB = S5_WIDTH // LANES
S5_GPB = LANES // S5_GROUP
S5_SB = S5_GPB * S5_STATE
S5_KT = S5_T * LANES
VMEM_LIMIT = 56 * 1024 * 1024


def _cparams(sem):
    return pltpu.CompilerParams(dimension_semantics=sem, vmem_limit_bytes=VMEM_LIMIT)


def _mm_kernel(a_ref, b_ref, o_ref, *scratch, nk, act):
    def finish(r):
        if act == "relu2":
            r = jnp.square(jnp.maximum(r, 0.0))
        elif act == "rotate_tail":
            r = pltpu.roll(r, TAIL_WIDTH - 2 * MLSTM_HEADS, 1)
        o_ref[...] = r.astype(o_ref.dtype)

    if nk == 1:
        finish(jnp.dot(a_ref[...], b_ref[...], preferred_element_type=F32))
        return
    acc_ref, = scratch
    k = pl.program_id(2)

    @pl.when(k == 0)
    def _():
        acc_ref[...] = jnp.zeros_like(acc_ref)

    acc_ref[...] += jnp.dot(a_ref[...], b_ref[...], preferred_element_type=F32)

    @pl.when(k == nk - 1)
    def _():
        finish(acc_ref[...])


def matmul(a, b, layer, *, out_dtype, tm, tn, n=None, tk=None, act=None):
    m, kdim = a.shape
    n = b.shape[2] if n is None else n
    tk = kdim if tk is None else tk
    nk = kdim // tk
    assert m % tm == 0 and n % tn == 0 and kdim % tk == 0 and b.shape[1] == kdim
    return pl.pallas_call(
        functools.partial(_mm_kernel, nk=nk, act=act),
        out_shape=jax.ShapeDtypeStruct((m, n), out_dtype),
        grid=(m // tm, n // tn, nk),
        in_specs=[pl.BlockSpec((tm, tk), lambda i, j, k: (i, k)),
                  pl.BlockSpec((None, tk, tn), lambda i, j, k: (layer, k, j))],
        out_specs=pl.BlockSpec((tm, tn), lambda i, j, k: (i, j)),
        scratch_shapes=[] if nk == 1 else [pltpu.VMEM((tm, tn), F32)],
        compiler_params=_cparams(("parallel", "parallel", "arbitrary")),
    )(a, b)


def _rope_kernel(pos_ref, inv_ref, sign_ref, cos_ref, sin_ref):
    ang = pos_ref[...] * inv_ref[...]
    cos_ref[...] = jnp.cos(ang)
    sin_ref[...] = jnp.sin(ang) * sign_ref[...]


def rope_tables(pos_col, tl):
    l = pos_col.shape[0]
    half = np.arange(0, HEAD_DIM, 2, dtype=np.float32) / np.float32(HEAD_DIM)
    inv = (np.float32(ROPE_BASE) ** (-half)).astype(np.float32)
    inv2 = jnp.asarray(np.concatenate([inv, inv])[None, :])
    sign = jnp.asarray(np.concatenate([-np.ones(64, np.float32), np.ones(64, np.float32)])[None, :])
    row = pl.BlockSpec((1, LANES), lambda i: (0, 0))
    return pl.pallas_call(
        _rope_kernel,
        out_shape=[jax.ShapeDtypeStruct((l, LANES), F32)] * 2,
        grid=(l // tl,),
        in_specs=[pl.BlockSpec((tl, 1), lambda i: (i, 0)), row, row],
        out_specs=[pl.BlockSpec((tl, LANES), lambda i: (i, 0))] * 2,
        compiler_params=_cparams(("parallel",)),
    )(pos_col, inv2, sign)


def _head_norm(y):
    mu = jnp.mean(y, axis=-1, keepdims=True)
    d = y - mu
    var = jnp.mean(d * d, axis=-1, keepdims=True)
    return d * lax.rsqrt(var + LN_EPS)


def _mean_lanes_mxu(x):
    hi = x.astype(BF16)
    lo = (x - hi.astype(F32)).astype(BF16)
    avg = jnp.full((2 * HEAD_DIM, HEAD_DIM), 1.0 / HEAD_DIM, BF16)
    return _dot(jnp.concatenate([hi, lo], axis=1), avg)


def _head_norm_mxu(y):
    d = y - _mean_lanes_mxu(y)
    return d * lax.rsqrt(_mean_lanes_mxu(d * d) + LN_EPS)


def _sigmoid(x):
    return 1.0 / (1.0 + jnp.exp(-x))


def _dot_nt(a, b, precision=None):
    return lax.dot_general(a, b, (((1,), (1,)), ((), ())), precision=precision, preferred_element_type=F32)


def _dot(a, b):
    return jnp.dot(a, b, preferred_element_type=F32)


def _dot_tn(a, b):
    return lax.dot_general(a, b, (((0,), (0,)), ((), ())), preferred_element_type=F32)


def _head_lanes(h):
    return slice(h * HEAD_DIM, (h + 1) * HEAD_DIM)


def _layer_spec(stacked, layer):
    return pl.BlockSpec((None,) + stacked.shape[1:], lambda *_: (layer, 0, 0))


def _ret_log_gamma(h):
    return float(np.log(np.float32(1.0) - np.float32(2.0) ** np.float32(-5.0 - h)))


def _ret_kernel(q_ref, k_ref, v_ref, g_ref, cos_ref, sin_ref, w_ref, o_ref,
                state_ref, decay_ref, qdec_ref, kdec_ref, qb_buf, intra_buf, incr_buf, *, nchunk):
    rb = pl.program_id(0)

    @pl.when(rb == 0)
    def _():
        state_ref[...] = jnp.zeros_like(state_ref)
        ii = lax.broadcasted_iota(jnp.int32, (CHUNK, CHUNK), 0)
        jj = lax.broadcasted_iota(jnp.int32, (CHUNK, CHUNK), 1)
        rel = (ii - jj).astype(F32)
        idx = ii.astype(F32)
        for h in range(RET_HEADS):
            lg = _ret_log_gamma(h)
            decay_ref[h] = jnp.where(rel >= 0.0, jnp.exp(lg * jnp.maximum(rel, 0.0)), 0.0)
            qdec_ref[h] = jnp.exp(lg * (idx + 1.0))
            kdec_ref[h] = jnp.exp(lg * (CHUNK - 1.0 - idx))

    for c in range(nchunk):
        rows = slice(c * CHUNK, (c + 1) * CHUNK)
        cos = cos_ref[rows, :]
        sin = sin_ref[rows, :]
        for h in range(RET_HEADS):
            ln = _head_lanes(h)
            q = q_ref[rows, ln]
            k = k_ref[rows, ln]
            qr = q * cos + pltpu.roll(q, HEAD_DIM // 2, 1) * sin
            kr = (k * cos + pltpu.roll(k, HEAD_DIM // 2, 1) * sin) * QK_SCALE
            qb = qr.astype(BF16)
            vb = v_ref[rows, ln].astype(BF16)
            scores = _dot_nt(qb, kr.astype(BF16)) * decay_ref[h]
            qb_buf[rows, ln] = qb
            intra_buf[rows, ln] = _dot(scores.astype(BF16), vb)
            incr_buf[c, h] = _dot_tn((kr * kdec_ref[h]).astype(BF16), vb)

    for c in range(nchunk):
        rows = slice(c * CHUNK, (c + 1) * CHUNK)
        for h in range(RET_HEADS):
            ln = _head_lanes(h)
            st = state_ref[h]
            out = intra_buf[rows, ln] + _dot(qb_buf[rows, ln], st.astype(BF16)) * qdec_ref[h]
            state_ref[h] = st * math.exp(_ret_log_gamma(h) * CHUNK) + incr_buf[c, h]
            g = g_ref[rows, ln]
            y = _head_norm_mxu(out) * w_ref[:, ln] * (g * _sigmoid(g))
            o_ref[rows, ln] = y.astype(o_ref.dtype)


def retention_block(proj, cos_t, sin_t, norm_w, layer, tb):
    l = proj.shape[0]

    def col(j):
        return pl.BlockSpec((tb, RET_WIDTH), lambda r, j=j: (r, j))

    tab = pl.BlockSpec((tb, LANES), lambda r: (r, 0))
    hsq = pltpu.VMEM((RET_HEADS, CHUNK, CHUNK), F32)
    return pl.pallas_call(
        functools.partial(_ret_kernel, nchunk=tb // CHUNK),
        out_shape=jax.ShapeDtypeStruct((l, RET_WIDTH), BF16),
        grid=(l // tb,),
        in_specs=[col(0), col(1), col(2), col(3), tab, tab, _layer_spec(norm_w, layer)],
        out_specs=pl.BlockSpec((tb, RET_WIDTH), lambda r: (r, 0)),
        scratch_shapes=[hsq, hsq, hsq, hsq, pltpu.VMEM((tb, RET_WIDTH), BF16), pltpu.VMEM((tb, RET_WIDTH), F32),
                        pltpu.VMEM((tb // CHUNK, RET_HEADS, HEAD_DIM, HEAD_DIM), F32)],
        compiler_params=_cparams(("arbitrary",)),
    )(proj, proj, proj, proj, cos_t, sin_t, norm_w)


def _log_sigmoid(x):
    return -(jnp.maximum(-x, 0.0) + jnp.log1p(jnp.exp(-jnp.abs(x))))


def _mlstm_kernel(q_ref, k_ref, v_ref, og_ref, gate_ref, cw_ref, cb_ref, gb_ref, nw_ref,
                  o_ref, c_st, m_st, qbuf, kbuf, qs, ks, gs, r_buf, e_buf, cmax_buf, bcum_buf, p_buf, u_buf,
                  *, nchunk, tb):
    rb = pl.program_id(0)
    pad = SUBLANES
    nh = MLSTM_HEADS

    @pl.when(rb == 0)
    def _():
        c_st[...] = jnp.zeros_like(c_st)
        m_st[...] = jnp.zeros_like(m_st)
        qbuf[0:pad, :] = jnp.zeros((pad, MLSTM_WIDTH), F32)
        kbuf[0:pad, :] = jnp.zeros((pad, MLSTM_WIDTH), F32)

    qbuf[pad:pad + tb, :] = q_ref[...]
    kbuf[pad:pad + tb, :] = k_ref[...]
    for r0 in range(0, tb, CONV_ROWS):
        for buf, woff, dst, scale in ((qbuf, 0, qs, None), (kbuf, MLSTM_WIDTH, ks, QK_SCALE)):
            wl = slice(woff, woff + MLSTM_WIDTH)
            acc = jnp.broadcast_to(cb_ref[:, wl], (CONV_ROWS, MLSTM_WIDTH))
            for tap in range(CONV_WIDTH):
                off = r0 + pad - (CONV_WIDTH - 1) + tap
                acc = acc + buf[off:off + CONV_ROWS, :] * cw_ref[tap:tap + 1, wl]
            act = acc * _sigmoid(acc)
            dst[r0:r0 + CONV_ROWS, :] = act if scale is None else act * scale
    qbuf[0:pad, :] = q_ref[tb - pad:tb, :]
    kbuf[0:pad, :] = k_ref[tb - pad:tb, :]

    lane_row = lax.broadcasted_iota(jnp.int32, (1, LANES), 1)
    graw = pltpu.roll(gate_ref[...] + gb_ref[...], LANES - GATE_LANE0, 1)
    gs[...] = jnp.where(lane_row < nh, graw, jnp.where(lane_row < 2 * nh, _log_sigmoid(graw), 0.0))

    ii = lax.broadcasted_iota(jnp.int32, (CHUNK, CHUNK), 0)
    jj = lax.broadcasted_iota(jnp.int32, (CHUNK, CHUNK), 1)
    causal = ii >= jj
    tril = causal.astype(F32)
    lane8 = lax.broadcasted_iota(jnp.int32, (SUBLANES, LANES), 1)
    ones_b = jnp.ones((CHUNK, HEAD_DIM), BF16)
    full = (CHUNK, CHUNK)

    def chunk_rows(c):
        return slice(c * CHUNK, (c + 1) * CHUNK)

    def aug_lanes(h):
        return slice(h * 2 * HEAD_DIM, (h + 1) * 2 * HEAD_DIM)

    for c in range(nchunk):
        rows = chunk_rows(c)
        g = gs[rows, :]
        gcum = jnp.dot(tril, g, precision=lax.Precision.HIGHEST, preferred_element_type=F32)
        bcum = pltpu.roll(gcum, LANES - nh, 1)
        r8 = jnp.transpose(g - bcum)[0:SUBLANES, :]
        cmax = r8
        for sh in (1, 2, 4, 8, 16, 32, 64):
            cmax = jnp.maximum(cmax, jnp.where(lane8 >= sh, pltpu.roll(cmax, sh, 1), -jnp.inf))
        r_buf[c] = r8
        e_buf[c] = jnp.exp(r8 - cmax[:, CHUNK - 1:CHUNK])
        cmax_buf[rows, :] = jnp.transpose(
            jnp.concatenate([cmax, jnp.zeros((CHUNK - SUBLANES, LANES), F32)], axis=0))
        bcum_buf[rows, :] = bcum

    for c in range(nchunk):
        rows = chunk_rows(c)
        for h in range(nh):
            ln = _head_lanes(h)
            kb = ks[rows, ln].astype(BF16)
            v_aug = jnp.concatenate([v_ref[rows, ln].astype(BF16), ones_b], axis=1)
            cm_bc = jnp.broadcast_to(cmax_buf[rows, h:h + 1], full)
            dloc = jnp.exp(jnp.where(causal, r_buf[c, h:h + 1, :] - cm_bc, -jnp.inf))
            s = _dot_nt(qs[rows, ln].astype(BF16), kb) * dloc
            p_buf[rows, aug_lanes(h)] = _dot(s.astype(BF16), v_aug)
            kw_t = (jnp.transpose(kb).astype(F32) * e_buf[c, h:h + 1, :]).astype(BF16)
            u_buf[c, h] = _dot(kw_t, v_aug)

    for c in range(nchunk):
        rows = chunk_rows(c)
        m_prev = m_st[0:1, :]
        cmax_c = cmax_buf[rows, :]
        bcum = bcum_buf[rows, :]
        delta = cmax_c - m_prev
        m_all = jnp.maximum(cmax_c, m_prev)
        floor_all = jnp.exp(-(bcum + m_all))
        m_last = m_all[CHUNK - 1:CHUNK, :]
        keep_row = jnp.exp(m_prev - m_last)
        gain_row = jnp.exp(cmax_c[CHUNK - 1:CHUNK, :] - m_last)
        m_st[0:1, :] = bcum[CHUNK - 1:CHUNK, :] + m_last
        for h in range(nh):
            ln = _head_lanes(h)
            d_bc = jnp.broadcast_to(delta[:, h:h + 1], full)
            f_bc = jnp.exp(jnp.minimum(d_bc, 0.0))
            w_bc = jnp.exp(jnp.minimum(-d_bc, 0.0))
            cs = c_st[h]
            inter = _dot((w_bc * qs[rows, ln]).astype(BF16), cs.astype(BF16))
            intra = p_buf[rows, aug_lanes(h)]
            num = f_bc * intra[:, :HEAD_DIM] + inter[:, :HEAD_DIM]
            den = f_bc * intra[:, HEAD_DIM:] + inter[:, HEAD_DIM:]
            inv = 1.0 / jnp.maximum(jnp.abs(den[:, h:h + 1]), floor_all[:, h:h + 1])
            c_st[h] = keep_row[:, h:h + 1] * cs + gain_row[:, h:h + 1] * u_buf[c, h]
            y = _head_norm_mxu(num * inv) * nw_ref[:, ln] * _sigmoid(og_ref[rows, ln])
            o_ref[rows, ln] = y.astype(o_ref.dtype)


def mlstm_block(proj, tail, conv_w, conv_b, gate_bias, norm_w, layer, tb):
    l = proj.shape[0]
    base = (4 * RET_WIDTH) // MLSTM_WIDTH

    def col(j):
        return pl.BlockSpec((tb, MLSTM_WIDTH), lambda r, j=j: (r, base + j))

    nchunk = tb // CHUNK
    wide = pltpu.VMEM((tb, MLSTM_WIDTH), F32)
    hist = pltpu.VMEM((tb + SUBLANES, MLSTM_WIDTH), F32)
    narrow = pltpu.VMEM((tb, LANES), F32)
    rows8 = pltpu.VMEM((nchunk, SUBLANES, LANES), F32)
    return pl.pallas_call(
        functools.partial(_mlstm_kernel, nchunk=nchunk, tb=tb),
        out_shape=jax.ShapeDtypeStruct((l, MLSTM_WIDTH), BF16),
        grid=(l // tb,),
        in_specs=[col(0), col(1), col(2), col(3),
                  pl.BlockSpec((tb, LANES), lambda r: (r, GATE_COLBLK)),
                  _layer_spec(conv_w, layer), _layer_spec(conv_b, layer), _layer_spec(gate_bias, layer),
                  _layer_spec(norm_w, layer)],
        out_specs=pl.BlockSpec((tb, MLSTM_WIDTH), lambda r: (r, 0)),
        scratch_shapes=[pltpu.VMEM((MLSTM_HEADS, HEAD_DIM, 2 * HEAD_DIM), F32), pltpu.VMEM((SUBLANES, LANES), F32),
                        hist, hist, wide, wide, narrow, rows8, rows8, narrow, narrow,
                        pltpu.VMEM((tb, 2 * MLSTM_WIDTH), F32),
                        pltpu.VMEM((nchunk, MLSTM_HEADS, HEAD_DIM, 2 * HEAD_DIM), F32)],
        compiler_params=_cparams(("arbitrary",)),
    )(proj, proj, proj, proj, tail, conv_w, conv_b, gate_bias, norm_w)


def _s5_prep_kernel(lre_ref, lim_ref, ldt_ref, btre_ref, btim_ref, ctre_ref, ctim_ref,
                    toep_ref, vre_ref, vim_ref, wre_ref, wim_ref, are_ref, aim_ref):
    lre = lre_ref[...]
    lim = lim_ref[...]
    dt = jnp.exp(ldt_ref[...])
    mag = jnp.exp(lre * dt)
    ang = lim * dt
    zr = mag * jnp.cos(ang) - 1.0
    zi = mag * jnp.sin(ang)
    den = lre * lre + lim * lim
    w_re = (zr * lre + zi * lim) / den
    w_im = (zi * lre - zr * lim) / den
    row_g = lax.broadcasted_iota(jnp.int32, (LANES, S5_SB), 0) // S5_GROUP
    col_g = lax.broadcasted_iota(jnp.int32, (LANES, S5_SB), 1) // S5_STATE
    same = row_g == col_g
    bt_re = btre_ref[...]
    bt_im = btim_ref[...]
    bb_re = jnp.where(same, w_re * bt_re - w_im * bt_im, 0.0)
    bb_im = jnp.where(same, w_re * bt_im + w_im * bt_re, 0.0)
    ct_re = jnp.where(same, ctre_ref[...], 0.0)
    ct_im = jnp.where(same, ctim_ref[...], 0.0)
    hp = lax.Precision.HIGHEST
    kblk = []
    for d in range(S5_T + 1):
        pm = jnp.exp(lre * dt * float(d))
        pa = lim * dt * float(d)
        p_re = pm * jnp.cos(pa)
        p_im = pm * jnp.sin(pa)
        if d < S5_T:
            ab_re = p_re * bb_re - p_im * bb_im
            ab_im = p_re * bb_im + p_im * bb_re
            srow = slice((S5_T - 1 - d) * LANES, (S5_T - d) * LANES)
            vre_ref[0, srow, :] = ab_re.astype(BF16)
            vim_ref[0, srow, :] = ab_im.astype(BF16)
            kblk.append((_dot_nt(ab_re, ct_re, hp) - _dot_nt(ab_im, ct_im, hp)).astype(BF16))
        if d >= 1:
            trow = slice((d - 1) * LANES, d * LANES)
            wre_ref[0, trow, :] = (ct_re * p_re - ct_im * p_im).astype(BF16)
            wim_ref[0, trow, :] = (-(ct_re * p_im + ct_im * p_re)).astype(BF16)
        if d == S5_T:
            are_ref[...] = p_re
            aim_ref[...] = p_im
    zero = jnp.zeros((LANES, LANES), BF16)
    for s in range(S5_T):
        for t in range(S5_T):
            toep_ref[0, s * LANES:(s + 1) * LANES, t * LANES:(t + 1) * LANES] = kblk[t - s] if t >= s else zero


def s5_operator_inputs(lam_re, lam_im, log_dt, b_re, b_im, c_re, c_im):
    depth = lam_re.shape[0]
    nst = S5_GROUPS * S5_STATE

    def tiled(m):
        return jnp.tile(m.astype(F32).reshape(depth, S5_WIDTH, S5_STATE), (1, 1, S5_GPB))

    lre = lam_re.astype(F32).reshape(depth, 1, nst)
    lim = lam_im.astype(F32).reshape(depth, 1, nst)
    ldt = jnp.repeat(log_dt.astype(F32), S5_STATE, axis=-1).reshape(depth, 1, nst)
    return (lre, lim, ldt, tiled(jnp.swapaxes(b_re, -1, -2)), tiled(jnp.swapaxes(b_im, -1, -2)),
            tiled(c_re), tiled(c_im))


def s5_operators(prep_inputs, layer):
    nst = S5_GROUPS * S5_STATE
    lane_in = pl.BlockSpec((None, 1, S5_SB), lambda g: (layer, 0, g))
    par_in = pl.BlockSpec((None, LANES, S5_SB), lambda g: (layer, g, 0))
    lane_out = pl.BlockSpec((1, S5_SB), lambda g: (0, g))

    def out_blk(rows, cols):
        return pl.BlockSpec((1, rows, cols), lambda g: (g, 0, 0))

    return pl.pallas_call(
        _s5_prep_kernel,
        out_shape=[jax.ShapeDtypeStruct((S5_GB, S5_KT, S5_KT), BF16)]
        + [jax.ShapeDtypeStruct((S5_GB, S5_KT, S5_SB), BF16)] * 4
        + [jax.ShapeDtypeStruct((1, nst), F32)] * 2,
        grid=(S5_GB,),
        in_specs=[lane_in, lane_in, lane_in, par_in, par_in, par_in, par_in],
        out_specs=[out_blk(S5_KT, S5_KT)] + [out_blk(S5_KT, S5_SB)] * 4 + [lane_out, lane_out],
        compiler_params=_cparams(("parallel",)),
    )(*prep_inputs)


def _gelu_tanh(x):
    c = math.sqrt(2.0 / math.pi)
    return 0.5 * x * (1.0 + jnp.tanh(c * (x + 0.044715 * (x * x * x))))


def _s5_kernel(u_ref, toep_ref, vre_ref, vim_ref, wre_ref, wim_ref, are_ref, aim_ref, d_ref, y_ref,
               car_re, car_im, z_re, z_im, x_re, x_im, *, tmc):
    rb = pl.program_id(1)

    @pl.when(rb == 0)
    def _():
        car_re[...] = jnp.zeros_like(car_re)
        car_im[...] = jnp.zeros_like(car_im)

    us = [u_ref[pl.ds(s, tmc, stride=S5_T), :] for s in range(S5_T)]
    ucat = jnp.concatenate([u.astype(BF16) for u in us], axis=1)
    z_re[...] = _dot(ucat, vre_ref[0])
    z_im[...] = _dot(ucat, vim_ref[0])
    a_r = are_ref[...]
    a_i = aim_ref[...]

    def step(c, carry):
        xr, xi = carry
        row = pl.ds(c, 1)
        x_re[row, :] = xr
        x_im[row, :] = xi
        return (a_r * xr - a_i * xi + z_re[row, :], a_r * xi + a_i * xr + z_im[row, :])

    xr, xi = lax.fori_loop(0, tmc, step, (car_re[0:1, :], car_im[0:1, :]))
    car_re[0:1, :] = xr
    car_im[0:1, :] = xi
    y = (_dot(ucat, toep_ref[0]) + _dot_nt(x_re[...].astype(BF16), wre_ref[0])
         + _dot_nt(x_im[...].astype(BF16), wim_ref[0]))
    for t in range(S5_T):
        yt = y[:, t * LANES:(t + 1) * LANES] + d_ref[...] * us[t]
        y_ref[pl.ds(t, tmc, stride=S5_T), :] = _gelu_tanh(yt)


def s5_block(tail, ops, d_skip, layer, tb):
    l = tail.shape[0]
    toep, v_re, v_im, w_re, w_im, a_re, a_im = ops
    tmc = tb // S5_T

    def per_gb(cols):
        return pl.BlockSpec((1, S5_KT, cols), lambda g, r: (g, 0, 0))

    lane_blk = pl.BlockSpec((1, S5_SB), lambda g, r: (0, g))
    fold = pltpu.VMEM((tmc, S5_SB), F32)
    carry = pltpu.VMEM((SUBLANES, S5_SB), F32)
    return pl.pallas_call(
        functools.partial(_s5_kernel, tmc=tmc),
        out_shape=jax.ShapeDtypeStruct((l, S5_WIDTH), F32),
        grid=(S5_GB, l // tb),
        in_specs=[pl.BlockSpec((tb, LANES), lambda g, r: (r, g)),
                  per_gb(S5_KT), per_gb(S5_SB), per_gb(S5_SB), per_gb(S5_SB), per_gb(S5_SB),
                  lane_blk, lane_blk, pl.BlockSpec((None, 1, LANES), lambda g, r: (layer, 0, g))],
        out_specs=pl.BlockSpec((tb, LANES), lambda g, r: (r, g)),
        scratch_shapes=[carry, carry, fold, fold, fold, fold],
        compiler_params=_cparams(("parallel", "arbitrary")),
    )(tail, toep, v_re, v_im, w_re, w_im, a_re, a_im, d_skip)


def _glu_kernel(y_ref, w_ref, b_ref, o_ref):
    y = y_ref[...]
    z = _dot(y.astype(BF16), w_ref[...]) + b_ref[...]
    o_ref[...] = (y * _sigmoid(z)).astype(o_ref.dtype)


def glu_block(y, w, b, layer, tm):
    l, n = y.shape
    return pl.pallas_call(
        _glu_kernel,
        out_shape=jax.ShapeDtypeStruct((l, n), BF16),
        grid=(l // tm,),
        in_specs=[pl.BlockSpec((tm, n), lambda i: (i, 0)), _layer_spec(w, layer), _layer_spec(b, layer)],
        out_specs=pl.BlockSpec((tm, n), lambda i: (i, 0)),
        compiler_params=_cparams(("parallel",)),
    )(y, w, b)


def _layer_norm(z, w, b):
    mu = jnp.mean(z, axis=-1, keepdims=True)
    d = z - mu
    var = jnp.mean(d * d, axis=-1, keepdims=True)
    return d * lax.rsqrt(var + LN_EPS) * w + b


def _outproj_kernel(a1_ref, a2_ref, a3_ref, w1_ref, w2_ref, w3_ref, x_ref, lw_ref, lb_ref, o_ref, ob_ref):
    mix = _dot(a1_ref[...], w1_ref[...]) + _dot(a2_ref[...], w2_ref[...]) + _dot(a3_ref[...], w3_ref[...])
    x1 = _layer_norm(ALPHA * x_ref[...] + mix, lw_ref[...], lb_ref[...])
    o_ref[...] = x1
    ob_ref[...] = x1.astype(BF16)


def outproj_block(y_ret, y_m, y_s, w_out, x, ln_w, ln_b, layer, tm):
    l = x.shape[0]

    def rows(n):
        return pl.BlockSpec((tm, n), lambda i: (i, 0))

    def wrows(n, blk):
        return pl.BlockSpec((None, n, D_MODEL), lambda i: (layer, blk, 0))

    return pl.pallas_call(
        _outproj_kernel,
        out_shape=[jax.ShapeDtypeStruct((l, D_MODEL), F32), jax.ShapeDtypeStruct((l, D_MODEL), BF16)],
        grid=(l // tm,),
        in_specs=[rows(RET_WIDTH), rows(MLSTM_WIDTH), rows(S5_WIDTH),
                  wrows(RET_WIDTH, 0), wrows(MLSTM_WIDTH, 1), wrows(S5_WIDTH, (RET_WIDTH + MLSTM_WIDTH) // S5_WIDTH),
                  rows(D_MODEL), _layer_spec(ln_w, layer), _layer_spec(ln_b, layer)],
        out_specs=[rows(D_MODEL), rows(D_MODEL)],
        compiler_params=_cparams(("parallel",)),
    )(y_ret, y_m, y_s, w_out, w_out, w_out, x, ln_w, ln_b)


def _down_kernel(h_ref, w_ref, x1_ref, r_ref, rb_ref):
    r = ALPHA * x1_ref[...] + _dot(h_ref[...], w_ref[...])
    r_ref[...] = r
    rb_ref[...] = r.astype(BF16)


def down_block(hid, w_down, x1, layer, tm):
    l, ff = hid.shape
    rows = pl.BlockSpec((tm, D_MODEL), lambda i: (i, 0))
    return pl.pallas_call(
        _down_kernel,
        out_shape=[jax.ShapeDtypeStruct((l, D_MODEL), F32), jax.ShapeDtypeStruct((l, D_MODEL), BF16)],
        grid=(l // tm,),
        in_specs=[pl.BlockSpec((tm, ff), lambda i: (i, 0)),
                  pl.BlockSpec((None, ff, D_MODEL), lambda i: (layer, 0, 0), pipeline_mode=pl.Buffered(1)), rows],
        out_specs=[rows, rows],
        compiler_params=_cparams(("parallel",)),
    )(hid, w_down, x1)


def _final_kernel(r_ref, rb_ref, p_ref, wg_ref, wp_ref, lw_ref, lb_ref, o_ref, ob_ref):
    gate = _sigmoid(_dot(rb_ref[...], wg_ref[...]))
    ple = gate * _dot(p_ref[...], wp_ref[...])
    out = _layer_norm(r_ref[...] + ple, lw_ref[...], lb_ref[...])
    o_ref[...] = out
    ob_ref[...] = out.astype(BF16)


def final_block(r, rb, p_b, w_gate, w_ple, ln_w, ln_b, layer, tm):
    l = r.shape[0]
    rows = pl.BlockSpec((tm, D_MODEL), lambda i: (i, 0))
    return pl.pallas_call(
        _final_kernel,
        out_shape=[jax.ShapeDtypeStruct((l, D_MODEL), F32), jax.ShapeDtypeStruct((l, D_MODEL), BF16)],
        grid=(l // tm,),
        in_specs=[rows, rows, pl.BlockSpec((None, tm, PLE_DIM), lambda i: (layer, i, 0)),
                  _layer_spec(w_gate, layer), _layer_spec(w_ple, layer), _layer_spec(ln_w, layer),
                  _layer_spec(ln_b, layer)],
        out_specs=[rows, rows],
        compiler_params=_cparams(("parallel",)),
    )(r, rb, p_b, w_gate, w_ple, ln_w, ln_b)


def _rows3(v):
    return v.astype(F32).reshape(v.shape[0], 1, -1)


def _tiles(l):
    return {"mm_m": min(1024, l), "mix": min(512, l), "s5": min(2048, l), "row": min(512, l), "down": min(256, l),
            "rope": min(2048, l)}


def kernel(x, p, positions, w_in, mlstm_conv_w, mlstm_conv_b, mlstm_i_bias, mlstm_f_bias, ret_norm_w, mlstm_norm_w, s5_lambda_re, s5_lambda_im, s5_log_dt, s5_B_re, s5_B_im, s5_C_re, s5_C_im, s5_D, s5_glu_w, s5_glu_b, w_out, ln1_w, ln1_b, w_up, w_down, w_gate, w_ple, ln2_w, ln2_b):
    bsz, l, _ = x.shape
    depth = w_in.shape[0]
    assert bsz == 1
    t = _tiles(l)
    cos_t, sin_t = rope_tables(positions.astype(F32).reshape(l, 1), t["rope"])
    xf = x.reshape(l, D_MODEL).astype(F32)
    xb = xf.astype(BF16)
    p_b = p.reshape(depth, l, PLE_DIM).astype(BF16)

    w_in_b = w_in.astype(BF16)
    w_tail_b = jnp.pad(w_in[:, :, MAIN_WIDTH:], ((0, 0), (0, 0), (0, TAIL_WIDTH - TAIL_COLS))).astype(BF16)
    w_out_b, w_up_b, w_down_b = w_out.astype(BF16), w_up.astype(BF16), w_down.astype(BF16)
    w_gate_b, w_ple_b, glu_w_b = w_gate.astype(BF16), w_ple.astype(BF16), s5_glu_w.astype(BF16)
    gate_bias = jnp.concatenate([jnp.zeros((depth, GATE_LANE0), F32), mlstm_i_bias.astype(F32),
                                 mlstm_f_bias.astype(F32)], axis=1).reshape(depth, 1, LANES)
    conv_w = mlstm_conv_w.astype(F32)
    conv_b, ret_nw, mlstm_nw = _rows3(mlstm_conv_b), _rows3(ret_norm_w), _rows3(mlstm_norm_w)
    s5_d, glu_b = _rows3(s5_D), _rows3(s5_glu_b)
    ln1w, ln1b, ln2w, ln2b = _rows3(ln1_w), _rows3(ln1_b), _rows3(ln2_w), _rows3(ln2_b)
    s5_in = s5_operator_inputs(s5_lambda_re, s5_lambda_im, s5_log_dt, s5_B_re, s5_B_im, s5_C_re, s5_C_im)

    for i in range(depth):
        proj = matmul(xb, w_in_b, i, out_dtype=F32, tm=t["mm_m"], tn=768, n=MAIN_WIDTH)
        tail = matmul(xb, w_tail_b, i, out_dtype=F32, tm=t["mm_m"], tn=TAIL_WIDTH, act="rotate_tail")
        y_ret = retention_block(proj, cos_t, sin_t, ret_nw, i, t["mix"])
        y_m = mlstm_block(proj, tail, conv_w, conv_b, gate_bias, mlstm_nw, i, t["mix"])
        y_s = s5_block(tail, s5_operators(s5_in, i), s5_d, i, t["s5"])
        y_s = glu_block(y_s, glu_w_b, glu_b, i, t["row"])
        x1, x1b = outproj_block(y_ret, y_m, y_s, w_out_b, xf, ln1w, ln1b, i, t["row"])
        hid = matmul(x1b, w_up_b, i, out_dtype=BF16, tm=t["mm_m"], tn=1024, act="relu2")
        r, rb = down_block(hid, w_down_b, x1, i, t["down"])
        xf, xb = final_block(r, rb, p_b, w_gate_b, w_ple_b, ln2w, ln2b, i, t["row"])
    return xf.reshape(bsz, l, D_MODEL)
```

```python
import functools
import math

import jax
import jax.numpy as jnp
import numpy as np
from jax import lax
from jax.experimental import pallas as pl
from jax.experimental.pallas import tpu as pltpu

F32 = jnp.float32
BF16 = jnp.bfloat16

D_MODEL = 2048
DEPTH = 2
HEAD_DIM = 128
RET_HEADS = 6
MLSTM_HEADS = 6
RET_WIDTH = RET_HEADS * HEAD_DIM
MLSTM_WIDTH = MLSTM_HEADS * HEAD_DIM
S5_WIDTH = D_MODEL - RET_WIDTH - MLSTM_WIDTH
S5_GROUP = 16
S5_GROUPS = S5_WIDTH // S5_GROUP
S5_STATE = 64
CONV_WIDTH = 4
CHUNK = 128
D_FF = 4 * D_MODEL
PLE_DIM = 256
ROPE_BASE = 10000.0
LN_EPS = 1e-5
ALPHA = (2 * DEPTH) ** 0.25
QK_SCALE = HEAD_DIM ** -0.5

LANES = 128
SUBLANES = 8
MAIN_WIDTH = 4 * RET_WIDTH + 4 * MLSTM_WIDTH
TAIL_COLS = 2 * MLSTM_HEADS + S5_WIDTH
TAIL_WIDTH = 5 * LANES
GATE_LANE0 = LANES - 2 * MLSTM_HEADS
GATE_COLBLK = S5_WIDTH // LANES
CONV_ROWS = 64
S5_T = 16
S5_GB = S5_WIDTH // LANES
S5_GPB = LANES // S5_GROUP
S5_SB = S5_GPB * S5_STATE
S5_KT = S5_T * LANES
VMEM_LIMIT = 56 * 1024 * 1024


def _cparams(sem):
    return pltpu.CompilerParams(dimension_semantics=sem, vmem_limit_bytes=VMEM_LIMIT)


def _mm_kernel(a_ref, b_ref, o_ref, *scratch, nk, act):
    def finish(r):
        if act == "relu2":
            r = jnp.square(jnp.maximum(r, 0.0))
        elif act == "rotate_tail":
            r = pltpu.roll(r, TAIL_WIDTH - 2 * MLSTM_HEADS, 1)
        o_ref[...] = r.astype(o_ref.dtype)

    if nk == 1:
        finish(jnp.dot(a_ref[...], b_ref[...], preferred_element_type=F32))
        return
    acc_ref, = scratch
    k = pl.program_id(2)

    @pl.when(k == 0)
    def _():
        acc_ref[...] = jnp.zeros_like(acc_ref)

    acc_ref[...] += jnp.dot(a_ref[...], b_ref[...], preferred_element_type=F32)

    @pl.when(k == nk - 1)
    def _():
        finish(acc_ref[...])


def matmul(a, b, layer, *, out_dtype, tm, tn, n=None, tk=None, act=None):
    m, kdim = a.shape
    n = b.shape[2] if n is None else n
    tk = kdim if tk is None else tk
    nk = kdim // tk
    assert m % tm == 0 and n % tn == 0 and kdim % tk == 0 and b.shape[1] == kdim
    return pl.pallas_call(
        functools.partial(_mm_kernel, nk=nk, act=act),
        out_shape=jax.ShapeDtypeStruct((m, n), out_dtype),
        grid=(m // tm, n // tn, nk),
        in_specs=[pl.BlockSpec((tm, tk), lambda i, j, k: (i, k)),
                  pl.BlockSpec((None, tk, tn), lambda i, j, k: (layer, k, j))],
        out_specs=pl.BlockSpec((tm, tn), lambda i, j, k: (i, j)),
        scratch_shapes=[] if nk == 1 else [pltpu.VMEM((tm, tn), F32)],
        compiler_params=_cparams(("parallel", "parallel", "arbitrary")),
    )(a, b)


def _rope_kernel(pos_ref, inv_ref, sign_ref, cos_ref, sin_ref):
    ang = pos_ref[...] * inv_ref[...]
    cos_ref[...] = jnp.cos(ang)
    sin_ref[...] = jnp.sin(ang) * sign_ref[...]


def rope_tables(pos_col, tl):
    l = pos_col.shape[0]
    half = np.arange(0, HEAD_DIM, 2, dtype=np.float32) / np.float32(HEAD_DIM)
    inv = (np.float32(ROPE_BASE) ** (-half)).astype(np.float32)
    inv2 = jnp.asarray(np.concatenate([inv, inv])[None, :])
    sign = jnp.asarray(np.concatenate([-np.ones(64, np.float32), np.ones(64, np.float32)])[None, :])
    row = pl.BlockSpec((1, LANES), lambda i: (0, 0))
    return pl.pallas_call(
        _rope_kernel,
        out_shape=[jax.ShapeDtypeStruct((l, LANES), F32)] * 2,
        grid=(l // tl,),
        in_specs=[pl.BlockSpec((tl, 1), lambda i: (i, 0)), row, row],
        out_specs=[pl.BlockSpec((tl, LANES), lambda i: (i, 0))] * 2,
        compiler_params=_cparams(("parallel",)),
    )(pos_col, inv2, sign)


def _head_norm(y):
    mu = jnp.mean(y, axis=-1, keepdims=True)
    d = y - mu
    var = jnp.mean(d * d, axis=-1, keepdims=True)
    return d * lax.rsqrt(var + LN_EPS)


def _mean_lanes_mxu(x):
    hi = x.astype(BF16)
    lo = (x - hi.astype(F32)).astype(BF16)
    avg = jnp.full((2 * HEAD_DIM, HEAD_DIM), 1.0 / HEAD_DIM, BF16)
    return _dot(jnp.concatenate([hi, lo], axis=1), avg)


def _head_norm_mxu(y):
    d = y - _mean_lanes_mxu(y)
    return d * lax.rsqrt(_mean_lanes_mxu(d * d) + LN_EPS)


def _sigmoid(x):
    return 1.0 / (1.0 + jnp.exp(-x))


def _dot_nt(a, b, precision=None):
    return lax.dot_general(a, b, (((1,), (1,)), ((), ())), precision=precision, preferred_element_type=F32)


def _dot(a, b):
    return jnp.dot(a, b, preferred_element_type=F32)


def _dot_tn(a, b):
    return lax.dot_general(a, b, (((0,), (0,)), ((), ())), preferred_element_type=F32)


def _head_lanes(h):
    return slice(h * HEAD_DIM, (h + 1) * HEAD_DIM)


def _layer_spec(stacked, layer):
    return pl.BlockSpec((None,) + stacked.shape[1:], lambda *_: (layer, 0, 0))


def _ret_log_gamma(h):
    return float(np.log(np.float32(1.0) - np.float32(2.0) ** np.float32(-5.0 - h)))


def _ret_kernel(q_ref, k_ref, v_ref, g_ref, cos_ref, sin_ref, w_ref, o_ref,
                state_ref, decay_ref, qdec_ref, kdec_ref, qb_buf, s_buf, kdt_buf, intra_buf, incr_buf, *, nchunk):
    rb = pl.program_id(0)

    @pl.when(rb == 0)
    def _():
        state_ref[...] = jnp.zeros_like(state_ref)
        ii = lax.broadcasted_iota(jnp.int32, (CHUNK, CHUNK), 0)
        jj = lax.broadcasted_iota(jnp.int32, (CHUNK, CHUNK), 1)
        rel = (ii - jj).astype(F32)
        idx = ii.astype(F32)
        for h in range(RET_HEADS):
            lg = _ret_log_gamma(h)
            decay_ref[h] = jnp.where(rel >= 0.0, jnp.exp(lg * jnp.maximum(rel, 0.0)), 0.0)
            qdec_ref[h] = jnp.exp(lg * (idx + 1.0))
            kdec_ref[h] = jnp.exp(lg * (CHUNK - 1.0 - idx))

    for c in range(nchunk):
        rows = slice(c * CHUNK, (c + 1) * CHUNK)
        cos = cos_ref[rows, :]
        sin = sin_ref[rows, :]
        for h in range(RET_HEADS):
            ln = _head_lanes(h)
            q = q_ref[rows, ln]
            k = k_ref[rows, ln]
            qr = q * cos + pltpu.roll(q, HEAD_DIM // 2, 1) * sin
            kr = (k * cos + pltpu.roll(k, HEAD_DIM // 2, 1) * sin) * QK_SCALE
            qb = qr.astype(BF16)
            qb_buf[rows, ln] = qb
            s_buf[rows, ln] = (_dot_nt(qb, kr.astype(BF16)) * decay_ref[h]).astype(BF16)
            kdt_buf[rows, ln] = jnp.transpose((kr * kdec_ref[h]).astype(BF16))

    for c in range(nchunk):
        rows = slice(c * CHUNK, (c + 1) * CHUNK)
        for h in range(RET_HEADS):
            ln = _head_lanes(h)
            lhs = jnp.concatenate([s_buf[rows, ln], kdt_buf[rows, ln]], axis=0)
            both = _dot(lhs, v_ref[rows, ln].astype(BF16))
            intra_buf[rows, ln] = both[:CHUNK]
            incr_buf[c, h] = both[CHUNK:]

    for c in range(nchunk):
        rows = slice(c * CHUNK, (c + 1) * CHUNK)
        for h in range(RET_HEADS):
            ln = _head_lanes(h)
            st = state_ref[h]
            out = intra_buf[rows, ln] + _dot(qb_buf[rows, ln], st.astype(BF16)) * qdec_ref[h]
            state_ref[h] = st * math.exp(_ret_log_gamma(h) * CHUNK) + incr_buf[c, h]
            g = g_ref[rows, ln]
            y = _head_norm(out) * w_ref[:, ln] * (g * _sigmoid(g))
            o_ref[rows, ln] = y.astype(o_ref.dtype)


def retention_block(proj, cos_t, sin_t, norm_w, layer, tb):
    l = proj.shape[0]

    def col(j):
        return pl.BlockSpec((tb, RET_WIDTH), lambda r, j=j: (r, j))

    tab = pl.BlockSpec((tb, LANES), lambda r: (r, 0))
    hsq = pltpu.VMEM((RET_HEADS, CHUNK, CHUNK), F32)
    half = pltpu.VMEM((tb, RET_WIDTH), BF16)
    return pl.pallas_call(
        functools.partial(_ret_kernel, nchunk=tb // CHUNK),
        out_shape=jax.ShapeDtypeStruct((l, RET_WIDTH), BF16),
        grid=(l // tb,),
        in_specs=[col(0), col(1), col(2), col(3), tab, tab, _layer_spec(norm_w, layer)],
        out_specs=pl.BlockSpec((tb, RET_WIDTH), lambda r: (r, 0)),
        scratch_shapes=[hsq, hsq, hsq, hsq, half, half, half, pltpu.VMEM((tb, RET_WIDTH), F32),
                        pltpu.VMEM((tb // CHUNK, RET_HEADS, HEAD_DIM, HEAD_DIM), F32)],
        compiler_params=_cparams(("arbitrary",)),
    )(proj, proj, proj, proj, cos_t, sin_t, norm_w)


def _log_sigmoid(x):
    return -(jnp.maximum(-x, 0.0) + jnp.log1p(jnp.exp(-jnp.abs(x))))


def _mlstm_kernel(q_ref, k_ref, v_ref, og_ref, gate_ref, cw_ref, cb_ref, gb_ref, nw_ref,
                  o_ref, c_st, m_st, qbuf, kbuf, qs, ks, gs, r_buf, cmax_buf, bcum_buf, mall_buf, floor_buf,
                  p_buf, u_buf, wq_buf, s_buf, kwt_buf, *, nchunk, tb):
    rb = pl.program_id(0)
    pad = SUBLANES
    nh = MLSTM_HEADS

    @pl.when(rb == 0)
    def _():
        c_st[...] = jnp.zeros_like(c_st)
        m_st[...] = jnp.zeros_like(m_st)
        qbuf[0:pad, :] = jnp.zeros((pad, MLSTM_WIDTH), F32)
        kbuf[0:pad, :] = jnp.zeros((pad, MLSTM_WIDTH), F32)

    qbuf[pad:pad + tb, :] = q_ref[...]
    kbuf[pad:pad + tb, :] = k_ref[...]
    for r0 in range(0, tb, CONV_ROWS):
        for buf, woff, dst, scale in ((qbuf, 0, qs, None), (kbuf, MLSTM_WIDTH, ks, QK_SCALE)):
            wl = slice(woff, woff + MLSTM_WIDTH)
            acc = jnp.broadcast_to(cb_ref[:, wl], (CONV_ROWS, MLSTM_WIDTH))
            for tap in range(CONV_WIDTH):
                off = r0 + pad - (CONV_WIDTH - 1) + tap
                acc = acc + buf[off:off + CONV_ROWS, :] * cw_ref[tap:tap + 1, wl]
            act = acc * _sigmoid(acc)
            dst[r0:r0 + CONV_ROWS, :] = act if scale is None else act * scale
    qbuf[0:pad, :] = q_ref[tb - pad:tb, :]
    kbuf[0:pad, :] = k_ref[tb - pad:tb, :]

    lane_row = lax.broadcasted_iota(jnp.int32, (1, LANES), 1)
    graw = pltpu.roll(gate_ref[...] + gb_ref[...], LANES - GATE_LANE0, 1)
    gs[...] = jnp.where(lane_row < nh, graw, jnp.where(lane_row < 2 * nh, _log_sigmoid(graw), 0.0))

    ii = lax.broadcasted_iota(jnp.int32, (CHUNK, CHUNK), 0)
    jj = lax.broadcasted_iota(jnp.int32, (CHUNK, CHUNK), 1)
    causal = ii >= jj
    tril = causal.astype(F32)
    lane8 = lax.broadcasted_iota(jnp.int32, (SUBLANES, LANES), 1)
    ones_b = jnp.ones((CHUNK, HEAD_DIM), BF16)
    full = (CHUNK, CHUNK)

    def chunk_rows(c):
        return slice(c * CHUNK, (c + 1) * CHUNK)

    def aug_lanes(h):
        return slice(h * 2 * HEAD_DIM, (h + 1) * 2 * HEAD_DIM)

    for c in range(nchunk):
        rows = chunk_rows(c)
        g = gs[rows, :]
        gcum = jnp.dot(tril, g, precision=lax.Precision.HIGHEST, preferred_element_type=F32)
        bcum = pltpu.roll(gcum, LANES - nh, 1)
        r8 = jnp.transpose(g - bcum)[0:SUBLANES, :]
        cmax = r8
        for sh in (1, 2, 4, 8, 16, 32, 64):
            cmax = jnp.maximum(cmax, jnp.where(lane8 >= sh, pltpu.roll(cmax, sh, 1), -jnp.inf))
        r_buf[c] = r8
        cmax_buf[rows, :] = jnp.transpose(
            jnp.concatenate([cmax, jnp.zeros((CHUNK - SUBLANES, LANES), F32)], axis=0))
        bcum_buf[rows, :] = bcum

    m_prev = m_st[0:1, :]
    m_prevs, keep_rows = [], []
    for c in range(nchunk):
        rows = chunk_rows(c)
        m_all = jnp.maximum(cmax_buf[rows, :], m_prev)
        floor_buf[rows, :] = jnp.exp(-(bcum_buf[rows, :] + m_all))
        mall_buf[rows, :] = m_all
        m_last = m_all[CHUNK - 1:CHUNK, :]
        m_prevs.append(m_prev)
        keep_rows.append(jnp.exp(m_prev - m_last))
        m_prev = bcum_buf[c * CHUNK + CHUNK - 1:(c + 1) * CHUNK, :] + m_last
    m_st[0:1, :] = m_prev

    for c in range(nchunk):
        rows = chunk_rows(c)
        for h in range(nh):
            ln = _head_lanes(h)
            qc = qs[rows, ln]
            kb = ks[rows, ln].astype(BF16)
            m_bc = jnp.broadcast_to(mall_buf[rows, h:h + 1], full)
            dmat = jnp.exp(jnp.where(causal, r_buf[c, h:h + 1, :] - m_bc, -jnp.inf))
            s_buf[rows, ln] = (_dot_nt(qc.astype(BF16), kb) * dmat).astype(BF16)
            kwt_buf[rows, ln] = (jnp.transpose(kb).astype(F32) * dmat[CHUNK - 1:CHUNK, :]).astype(BF16)
            wq_buf[rows, ln] = (jnp.exp(m_prevs[c][:, h:h + 1] - m_bc) * qc).astype(BF16)

    for c in range(nchunk):
        rows = chunk_rows(c)
        for h in range(nh):
            ln = _head_lanes(h)
            v_aug = jnp.concatenate([v_ref[rows, ln].astype(BF16), ones_b], axis=1)
            both = _dot(jnp.concatenate([s_buf[rows, ln], kwt_buf[rows, ln]], axis=0), v_aug)
            p_buf[rows, aug_lanes(h)] = both[:CHUNK]
            u_buf[c, h] = both[CHUNK:]

    for c in range(nchunk):
        rows = chunk_rows(c)
        for h in range(nh):
            ln = _head_lanes(h)
            cs = c_st[h]
            res = p_buf[rows, aug_lanes(h)] + _dot(wq_buf[rows, ln], cs.astype(BF16))
            c_st[h] = keep_rows[c][:, h:h + 1] * cs + u_buf[c, h]
            den = res[:, HEAD_DIM + h:HEAD_DIM + h + 1]
            inv = 1.0 / jnp.maximum(jnp.abs(den), floor_buf[rows, h:h + 1])
            y = _head_norm(res[:, :HEAD_DIM] * inv) * nw_ref[:, ln] * _sigmoid(og_ref[rows, ln])
            o_ref[rows, ln] = y.astype(o_ref.dtype)


def mlstm_block(proj, tail, conv_w, conv_b, gate_bias, norm_w, layer, tb):
    l = proj.shape[0]
    base = (4 * RET_WIDTH) // MLSTM_WIDTH

    def col(j):
        return pl.BlockSpec((tb, MLSTM_WIDTH), lambda r, j=j: (r, base + j))

    nchunk = tb // CHUNK
    wide = pltpu.VMEM((tb, MLSTM_WIDTH), F32)
    hist = pltpu.VMEM((tb + SUBLANES, MLSTM_WIDTH), F32)
    narrow = pltpu.VMEM((tb, LANES), F32)
    half = pltpu.VMEM((tb, MLSTM_WIDTH), BF16)
    rows8 = pltpu.VMEM((nchunk, SUBLANES, LANES), F32)
    return pl.pallas_call(
        functools.partial(_mlstm_kernel, nchunk=nchunk, tb=tb),
        out_shape=jax.ShapeDtypeStruct((l, MLSTM_WIDTH), BF16),
        grid=(l // tb,),
        in_specs=[col(0), col(1), col(2), col(3),
                  pl.BlockSpec((tb, LANES), lambda r: (r, GATE_COLBLK)),
                  _layer_spec(conv_w, layer), _layer_spec(conv_b, layer), _layer_spec(gate_bias, layer),
                  _layer_spec(norm_w, layer)],
        out_specs=pl.BlockSpec((tb, MLSTM_WIDTH), lambda r: (r, 0)),
        scratch_shapes=[pltpu.VMEM((MLSTM_HEADS, HEAD_DIM, 2 * HEAD_DIM), F32), pltpu.VMEM((SUBLANES, LANES), F32),
                        hist, hist, wide, wide, narrow, rows8, narrow, narrow, narrow, narrow,
                        pltpu.VMEM((tb, 2 * MLSTM_WIDTH), F32),
                        pltpu.VMEM((nchunk, MLSTM_HEADS, HEAD_DIM, 2 * HEAD_DIM), F32),
                        half, half, half],
        compiler_params=_cparams(("arbitrary",)),
    )(proj, proj, proj, proj, tail, conv_w, conv_b, gate_bias, norm_w)


def _s5_prep_kernel(lre_ref, lim_ref, ldt_ref, btre_ref, btim_ref, ctre_ref, ctim_ref,
                    toep_ref, vre_ref, vim_ref, wre_ref, wim_ref, are_ref, aim_ref):
    lre = lre_ref[...]
    lim = lim_ref[...]
    dt = jnp.exp(ldt_ref[...])
    mag = jnp.exp(lre * dt)
    ang = lim * dt
    zr = mag * jnp.cos(ang) - 1.0
    zi = mag * jnp.sin(ang)
    den = lre * lre + lim * lim
    w_re = (zr * lre + zi * lim) / den
    w_im = (zi * lre - zr * lim) / den
    row_g = lax.broadcasted_iota(jnp.int32, (LANES, S5_SB), 0) // S5_GROUP
    col_g = lax.broadcasted_iota(jnp.int32, (LANES, S5_SB), 1) // S5_STATE
    same = row_g == col_g
    bt_re = btre_ref[...]
    bt_im = btim_ref[...]
    bb_re = jnp.where(same, w_re * bt_re - w_im * bt_im, 0.0)
    bb_im = jnp.where(same, w_re * bt_im + w_im * bt_re, 0.0)
    ct_re = jnp.where(same, ctre_ref[...], 0.0)
    ct_im = jnp.where(same, ctim_ref[...], 0.0)
    hp = lax.Precision.HIGHEST
    kblk = []
    for d in range(S5_T + 1):
        pm = jnp.exp(lre * dt * float(d))
        pa = lim * dt * float(d)
        p_re = pm * jnp.cos(pa)
        p_im = pm * jnp.sin(pa)
        if d < S5_T:
            ab_re = p_re * bb_re - p_im * bb_im
            ab_im = p_re * bb_im + p_im * bb_re
            srow = slice((S5_T - 1 - d) * LANES, (S5_T - d) * LANES)
            vre_ref[0, srow, :] = ab_re.astype(BF16)
            vim_ref[0, srow, :] = ab_im.astype(BF16)
            kblk.append((_dot_nt(ab_re, ct_re, hp) - _dot_nt(ab_im, ct_im, hp)).astype(BF16))
        if d >= 1:
            trow = slice((d - 1) * LANES, d * LANES)
            wre_ref[0, trow, :] = (ct_re * p_re - ct_im * p_im).astype(BF16)
            wim_ref[0, trow, :] = (-(ct_re * p_im + ct_im * p_re)).astype(BF16)
        if d == S5_T:
            are_ref[...] = p_re
            aim_ref[...] = p_im
    zero = jnp.zeros((LANES, LANES), BF16)
    for s in range(S5_T):
        for t in range(S5_T):
            toep_ref[0, s * LANES:(s + 1) * LANES, t * LANES:(t + 1) * LANES] = kblk[t - s] if t >= s else zero


def s5_operator_inputs(lam_re, lam_im, log_dt, b_re, b_im, c_re, c_im):
    depth = lam_re.shape[0]
    nst = S5_GROUPS * S5_STATE

    def tiled(m):
        return jnp.tile(m.astype(F32).reshape(depth, S5_WIDTH, S5_STATE), (1, 1, S5_GPB))

    lre = lam_re.astype(F32).reshape(depth, 1, nst)
    lim = lam_im.astype(F32).reshape(depth, 1, nst)
    ldt = jnp.repeat(log_dt.astype(F32), S5_STATE, axis=-1).reshape(depth, 1, nst)
    return (lre, lim, ldt, tiled(jnp.swapaxes(b_re, -1, -2)), tiled(jnp.swapaxes(b_im, -1, -2)),
            tiled(c_re), tiled(c_im))


def s5_operators(prep_inputs, layer):
    nst = S5_GROUPS * S5_STATE
    lane_in = pl.BlockSpec((None, 1, S5_SB), lambda g: (layer, 0, g))
    par_in = pl.BlockSpec((None, LANES, S5_SB), lambda g: (layer, g, 0))
    lane_out = pl.BlockSpec((1, S5_SB), lambda g: (0, g))

    def out_blk(rows, cols):
        return pl.BlockSpec((1, rows, cols), lambda g: (g, 0, 0))

    return pl.pallas_call(
        _s5_prep_kernel,
        out_shape=[jax.ShapeDtypeStruct((S5_GB, S5_KT, S5_KT), BF16)]
        + [jax.ShapeDtypeStruct((S5_GB, S5_KT, S5_SB), BF16)] * 4
        + [jax.ShapeDtypeStruct((1, nst), F32)] * 2,
        grid=(S5_GB,),
        in_specs=[lane_in, lane_in, lane_in, par_in, par_in, par_in, par_in],
        out_specs=[out_blk(S5_KT, S5_KT)] + [out_blk(S5_KT, S5_SB)] * 4 + [lane_out, lane_out],
        compiler_params=_cparams(("parallel",)),
    )(*prep_inputs)


def _gelu_tanh(x):
    c = math.sqrt(2.0 / math.pi)
    return 0.5 * x * (1.0 + jnp.tanh(c * (x + 0.044715 * (x * x * x))))


def _s5_kernel(u_ref, toep_ref, vre_ref, vim_ref, wre_ref, wim_ref, are_ref, aim_ref, d_ref, y_ref,
               car_re, car_im, z_re, z_im, x_re, x_im, *, tmc):
    rb = pl.program_id(1)

    @pl.when(rb == 0)
    def _():
        car_re[...] = jnp.zeros_like(car_re)
        car_im[...] = jnp.zeros_like(car_im)

    us = [u_ref[pl.ds(s, tmc, stride=S5_T), :] for s in range(S5_T)]
    ucat = jnp.concatenate([u.astype(BF16) for u in us], axis=1)
    z_re[...] = _dot(ucat, vre_ref[0])
    z_im[...] = _dot(ucat, vim_ref[0])
    a_r = are_ref[...]
    a_i = aim_ref[...]

    def step(c, carry):
        xr, xi = carry
        row = pl.ds(c, 1)
        x_re[row, :] = xr
        x_im[row, :] = xi
        return (a_r * xr - a_i * xi + z_re[row, :], a_r * xi + a_i * xr + z_im[row, :])

    xr, xi = lax.fori_loop(0, tmc, step, (car_re[0:1, :], car_im[0:1, :]))
    car_re[0:1, :] = xr
    car_im[0:1, :] = xi
    y = (_dot(ucat, toep_ref[0]) + _dot_nt(x_re[...].astype(BF16), wre_ref[0])
         + _dot_nt(x_im[...].astype(BF16), wim_ref[0]))
    for t in range(S5_T):
        yt = y[:, t * LANES:(t + 1) * LANES] + d_ref[...] * us[t]
        y_ref[pl.ds(t, tmc, stride=S5_T), :] = _gelu_tanh(yt)


def s5_block(tail, ops, d_skip, layer, tb):
    l = tail.shape[0]
    toep, v_re, v_im, w_re, w_im, a_re, a_im = ops
    tmc = tb // S5_T

    def per_gb(cols):
        return pl.BlockSpec((1, S5_KT, cols), lambda g, r: (g, 0, 0))

    lane_blk = pl.BlockSpec((1, S5_SB), lambda g, r: (0, g))
    fold = pltpu.VMEM((tmc, S5_SB), F32)
    carry = pltpu.VMEM((SUBLANES, S5_SB), F32)
    return pl.pallas_call(
        functools.partial(_s5_kernel, tmc=tmc),
        out_shape=jax.ShapeDtypeStruct((l, S5_WIDTH), F32),
        grid=(S5_GB, l // tb),
        in_specs=[pl.BlockSpec((tb, LANES), lambda g, r: (r, g)),
                  per_gb(S5_KT), per_gb(S5_SB), per_gb(S5_SB), per_gb(S5_SB), per_gb(S5_SB),
                  lane_blk, lane_blk, pl.BlockSpec((None, 1, LANES), lambda g, r: (layer, 0, g))],
        out_specs=pl.BlockSpec((tb, LANES), lambda g, r: (r, g)),
        scratch_shapes=[carry, carry, fold, fold, fold, fold],
        compiler_params=_cparams(("parallel", "arbitrary")),
    )(tail, toep, v_re, v_im, w_re, w_im, a_re, a_im, d_skip)


def _glu_kernel(y_ref, w_ref, b_ref, o_ref):
    y = y_ref[...]
    z = _dot(y.astype(BF16), w_ref[...]) + b_ref[...]
    o_ref[...] = (y * _sigmoid(z)).astype(o_ref.dtype)


def glu_block(y, w, b, layer, tm):
    l, n = y.shape
    return pl.pallas_call(
        _glu_kernel,
        out_shape=jax.ShapeDtypeStruct((l, n), BF16),
        grid=(l // tm,),
        in_specs=[pl.BlockSpec((tm, n), lambda i: (i, 0)), _layer_spec(w, layer), _layer_spec(b, layer)],
        out_specs=pl.BlockSpec((tm, n), lambda i: (i, 0)),
        compiler_params=_cparams(("parallel",)),
    )(y, w, b)


def _layer_norm(z, w, b):
    mu = jnp.mean(z, axis=-1, keepdims=True)
    d = z - mu
    var = jnp.mean(d * d, axis=-1, keepdims=True)
    return d * lax.rsqrt(var + LN_EPS) * w + b


def _outproj_kernel(a1_ref, a2_ref, a3_ref, w1_ref, w2_ref, w3_ref, x_ref, lw_ref, lb_ref, o_ref, ob_ref):
    mix = _dot(a1_ref[...], w1_ref[...]) + _dot(a2_ref[...], w2_ref[...]) + _dot(a3_ref[...], w3_ref[...])
    x1 = _layer_norm(ALPHA * x_ref[...] + mix, lw_ref[...], lb_ref[...])
    o_ref[...] = x1
    ob_ref[...] = x1.astype(BF16)


def outproj_block(y_ret, y_m, y_s, w_out, x, ln_w, ln_b, layer, tm):
    l = x.shape[0]

    def rows(n):
        return pl.BlockSpec((tm, n), lambda i: (i, 0))

    def wrows(n, blk):
        return pl.BlockSpec((None, n, D_MODEL), lambda i: (layer, blk, 0))

    return pl.pallas_call(
        _outproj_kernel,
        out_shape=[jax.ShapeDtypeStruct((l, D_MODEL), F32), jax.ShapeDtypeStruct((l, D_MODEL), BF16)],
        grid=(l // tm,),
        in_specs=[rows(RET_WIDTH), rows(MLSTM_WIDTH), rows(S5_WIDTH),
                  wrows(RET_WIDTH, 0), wrows(MLSTM_WIDTH, 1), wrows(S5_WIDTH, (RET_WIDTH + MLSTM_WIDTH) // S5_WIDTH),
                  rows(D_MODEL), _layer_spec(ln_w, layer), _layer_spec(ln_b, layer)],
        out_specs=[rows(D_MODEL), rows(D_MODEL)],
        compiler_params=_cparams(("parallel",)),
    )(y_ret, y_m, y_s, w_out, w_out, w_out, x, ln_w, ln_b)


def _down_kernel(h_ref, w_ref, x1_ref, r_ref, rb_ref):
    r = ALPHA * x1_ref[...] + _dot(h_ref[...], w_ref[...])
    r_ref[...] = r
    rb_ref[...] = r.astype(BF16)


def down_block(hid, w_down, x1, layer, tm):
    l, ff = hid.shape
    rows = pl.BlockSpec((tm, D_MODEL), lambda i: (i, 0))
    return pl.pallas_call(
        _down_kernel,
        out_shape=[jax.ShapeDtypeStruct((l, D_MODEL), F32), jax.ShapeDtypeStruct((l, D_MODEL), BF16)],
        grid=(l // tm,),
        in_specs=[pl.BlockSpec((tm, ff), lambda i: (i, 0)),
                  pl.BlockSpec((None, ff, D_MODEL), lambda i: (layer, 0, 0), pipeline_mode=pl.Buffered(1)), rows],
        out_specs=[rows, rows],
        compiler_params=_cparams(("parallel",)),
    )(hid, w_down, x1)


def _final_kernel(r_ref, rb_ref, p_ref, wg_ref, wp_ref, lw_ref, lb_ref, o_ref, ob_ref):
    gate = _sigmoid(_dot(rb_ref[...], wg_ref[...]))
    ple = gate * _dot(p_ref[...], wp_ref[...])
    out = _layer_norm(r_ref[...] + ple, lw_ref[...], lb_ref[...])
    o_ref[...] = out
    ob_ref[...] = out.astype(BF16)


def final_block(r, rb, p_b, w_gate, w_ple, ln_w, ln_b, layer, tm):
    l = r.shape[0]
    rows = pl.BlockSpec((tm, D_MODEL), lambda i: (i, 0))
    return pl.pallas_call(
        _final_kernel,
        out_shape=[jax.ShapeDtypeStruct((l, D_MODEL), F32), jax.ShapeDtypeStruct((l, D_MODEL), BF16)],
        grid=(l // tm,),
        in_specs=[rows, rows, pl.BlockSpec((None, tm, PLE_DIM), lambda i: (layer, i, 0)),
                  _layer_spec(w_gate, layer), _layer_spec(w_ple, layer), _layer_spec(ln_w, layer),
                  _layer_spec(ln_b, layer)],
        out_specs=[rows, rows],
        compiler_params=_cparams(("parallel",)),
    )(r, rb, p_b, w_gate, w_ple, ln_w, ln_b)


def _rows3(v):
    return v.astype(F32).reshape(v.shape[0], 1, -1)


def _tiles(l):
    return {"mm_m": min(1024, l), "mix": min(512, l), "s5": min(2048, l), "row": min(512, l), "down": min(256, l),
            "rope": min(2048, l)}


def kernel(x, p, positions, w_in, mlstm_conv_w, mlstm_conv_b, mlstm_i_bias, mlstm_f_bias, ret_norm_w, mlstm_norm_w, s5_lambda_re, s5_lambda_im, s5_log_dt, s5_B_re, s5_B_im, s5_C_re, s5_C_im, s5_D, s5_glu_w, s5_glu_b, w_out, ln1_w, ln1_b, w_up, w_down, w_gate, w_ple, ln2_w, ln2_b):
    bsz, l, _ = x.shape
    depth = w_in.shape[0]
    assert bsz == 1
    t = _tiles(l)
    cos_t, sin_t = rope_tables(positions.astype(F32).reshape(l, 1), t["rope"])
    xf = x.reshape(l, D_MODEL).astype(F32)
    xb = xf.astype(BF16)
    p_b = p.reshape(depth, l, PLE_DIM).astype(BF16)

    w_in_b = w_in[:, :, :MAIN_WIDTH].astype(BF16)
    w_tail_b = jnp.pad(w_in[:, :, MAIN_WIDTH:], ((0, 0), (0, 0), (0, TAIL_WIDTH - TAIL_COLS))).astype(BF16)
    w_out_b, w_up_b, w_down_b = w_out.astype(BF16), w_up.astype(BF16), w_down.astype(BF16)
    w_gate_b, w_ple_b, glu_w_b = w_gate.astype(BF16), w_ple.astype(BF16), s5_glu_w.astype(BF16)
    gate_bias = jnp.concatenate([jnp.zeros((depth, GATE_LANE0), F32), mlstm_i_bias.astype(F32),
                                 mlstm_f_bias.astype(F32)], axis=1).reshape(depth, 1, LANES)
    conv_w = mlstm_conv_w.astype(F32)
    conv_b, ret_nw, mlstm_nw = _rows3(mlstm_conv_b), _rows3(ret_norm_w), _rows3(mlstm_norm_w)
    s5_d, glu_b = _rows3(s5_D), _rows3(s5_glu_b)
    ln1w, ln1b, ln2w, ln2b = _rows3(ln1_w), _rows3(ln1_b), _rows3(ln2_w), _rows3(ln2_b)
    s5_in = s5_operator_inputs(s5_lambda_re, s5_lambda_im, s5_log_dt, s5_B_re, s5_B_im, s5_C_re, s5_C_im)

    for i in range(depth):
        proj = matmul(xb, w_in_b, i, out_dtype=F32, tm=t["mm_m"], tn=768, n=MAIN_WIDTH)
        tail = matmul(xb, w_tail_b, i, out_dtype=F32, tm=t["mm_m"], tn=TAIL_WIDTH, act="rotate_tail")
        y_ret = retention_block(proj, cos_t, sin_t, ret_nw, i, t["mix"])
        y_m = mlstm_block(proj, tail, conv_w, conv_b, gate_bias, mlstm_nw, i, t["mix"])
        y_s = s5_block(tail, s5_operators(s5_in, i), s5_d, i, t["s5"])
        y_s = glu_block(y_s, glu_w_b, glu_b, i, t["row"])
        x1, x1b = outproj_block(y_ret, y_m, y_s, w_out_b, xf, ln1w, ln1b, i, t["row"])
        hid = matmul(x1b, w_up_b, i, out_dtype=BF16, tm=t["mm_m"], tn=1024, act="relu2")
        r, rb = down_block(hid, w_down_b, x1, i, t["down"])
        xf, xb = final_block(r, rb, p_b, w_gate_b, w_ple_b, ln2w, ln2b, i, t["row"])
    return xf.reshape(bsz, l, D_MODEL)
```

```python
import functools
import math

import jax
import jax.numpy as jnp
import numpy as np
from jax import lax
from jax.experimental import pallas as pl
from jax.experimental.pallas import tpu as pltpu

F32 = jnp.float32
BF16 = jnp.bfloat16

D_MODEL = 2048
DEPTH = 2
HEAD_DIM = 128
RET_HEADS = 6
MLSTM_HEADS = 6
RET_WIDTH = RET_HEADS * HEAD_DIM
MLSTM_WIDTH = MLSTM_HEADS * HEAD_DIM
S5_WIDTH = D_MODEL - RET_WIDTH - MLSTM_WIDTH
S5_GROUP = 16
S5_GROUPS = S5_WIDTH // S5_GROUP
S5_STATE = 64
CONV_WIDTH = 4
CHUNK = 128
D_FF = 4 * D_MODEL
PLE_DIM = 256
ROPE_BASE = 10000.0
LN_EPS = 1e-5
ALPHA = (2 * DEPTH) ** 0.25
QK_SCALE = HEAD_DIM ** -0.5

LANES = 128
SUBLANES = 8
MAIN_WIDTH = 4 * RET_WIDTH + 4 * MLSTM_WIDTH
TAIL_COLS = 2 * MLSTM_HEADS + S5_WIDTH
TAIL_WIDTH = 5 * LANES
GATE_LANE0 = LANES - 2 * MLSTM_HEADS
GATE_COLBLK = S5_WIDTH // LANES
CONV_ROWS = 64
S5_T = 16
S5_GB = S5_WIDTH // LANES
S5_GPB = LANES // S5_GROUP
S5_SB = S5_GPB * S5_STATE
S5_KT = S5_T * LANES
VMEM_LIMIT = 56 * 1024 * 1024


def _cparams(sem):
    return pltpu.CompilerParams(dimension_semantics=sem, vmem_limit_bytes=VMEM_LIMIT)


def _mm_kernel(a_ref, b_ref, o_ref, *scratch, nk, act, b_cols_major):
    def finish(r):
        if act == "relu2":
            r = jnp.square(jnp.maximum(r, 0.0))
        elif act == "rotate_tail":
            r = pltpu.roll(r, TAIL_WIDTH - 2 * MLSTM_HEADS, 1)
        o_ref[...] = r.astype(o_ref.dtype)

    if nk == 1:
        finish(_dot_nt(a_ref[...], b_ref[...]) if b_cols_major else _dot(a_ref[...], b_ref[...]))
        return
    acc_ref, = scratch
    k = pl.program_id(2)

    @pl.when(k == 0)
    def _():
        acc_ref[...] = jnp.zeros_like(acc_ref)

    acc_ref[...] += _dot(a_ref[...], b_ref[...])

    @pl.when(k == nk - 1)
    def _():
        finish(acc_ref[...])


def matmul(a, b, layer, *, out_dtype, tm, tn, tk=None, act=None, b_cols_major=False):
    m, kdim = a.shape
    n = b.shape[1] if b_cols_major else b.shape[2]
    tk = kdim if tk is None else tk
    nk = kdim // tk
    assert m % tm == 0 and n % tn == 0 and kdim % tk == 0 and b.shape[2 if b_cols_major else 1] == kdim
    assert nk == 1 or not b_cols_major
    b_spec = (pl.BlockSpec((None, tn, tk), lambda i, j, k: (layer, j, k)) if b_cols_major
              else pl.BlockSpec((None, tk, tn), lambda i, j, k: (layer, k, j)))
    return pl.pallas_call(
        functools.partial(_mm_kernel, nk=nk, act=act, b_cols_major=b_cols_major),
        out_shape=jax.ShapeDtypeStruct((m, n), out_dtype),
        grid=(m // tm, n // tn, nk),
        in_specs=[pl.BlockSpec((tm, tk), lambda i, j, k: (i, k)), b_spec],
        out_specs=pl.BlockSpec((tm, tn), lambda i, j, k: (i, j)),
        scratch_shapes=[] if nk == 1 else [pltpu.VMEM((tm, tn), F32)],
        compiler_params=_cparams(("parallel", "parallel", "arbitrary")),
    )(a, b)


def _cast_cols_kernel(x_ref, o_ref, *, col0, ncols_total):
    tc = x_ref.shape[0]
    col = col0 + pl.program_id(0) * tc + lax.broadcasted_iota(jnp.int32, (tc, 1), 0)
    for d in range(x_ref.shape[1]):
        o_ref[d] = jnp.where(col < ncols_total, x_ref[:, d, :], 0.0).astype(o_ref.dtype)


def cast_cols_transposed(w_t, col0, ncols, tc):
    ctot, depth, kdim = w_t.shape
    assert col0 % tc == 0 and ncols % tc == 0
    return pl.pallas_call(
        functools.partial(_cast_cols_kernel, col0=col0, ncols_total=ctot),
        out_shape=jax.ShapeDtypeStruct((depth, ncols, kdim), BF16),
        grid=(ncols // tc,),
        in_specs=[pl.BlockSpec((tc, depth, kdim), lambda i: (col0 // tc + i, 0, 0))],
        out_specs=pl.BlockSpec((depth, tc, kdim), lambda i: (0, i, 0)),
        compiler_params=_cparams(("parallel",)),
    )(w_t)


def _rope_kernel(pos_ref, inv_ref, sign_ref, cos_ref, sin_ref):
    ang = pos_ref[...] * inv_ref[...]
    cos_ref[...] = jnp.cos(ang)
    sin_ref[...] = jnp.sin(ang) * sign_ref[...]


def rope_tables(pos_col, tl):
    l = pos_col.shape[0]
    half = np.arange(0, HEAD_DIM, 2, dtype=np.float32) / np.float32(HEAD_DIM)
    inv = (np.float32(ROPE_BASE) ** (-half)).astype(np.float32)
    inv2 = jnp.asarray(np.concatenate([inv, inv])[None, :])
    sign = jnp.asarray(np.concatenate([-np.ones(64, np.float32), np.ones(64, np.float32)])[None, :])
    row = pl.BlockSpec((1, LANES), lambda i: (0, 0))
    return pl.pallas_call(
        _rope_kernel,
        out_shape=[jax.ShapeDtypeStruct((l, LANES), F32)] * 2,
        grid=(l // tl,),
        in_specs=[pl.BlockSpec((tl, 1), lambda i: (i, 0)), row, row],
        out_specs=[pl.BlockSpec((tl, LANES), lambda i: (i, 0))] * 2,
        compiler_params=_cparams(("parallel",)),
    )(pos_col, inv2, sign)


def _head_norm(y):
    mu = jnp.mean(y, axis=-1, keepdims=True)
    d = y - mu
    var = jnp.mean(d * d, axis=-1, keepdims=True)
    return d * lax.rsqrt(var + LN_EPS)


def _mean_lanes_mxu(x):
    hi = x.astype(BF16)
    lo = (x - hi.astype(F32)).astype(BF16)
    avg = jnp.full((2 * HEAD_DIM, HEAD_DIM), 1.0 / HEAD_DIM, BF16)
    return _dot(jnp.concatenate([hi, lo], axis=1), avg)


def _head_norm_mxu(y):
    d = y - _mean_lanes_mxu(y)
    return d * lax.rsqrt(_mean_lanes_mxu(d * d) + LN_EPS)


def _sigmoid(x):
    return 1.0 / (1.0 + jnp.exp(-x))


def _dot_nt(a, b, precision=None):
    return lax.dot_general(a, b, (((1,), (1,)), ((), ())), precision=precision, preferred_element_type=F32)


def _dot(a, b):
    return jnp.dot(a, b, preferred_element_type=F32)


def _dot_tn(a, b):
    return lax.dot_general(a, b, (((0,), (0,)), ((), ())), preferred_element_type=F32)


def _head_lanes(h):
    return slice(h * HEAD_DIM, (h + 1) * HEAD_DIM)


def _layer_spec(stacked, layer):
    return pl.BlockSpec((None,) + stacked.shape[1:], lambda *_: (layer, 0, 0))


def _ret_log_gamma(h):
    return float(np.log(np.float32(1.0) - np.float32(2.0) ** np.float32(-5.0 - h)))


def _ret_kernel(q_ref, k_ref, v_ref, g_ref, cos_ref, sin_ref, w_ref, o_ref,
                state_ref, decay_ref, qdec_ref, kdec_ref, qb_buf, s_buf, kdt_buf, intra_buf, incr_buf, *, nchunk):
    rb = pl.program_id(0)

    @pl.when(rb == 0)
    def _():
        state_ref[...] = jnp.zeros_like(state_ref)
        ii = lax.broadcasted_iota(jnp.int32, (CHUNK, CHUNK), 0)
        jj = lax.broadcasted_iota(jnp.int32, (CHUNK, CHUNK), 1)
        rel = (ii - jj).astype(F32)
        idx = ii.astype(F32)
        for h in range(RET_HEADS):
            lg = _ret_log_gamma(h)
            decay_ref[h] = jnp.where(rel >= 0.0, jnp.exp(lg * jnp.maximum(rel, 0.0)), 0.0)
            qdec_ref[h] = jnp.exp(lg * (idx + 1.0))
            kdec_ref[h] = jnp.exp(lg * (CHUNK - 1.0 - idx))

    for c in range(nchunk):
        rows = slice(c * CHUNK, (c + 1) * CHUNK)
        cos = cos_ref[rows, :]
        sin = sin_ref[rows, :]
        for h in range(RET_HEADS):
            ln = _head_lanes(h)
            q = q_ref[rows, ln]
            k = k_ref[rows, ln]
            qr = q * cos + pltpu.roll(q, HEAD_DIM // 2, 1) * sin
            kr = (k * cos + pltpu.roll(k, HEAD_DIM // 2, 1) * sin) * QK_SCALE
            qb = qr.astype(BF16)
            qb_buf[rows, ln] = qb
            s_buf[rows, ln] = (_dot_nt(qb, kr.astype(BF16)) * decay_ref[h]).astype(BF16)
            kdt_buf[rows, ln] = jnp.transpose((kr * kdec_ref[h]).astype(BF16))

    for c in range(nchunk):
        rows = slice(c * CHUNK, (c + 1) * CHUNK)
        for h in range(RET_HEADS):
            ln = _head_lanes(h)
            lhs = jnp.concatenate([s_buf[rows, ln], kdt_buf[rows, ln]], axis=0)
            both = _dot(lhs, v_ref[rows, ln].astype(BF16))
            intra_buf[rows, ln] = both[:CHUNK]
            incr_buf[c, h] = both[CHUNK:]

    for c in range(nchunk):
        rows = slice(c * CHUNK, (c + 1) * CHUNK)
        for h in range(RET_HEADS):
            ln = _head_lanes(h)
            st = state_ref[h]
            out = intra_buf[rows, ln] + _dot(qb_buf[rows, ln], st.astype(BF16)) * qdec_ref[h]
            state_ref[h] = st * math.exp(_ret_log_gamma(h) * CHUNK) + incr_buf[c, h]
            g = g_ref[rows, ln]
            y = _head_norm(out) * w_ref[:, ln] * (g * _sigmoid(g))
            o_ref[rows, ln] = y.astype(o_ref.dtype)


def retention_block(proj, cos_t, sin_t, norm_w, layer, tb):
    l = proj.shape[0]

    def col(j):
        return pl.BlockSpec((tb, RET_WIDTH), lambda r, j=j: (r, j))

    tab = pl.BlockSpec((tb, LANES), lambda r: (r, 0))
    hsq = pltpu.VMEM((RET_HEADS, CHUNK, CHUNK), F32)
    half = pltpu.VMEM((tb, RET_WIDTH), BF16)
    return pl.pallas_call(
        functools.partial(_ret_kernel, nchunk=tb // CHUNK),
        out_shape=jax.ShapeDtypeStruct((l, RET_WIDTH), BF16),
        grid=(l // tb,),
        in_specs=[col(0), col(1), col(2), col(3), tab, tab, _layer_spec(norm_w, layer)],
        out_specs=pl.BlockSpec((tb, RET_WIDTH), lambda r: (r, 0)),
        scratch_shapes=[hsq, hsq, hsq, hsq, half, half, half, pltpu.VMEM((tb, RET_WIDTH), F32),
                        pltpu.VMEM((tb // CHUNK, RET_HEADS, HEAD_DIM, HEAD_DIM), F32)],
        compiler_params=_cparams(("arbitrary",)),
    )(proj, proj, proj, proj, cos_t, sin_t, norm_w)


def _log_sigmoid(x):
    return -(jnp.maximum(-x, 0.0) + jnp.log1p(jnp.exp(-jnp.abs(x))))


def _mlstm_kernel(q_ref, k_ref, v_ref, og_ref, gate_ref, cw_ref, cb_ref, gb_ref, nw_ref,
                  o_ref, c_st, m_st, qbuf, kbuf, qs, ks, gs, r_buf, cmax_buf, bcum_buf, mall_buf, floor_buf,
                  p_buf, u_buf, wq_buf, s_buf, kwt_buf, *, nchunk, tb):
    rb = pl.program_id(0)
    pad = SUBLANES
    nh = MLSTM_HEADS

    @pl.when(rb == 0)
    def _():
        c_st[...] = jnp.zeros_like(c_st)
        m_st[...] = jnp.zeros_like(m_st)
        qbuf[0:pad, :] = jnp.zeros((pad, MLSTM_WIDTH), F32)
        kbuf[0:pad, :] = jnp.zeros((pad, MLSTM_WIDTH), F32)

    qbuf[pad:pad + tb, :] = q_ref[...]
    kbuf[pad:pad + tb, :] = k_ref[...]
    for r0 in range(0, tb, CONV_ROWS):
        for buf, woff, dst, scale in ((qbuf, 0, qs, None), (kbuf, MLSTM_WIDTH, ks, QK_SCALE)):
            wl = slice(woff, woff + MLSTM_WIDTH)
            acc = jnp.broadcast_to(cb_ref[:, wl], (CONV_ROWS, MLSTM_WIDTH))
            for tap in range(CONV_WIDTH):
                off = r0 + pad - (CONV_WIDTH - 1) + tap
                acc = acc + buf[off:off + CONV_ROWS, :] * cw_ref[tap:tap + 1, wl]
            act = acc * _sigmoid(acc)
            dst[r0:r0 + CONV_ROWS, :] = act if scale is None else act * scale
    qbuf[0:pad, :] = q_ref[tb - pad:tb, :]
    kbuf[0:pad, :] = k_ref[tb - pad:tb, :]

    lane_row = lax.broadcasted_iota(jnp.int32, (1, LANES), 1)
    graw = pltpu.roll(gate_ref[...] + gb_ref[...], LANES - GATE_LANE0, 1)
    gs[...] = jnp.where(lane_row < nh, graw, jnp.where(lane_row < 2 * nh, _log_sigmoid(graw), 0.0))

    ii = lax.broadcasted_iota(jnp.int32, (CHUNK, CHUNK), 0)
    jj = lax.broadcasted_iota(jnp.int32, (CHUNK, CHUNK), 1)
    causal = ii >= jj
    tril = causal.astype(F32)
    lane8 = lax.broadcasted_iota(jnp.int32, (SUBLANES, LANES), 1)
    ones_b = jnp.ones((CHUNK, HEAD_DIM), BF16)
    full = (CHUNK, CHUNK)

    def chunk_rows(c):
        return slice(c * CHUNK, (c + 1) * CHUNK)

    def aug_lanes(h):
        return slice(h * 2 * HEAD_DIM, (h + 1) * 2 * HEAD_DIM)

    for c in range(nchunk):
        rows = chunk_rows(c)
        g = gs[rows, :]
        gcum = jnp.dot(tril, g, precision=lax.Precision.HIGHEST, preferred_element_type=F32)
        bcum = pltpu.roll(gcum, LANES - nh, 1)
        r8 = jnp.transpose(g - bcum)[0:SUBLANES, :]
        cmax = r8
        for sh in (1, 2, 4, 8, 16, 32, 64):
            cmax = jnp.maximum(cmax, jnp.where(lane8 >= sh, pltpu.roll(cmax, sh, 1), -jnp.inf))
        r_buf[c] = r8
        cmax_buf[rows, :] = jnp.transpose(
            jnp.concatenate([cmax, jnp.zeros((CHUNK - SUBLANES, LANES), F32)], axis=0))
        bcum_buf[rows, :] = bcum

    m_prev = m_st[0:1, :]
    m_prevs, keep_rows = [], []
    for c in range(nchunk):
        rows = chunk_rows(c)
        m_all = jnp.maximum(cmax_buf[rows, :], m_prev)
        floor_buf[rows, :] = jnp.exp(-(bcum_buf[rows, :] + m_all))
        mall_buf[rows, :] = m_all
        m_last = m_all[CHUNK - 1:CHUNK, :]
        m_prevs.append(m_prev)
        keep_rows.append(jnp.exp(m_prev - m_last))
        m_prev = bcum_buf[c * CHUNK + CHUNK - 1:(c + 1) * CHUNK, :] + m_last
    m_st[0:1, :] = m_prev

    for c in range(nchunk):
        rows = chunk_rows(c)
        for h in range(nh):
            ln = _head_lanes(h)
            qc = qs[rows, ln]
            kb = ks[rows, ln].astype(BF16)
            m_bc = jnp.broadcast_to(mall_buf[rows, h:h + 1], full)
            dmat = jnp.exp(jnp.where(causal, r_buf[c, h:h + 1, :] - m_bc, -jnp.inf))
            s_buf[rows, ln] = (_dot_nt(qc.astype(BF16), kb) * dmat).astype(BF16)
            kwt_buf[rows, ln] = (jnp.transpose(kb).astype(F32) * dmat[CHUNK - 1:CHUNK, :]).astype(BF16)
            wq_buf[rows, ln] = (jnp.exp(m_prevs[c][:, h:h + 1] - m_bc) * qc).astype(BF16)

    for c in range(nchunk):
        rows = chunk_rows(c)
        for h in range(nh):
            ln = _head_lanes(h)
            v_aug = jnp.concatenate([v_ref[rows, ln].astype(BF16), ones_b], axis=1)
            both = _dot(jnp.concatenate([s_buf[rows, ln], kwt_buf[rows, ln]], axis=0), v_aug)
            p_buf[rows, aug_lanes(h)] = both[:CHUNK]
            u_buf[c, h] = both[CHUNK:]

    for c in range(nchunk):
        rows = chunk_rows(c)
        for h in range(nh):
            ln = _head_lanes(h)
            cs = c_st[h]
            res = p_buf[rows, aug_lanes(h)] + _dot(wq_buf[rows, ln], cs.astype(BF16))
            c_st[h] = keep_rows[c][:, h:h + 1] * cs + u_buf[c, h]
            den = res[:, HEAD_DIM + h:HEAD_DIM + h + 1]
            inv = 1.0 / jnp.maximum(jnp.abs(den), floor_buf[rows, h:h + 1])
            y = _head_norm(res[:, :HEAD_DIM] * inv) * nw_ref[:, ln] * _sigmoid(og_ref[rows, ln])
            o_ref[rows, ln] = y.astype(o_ref.dtype)


def mlstm_block(proj, tail, conv_w, conv_b, gate_bias, norm_w, layer, tb):
    l = proj.shape[0]
    base = (4 * RET_WIDTH) // MLSTM_WIDTH

    def col(j):
        return pl.BlockSpec((tb, MLSTM_WIDTH), lambda r, j=j: (r, base + j))

    nchunk = tb // CHUNK
    wide = pltpu.VMEM((tb, MLSTM_WIDTH), F32)
    hist = pltpu.VMEM((tb + SUBLANES, MLSTM_WIDTH), F32)
    narrow = pltpu.VMEM((tb, LANES), F32)
    half = pltpu.VMEM((tb, MLSTM_WIDTH), BF16)
    rows8 = pltpu.VMEM((nchunk, SUBLANES, LANES), F32)
    return pl.pallas_call(
        functools.partial(_mlstm_kernel, nchunk=nchunk, tb=tb),
        out_shape=jax.ShapeDtypeStruct((l, MLSTM_WIDTH), BF16),
        grid=(l // tb,),
        in_specs=[col(0), col(1), col(2), col(3),
                  pl.BlockSpec((tb, LANES), lambda r: (r, GATE_COLBLK)),
                  _layer_spec(conv_w, layer), _layer_spec(conv_b, layer), _layer_spec(gate_bias, layer),
                  _layer_spec(norm_w, layer)],
        out_specs=pl.BlockSpec((tb, MLSTM_WIDTH), lambda r: (r, 0)),
        scratch_shapes=[pltpu.VMEM((MLSTM_HEADS, HEAD_DIM, 2 * HEAD_DIM), F32), pltpu.VMEM((SUBLANES, LANES), F32),
                        hist, hist, wide, wide, narrow, rows8, narrow, narrow, narrow, narrow,
                        pltpu.VMEM((tb, 2 * MLSTM_WIDTH), F32),
                        pltpu.VMEM((nchunk, MLSTM_HEADS, HEAD_DIM, 2 * HEAD_DIM), F32),
                        half, half, half],
        compiler_params=_cparams(("arbitrary",)),
    )(proj, proj, proj, proj, tail, conv_w, conv_b, gate_bias, norm_w)


def _s5_prep_kernel(lre_ref, lim_ref, ldt_ref, btre_ref, btim_ref, ctre_ref, ctim_ref,
                    toep_ref, vre_ref, vim_ref, wre_ref, wim_ref, are_ref, aim_ref):
    lre = lre_ref[...]
    lim = lim_ref[...]
    dt = jnp.exp(ldt_ref[...])
    mag = jnp.exp(lre * dt)
    ang = lim * dt
    zr = mag * jnp.cos(ang) - 1.0
    zi = mag * jnp.sin(ang)
    den = lre * lre + lim * lim
    w_re = (zr * lre + zi * lim) / den
    w_im = (zi * lre - zr * lim) / den
    row_g = lax.broadcasted_iota(jnp.int32, (LANES, S5_SB), 0) // S5_GROUP
    col_g = lax.broadcasted_iota(jnp.int32, (LANES, S5_SB), 1) // S5_STATE
    same = row_g == col_g
    bt_re = btre_ref[...]
    bt_im = btim_ref[...]
    bb_re = jnp.where(same, w_re * bt_re - w_im * bt_im, 0.0)
    bb_im = jnp.where(same, w_re * bt_im + w_im * bt_re, 0.0)
    ct_re = jnp.where(same, ctre_ref[...], 0.0)
    ct_im = jnp.where(same, ctim_ref[...], 0.0)
    hp = lax.Precision.HIGHEST
    kblk = []
    for d in range(S5_T + 1):
        pm = jnp.exp(lre * dt * float(d))
        pa = lim * dt * float(d)
        p_re = pm * jnp.cos(pa)
        p_im = pm * jnp.sin(pa)
        if d < S5_T:
            ab_re = p_re * bb_re - p_im * bb_im
            ab_im = p_re * bb_im + p_im * bb_re
            srow = slice((S5_T - 1 - d) * LANES, (S5_T - d) * LANES)
            vre_ref[0, srow, :] = ab_re.astype(BF16)
            vim_ref[0, srow, :] = ab_im.astype(BF16)
            kblk.append((_dot_nt(ab_re, ct_re, hp) - _dot_nt(ab_im, ct_im, hp)).astype(BF16))
        if d >= 1:
            trow = slice((d - 1) * LANES, d * LANES)
            wre_ref[0, trow, :] = (ct_re * p_re - ct_im * p_im).astype(BF16)
            wim_ref[0, trow, :] = (-(ct_re * p_im + ct_im * p_re)).astype(BF16)
        if d == S5_T:
            are_ref[...] = p_re
            aim_ref[...] = p_im
    zero = jnp.zeros((LANES, LANES), BF16)
    for s in range(S5_T):
        for t in range(S5_T):
            toep_ref[0, s * LANES:(s + 1) * LANES, t * LANES:(t + 1) * LANES] = kblk[t - s] if t >= s else zero


def s5_operator_inputs(lam_re, lam_im, log_dt, b_re, b_im, c_re, c_im):
    depth = lam_re.shape[0]
    nst = S5_GROUPS * S5_STATE

    def tiled(m):
        return jnp.tile(m.astype(F32).reshape(depth, S5_WIDTH, S5_STATE), (1, 1, S5_GPB))

    lre = lam_re.astype(F32).reshape(depth, 1, nst)
    lim = lam_im.astype(F32).reshape(depth, 1, nst)
    ldt = jnp.repeat(log_dt.astype(F32), S5_STATE, axis=-1).reshape(depth, 1, nst)
    return (lre, lim, ldt, tiled(jnp.swapaxes(b_re, -1, -2)), tiled(jnp.swapaxes(b_im, -1, -2)),
            tiled(c_re), tiled(c_im))


def s5_operators(prep_inputs, layer):
    nst = S5_GROUPS * S5_STATE
    lane_in = pl.BlockSpec((None, 1, S5_SB), lambda g: (layer, 0, g))
    par_in = pl.BlockSpec((None, LANES, S5_SB), lambda g: (layer, g, 0))
    lane_out = pl.BlockSpec((1, S5_SB), lambda g: (0, g))

    def out_blk(rows, cols):
        return pl.BlockSpec((1, rows, cols), lambda g: (g, 0, 0))

    return pl.pallas_call(
        _s5_prep_kernel,
        out_shape=[jax.ShapeDtypeStruct((S5_GB, S5_KT, S5_KT), BF16)]
        + [jax.ShapeDtypeStruct((S5_GB, S5_KT, S5_SB), BF16)] * 4
        + [jax.ShapeDtypeStruct((1, nst), F32)] * 2,
        grid=(S5_GB,),
        in_specs=[lane_in, lane_in, lane_in, par_in, par_in, par_in, par_in],
        out_specs=[out_blk(S5_KT, S5_KT)] + [out_blk(S5_KT, S5_SB)] * 4 + [lane_out, lane_out],
        compiler_params=_cparams(("parallel",)),
    )(*prep_inputs)


def _gelu_tanh(x):
    c = math.sqrt(2.0 / math.pi)
    return 0.5 * x * (1.0 + jnp.tanh(c * (x + 0.044715 * (x * x * x))))


def _s5_kernel(u_ref, toep_ref, vre_ref, vim_ref, wre_ref, wim_ref, are_ref, aim_ref, d_ref, y_ref,
               car_re, car_im, z_re, z_im, x_re, x_im, *, tmc):
    rb = pl.program_id(1)

    @pl.when(rb == 0)
    def _():
        car_re[...] = jnp.zeros_like(car_re)
        car_im[...] = jnp.zeros_like(car_im)

    us = [u_ref[pl.ds(s, tmc, stride=S5_T), :] for s in range(S5_T)]
    ucat = jnp.concatenate([u.astype(BF16) for u in us], axis=1)
    z_re[...] = _dot(ucat, vre_ref[0])
    z_im[...] = _dot(ucat, vim_ref[0])
    a_r = are_ref[...]
    a_i = aim_ref[...]

    def step(c, carry):
        xr, xi = carry
        row = pl.ds(c, 1)
        x_re[row, :] = xr
        x_im[row, :] = xi
        return (a_r * xr - a_i * xi + z_re[row, :], a_r * xi + a_i * xr + z_im[row, :])

    xr, xi = lax.fori_loop(0, tmc, step, (car_re[0:1, :], car_im[0:1, :]))
    car_re[0:1, :] = xr
    car_im[0:1, :] = xi
    y = (_dot(ucat, toep_ref[0]) + _dot_nt(x_re[...].astype(BF16), wre_ref[0])
         + _dot_nt(x_im[...].astype(BF16), wim_ref[0]))
    for t in range(S5_T):
        yt = y[:, t * LANES:(t + 1) * LANES] + d_ref[...] * us[t]
        y_ref[pl.ds(t, tmc, stride=S5_T), :] = _gelu_tanh(yt)


def s5_block(tail, ops, d_skip, layer, tb):
    l = tail.shape[0]
    toep, v_re, v_im, w_re, w_im, a_re, a_im = ops
    tmc = tb // S5_T

    def per_gb(cols):
        return pl.BlockSpec((1, S5_KT, cols), lambda g, r: (g, 0, 0))

    lane_blk = pl.BlockSpec((1, S5_SB), lambda g, r: (0, g))
    fold = pltpu.VMEM((tmc, S5_SB), F32)
    carry = pltpu.VMEM((SUBLANES, S5_SB), F32)
    return pl.pallas_call(
        functools.partial(_s5_kernel, tmc=tmc),
        out_shape=jax.ShapeDtypeStruct((l, S5_WIDTH), F32),
        grid=(S5_GB, l // tb),
        in_specs=[pl.BlockSpec((tb, LANES), lambda g, r: (r, g)),
                  per_gb(S5_KT), per_gb(S5_SB), per_gb(S5_SB), per_gb(S5_SB), per_gb(S5_SB),
                  lane_blk, lane_blk, pl.BlockSpec((None, 1, LANES), lambda g, r: (layer, 0, g))],
        out_specs=pl.BlockSpec((tb, LANES), lambda g, r: (r, g)),
        scratch_shapes=[carry, carry, fold, fold, fold, fold],
        compiler_params=_cparams(("parallel", "arbitrary")),
    )(tail, toep, v_re, v_im, w_re, w_im, a_re, a_im, d_skip)


def _glu_kernel(y_ref, w_ref, b_ref, o_ref):
    y = y_ref[...]
    z = _dot(y.astype(BF16), w_ref[...]) + b_ref[...]
    o_ref[...] = (y * _sigmoid(z)).astype(o_ref.dtype)


def glu_block(y, w, b, layer, tm):
    l, n = y.shape
    return pl.pallas_call(
        _glu_kernel,
        out_shape=jax.ShapeDtypeStruct((l, n), BF16),
        grid=(l // tm,),
        in_specs=[pl.BlockSpec((tm, n), lambda i: (i, 0)), _layer_spec(w, layer), _layer_spec(b, layer)],
        out_specs=pl.BlockSpec((tm, n), lambda i: (i, 0)),
        compiler_params=_cparams(("parallel",)),
    )(y, w, b)


def _layer_norm(z, w, b):
    mu = jnp.mean(z, axis=-1, keepdims=True)
    d = z - mu
    var = jnp.mean(d * d, axis=-1, keepdims=True)
    return d * lax.rsqrt(var + LN_EPS) * w + b


def _skewed(step, matmul_into, epilogue_from, acc_a, acc_b):
    @pl.when(step == 0)
    def _():
        acc_b[...] = jnp.zeros_like(acc_b)

    @pl.when(step % 2 == 0)
    def _():
        matmul_into(acc_a)
        epilogue_from(acc_b)

    @pl.when(step % 2 == 1)
    def _():
        matmul_into(acc_b)
        epilogue_from(acc_a)


def _skew_specs(tm, nsteps):
    def lead(n):
        return pl.BlockSpec((tm, n), lambda i: (jnp.minimum(i, nsteps - 1), 0))

    def lag(n):
        return pl.BlockSpec((tm, n), lambda i: (jnp.maximum(i - 1, 0), 0))

    return lead, lag


def _resident(stacked, layer, rows=None, blk=0):
    rows = stacked.shape[1] if rows is None else rows
    return pl.BlockSpec((None, rows, stacked.shape[2]), lambda i: (layer, blk, 0), pipeline_mode=pl.Buffered(1))


def _outproj_kernel(a1_ref, a2_ref, a3_ref, w1_ref, w2_ref, w3_ref, x_ref, lw_ref, lb_ref, o_ref, ob_ref,
                    acc_a, acc_b):
    def matmul_into(acc):
        acc[...] = (_dot(a1_ref[...], w1_ref[...]) + _dot(a2_ref[...], w2_ref[...])
                    + _dot(a3_ref[...], w3_ref[...]))

    def epilogue_from(acc):
        x1 = _layer_norm(ALPHA * x_ref[...] + acc[...], lw_ref[...], lb_ref[...])
        o_ref[...] = x1
        ob_ref[...] = x1.astype(BF16)

    _skewed(pl.program_id(0), matmul_into, epilogue_from, acc_a, acc_b)


def outproj_block(y_ret, y_m, y_s, w_out, x, ln_w, ln_b, layer, tm):
    l = x.shape[0]
    nsteps = l // tm
    lead, lag = _skew_specs(tm, nsteps)
    acc = pltpu.VMEM((tm, D_MODEL), F32)
    return pl.pallas_call(
        _outproj_kernel,
        out_shape=[jax.ShapeDtypeStruct((l, D_MODEL), F32), jax.ShapeDtypeStruct((l, D_MODEL), BF16)],
        grid=(nsteps + 1,),
        in_specs=[lead(RET_WIDTH), lead(MLSTM_WIDTH), lead(S5_WIDTH),
                  _resident(w_out, layer, RET_WIDTH, 0), _resident(w_out, layer, MLSTM_WIDTH, 1),
                  _resident(w_out, layer, S5_WIDTH, (RET_WIDTH + MLSTM_WIDTH) // S5_WIDTH),
                  lag(D_MODEL), _layer_spec(ln_w, layer), _layer_spec(ln_b, layer)],
        out_specs=[lag(D_MODEL), lag(D_MODEL)],
        scratch_shapes=[acc, acc],
        compiler_params=_cparams(("arbitrary",)),
    )(y_ret, y_m, y_s, w_out, w_out, w_out, x, ln_w, ln_b)


def _down_kernel(h_ref, w_ref, x1_ref, r_ref, rb_ref):
    r = ALPHA * x1_ref[...] + _dot(h_ref[...], w_ref[...])
    r_ref[...] = r
    rb_ref[...] = r.astype(BF16)


def down_block(hid, w_down, x1, layer, tm):
    l, ff = hid.shape
    rows = pl.BlockSpec((tm, D_MODEL), lambda i: (i, 0))
    return pl.pallas_call(
        _down_kernel,
        out_shape=[jax.ShapeDtypeStruct((l, D_MODEL), F32), jax.ShapeDtypeStruct((l, D_MODEL), BF16)],
        grid=(l // tm,),
        in_specs=[pl.BlockSpec((tm, ff), lambda i: (i, 0)),
                  pl.BlockSpec((None, ff, D_MODEL), lambda i: (layer, 0, 0), pipeline_mode=pl.Buffered(1)), rows],
        out_specs=[rows, rows],
        compiler_params=_cparams(("parallel",)),
    )(hid, w_down, x1)


def _final_kernel(r_ref, rb_ref, p_ref, wg_ref, wp_ref, lw_ref, lb_ref, o_ref, ob_ref, acc_a, acc_b):
    def matmul_into(acc):
        acc[...] = _sigmoid(_dot(rb_ref[...], wg_ref[...])) * _dot(p_ref[...], wp_ref[...])

    def epilogue_from(acc):
        out = _layer_norm(r_ref[...] + acc[...], lw_ref[...], lb_ref[...])
        o_ref[...] = out
        ob_ref[...] = out.astype(BF16)

    _skewed(pl.program_id(0), matmul_into, epilogue_from, acc_a, acc_b)


def final_block(r, rb, p_b, w_gate, w_ple, ln_w, ln_b, layer, tm):
    l = r.shape[0]
    nsteps = l // tm
    lead, lag = _skew_specs(tm, nsteps)
    acc = pltpu.VMEM((tm, D_MODEL), F32)
    return pl.pallas_call(
        _final_kernel,
        out_shape=[jax.ShapeDtypeStruct((l, D_MODEL), F32), jax.ShapeDtypeStruct((l, D_MODEL), BF16)],
        grid=(nsteps + 1,),
        in_specs=[lag(D_MODEL), lead(D_MODEL),
                  pl.BlockSpec((None, tm, PLE_DIM), lambda i: (layer, jnp.minimum(i, nsteps - 1), 0)),
                  _resident(w_gate, layer), _resident(w_ple, layer), _layer_spec(ln_w, layer),
                  _layer_spec(ln_b, layer)],
        out_specs=[lag(D_MODEL), lag(D_MODEL)],
        scratch_shapes=[acc, acc],
        compiler_params=_cparams(("arbitrary",)),
    )(r, rb, p_b, w_gate, w_ple, ln_w, ln_b)


def _rows3(v):
    return v.astype(F32).reshape(v.shape[0], 1, -1)


def _tiles(l):
    return {"mm_m": min(1024, l), "mix": min(512, l), "s5": min(2048, l), "row": min(512, l), "down": min(256, l),
            "rope": min(2048, l)}


def kernel(x, p, positions, w_in, mlstm_conv_w, mlstm_conv_b, mlstm_i_bias, mlstm_f_bias, ret_norm_w, mlstm_norm_w, s5_lambda_re, s5_lambda_im, s5_log_dt, s5_B_re, s5_B_im, s5_C_re, s5_C_im, s5_D, s5_glu_w, s5_glu_b, w_out, ln1_w, ln1_b, w_up, w_down, w_gate, w_ple, ln2_w, ln2_b):
    bsz, l, _ = x.shape
    depth = w_in.shape[0]
    assert bsz == 1
    t = _tiles(l)
    cos_t, sin_t = rope_tables(positions.astype(F32).reshape(l, 1), t["rope"])
    xf = x.reshape(l, D_MODEL).astype(F32)
    xb = xf.astype(BF16)
    p_b = p.reshape(depth, l, PLE_DIM).astype(BF16)

    w_in_t = jnp.transpose(w_in, (2, 0, 1))
    w_in_b = cast_cols_transposed(w_in_t, 0, MAIN_WIDTH, LANES)
    w_tail_b = cast_cols_transposed(w_in_t, MAIN_WIDTH, TAIL_WIDTH, LANES)
    w_out_b, w_up_b, w_down_b = w_out.astype(BF16), w_up.astype(BF16), w_down.astype(BF16)
    w_gate_b, w_ple_b, glu_w_b = w_gate.astype(BF16), w_ple.astype(BF16), s5_glu_w.astype(BF16)
    gate_bias = jnp.concatenate([jnp.zeros((depth, GATE_LANE0), F32), mlstm_i_bias.astype(F32),
                                 mlstm_f_bias.astype(F32)], axis=1).reshape(depth, 1, LANES)
    conv_w = mlstm_conv_w.astype(F32)
    conv_b, ret_nw, mlstm_nw = _rows3(mlstm_conv_b), _rows3(ret_norm_w), _rows3(mlstm_norm_w)
    s5_d, glu_b = _rows3(s5_D), _rows3(s5_glu_b)
    ln1w, ln1b, ln2w, ln2b = _rows3(ln1_w), _rows3(ln1_b), _rows3(ln2_w), _rows3(ln2_b)
    s5_in = s5_operator_inputs(s5_lambda_re, s5_lambda_im, s5_log_dt, s5_B_re, s5_B_im, s5_C_re, s5_C_im)

    for i in range(depth):
        proj = matmul(xb, w_in_b, i, out_dtype=F32, tm=t["mm_m"], tn=768, b_cols_major=True)
        tail = matmul(xb, w_tail_b, i, out_dtype=F32, tm=t["mm_m"], tn=TAIL_WIDTH, act="rotate_tail",
                      b_cols_major=True)
        y_ret = retention_block(proj, cos_t, sin_t, ret_nw, i, t["mix"])
        y_m = mlstm_block(proj, tail, conv_w, conv_b, gate_bias, mlstm_nw, i, t["mix"])
        y_s = s5_block(tail, s5_operators(s5_in, i), s5_d, i, t["s5"])
        y_s = glu_block(y_s, glu_w_b, glu_b, i, t["row"])
        x1, x1b = outproj_block(y_ret, y_m, y_s, w_out_b, xf, ln1w, ln1b, i, t["row"])
        hid = matmul(x1b, w_up_b, i, out_dtype=BF16, tm=t["mm_m"], tn=1024, act="relu2")
        r, rb = down_block(hid, w_down_b, x1, i, t["down"])
        xf, xb = final_block(r, rb, p_b, w_gate_b, w_ple_b, ln2w, ln2b, i, t["row"])
    return xf.reshape(bsz, l, D_MODEL)
```

```python
import functools
import math

import jax
import jax.numpy as jnp
import numpy as np
from jax import lax
from jax.experimental import pallas as pl
from jax.experimental.pallas import tpu as pltpu

F32 = jnp.float32
BF16 = jnp.bfloat16

D_MODEL = 2048
DEPTH = 2
HEAD_DIM = 128
RET_HEADS = 6
MLSTM_HEADS = 6
RET_WIDTH = RET_HEADS * HEAD_DIM
MLSTM_WIDTH = MLSTM_HEADS * HEAD_DIM
S5_WIDTH = D_MODEL - RET_WIDTH - MLSTM_WIDTH
S5_GROUP = 16
S5_GROUPS = S5_WIDTH // S5_GROUP
S5_STATE = 64
CONV_WIDTH = 4
CHUNK = 128
D_FF = 4 * D_MODEL
PLE_DIM = 256
ROPE_BASE = 10000.0
LN_EPS = 1e-5
ALPHA = (2 * DEPTH) ** 0.25
QK_SCALE = HEAD_DIM ** -0.5

LANES = 128
SUBLANES = 8
MAIN_WIDTH = 4 * RET_WIDTH + 4 * MLSTM_WIDTH
TAIL_COLS = 2 * MLSTM_HEADS + S5_WIDTH
TAIL_WIDTH = 5 * LANES
GATE_LANE0 = LANES - 2 * MLSTM_HEADS
GATE_COLBLK = S5_WIDTH // LANES
CONV_ROWS = 64
S5_T = 16
S5_GB = S5_WIDTH // LANES
S5_GPB = LANES // S5_GROUP
S5_SB = S5_GPB * S5_STATE
S5_KT = S5_T * LANES
VMEM_LIMIT = 56 * 1024 * 1024


def _cparams(sem):
    return pltpu.CompilerParams(dimension_semantics=sem, vmem_limit_bytes=VMEM_LIMIT)


def _mm_kernel(a_ref, b_ref, o_ref, *scratch, nk, act, b_cols_major):
    def finish(r):
        if act == "relu2":
            r = jnp.square(jnp.maximum(r, 0.0))
        elif act == "rotate_tail":
            r = pltpu.roll(r, TAIL_WIDTH - 2 * MLSTM_HEADS, 1)
        o_ref[...] = r.astype(o_ref.dtype)

    if nk == 1:
        finish(_dot_nt(a_ref[...], b_ref[...]) if b_cols_major else _dot(a_ref[...], b_ref[...]))
        return
    acc_ref, = scratch
    k = pl.program_id(2)

    @pl.when(k == 0)
    def _():
        acc_ref[...] = jnp.zeros_like(acc_ref)

    acc_ref[...] += _dot(a_ref[...], b_ref[...])

    @pl.when(k == nk - 1)
    def _():
        finish(acc_ref[...])


def matmul(a, b, layer, *, out_dtype, tm, tn, tk=None, act=None, b_cols_major=False):
    m, kdim = a.shape
    n = b.shape[1] if b_cols_major else b.shape[2]
    tk = kdim if tk is None else tk
    nk = kdim // tk
    assert m % tm == 0 and n % tn == 0 and kdim % tk == 0 and b.shape[2 if b_cols_major else 1] == kdim
    assert nk == 1 or not b_cols_major
    b_spec = (pl.BlockSpec((None, tn, tk), lambda i, j, k: (layer, j, k)) if b_cols_major
              else pl.BlockSpec((None, tk, tn), lambda i, j, k: (layer, k, j)))
    return pl.pallas_call(
        functools.partial(_mm_kernel, nk=nk, act=act, b_cols_major=b_cols_major),
        out_shape=jax.ShapeDtypeStruct((m, n), out_dtype),
        grid=(m // tm, n // tn, nk),
        in_specs=[pl.BlockSpec((tm, tk), lambda i, j, k: (i, k)), b_spec],
        out_specs=pl.BlockSpec((tm, tn), lambda i, j, k: (i, j)),
        scratch_shapes=[] if nk == 1 else [pltpu.VMEM((tm, tn), F32)],
        compiler_params=_cparams(("parallel", "parallel", "arbitrary")),
    )(a, b)


def _cast_cols_kernel(x_ref, o_ref, *, col0, ncols_total):
    tc = x_ref.shape[0]
    col = col0 + pl.program_id(0) * tc + lax.broadcasted_iota(jnp.int32, (tc, 1), 0)
    for d in range(x_ref.shape[1]):
        o_ref[d] = jnp.where(col < ncols_total, x_ref[:, d, :], 0.0).astype(o_ref.dtype)


def cast_cols_transposed(w_t, col0, ncols, tc):
    ctot, depth, kdim = w_t.shape
    assert col0 % tc == 0 and ncols % tc == 0
    return pl.pallas_call(
        functools.partial(_cast_cols_kernel, col0=col0, ncols_total=ctot),
        out_shape=jax.ShapeDtypeStruct((depth, ncols, kdim), BF16),
        grid=(ncols // tc,),
        in_specs=[pl.BlockSpec((tc, depth, kdim), lambda i: (col0 // tc + i, 0, 0))],
        out_specs=pl.BlockSpec((depth, tc, kdim), lambda i: (0, i, 0)),
        compiler_params=_cparams(("parallel",)),
    )(w_t)


def _rope_kernel(pos_ref, inv_ref, sign_ref, cos_ref, sin_ref):
    ang = pos_ref[...] * inv_ref[...]
    cos_ref[...] = jnp.cos(ang)
    sin_ref[...] = jnp.sin(ang) * sign_ref[...]


def rope_tables(pos_col, tl):
    l = pos_col.shape[0]
    half = np.arange(0, HEAD_DIM, 2, dtype=np.float32) / np.float32(HEAD_DIM)
    inv = (np.float32(ROPE_BASE) ** (-half)).astype(np.float32)
    inv2 = jnp.asarray(np.concatenate([inv, inv])[None, :])
    sign = jnp.asarray(np.concatenate([-np.ones(64, np.float32), np.ones(64, np.float32)])[None, :])
    row = pl.BlockSpec((1, LANES), lambda i: (0, 0))
    return pl.pallas_call(
        _rope_kernel,
        out_shape=[jax.ShapeDtypeStruct((l, LANES), F32)] * 2,
        grid=(l // tl,),
        in_specs=[pl.BlockSpec((tl, 1), lambda i: (i, 0)), row, row],
        out_specs=[pl.BlockSpec((tl, LANES), lambda i: (i, 0))] * 2,
        compiler_params=_cparams(("parallel",)),
    )(pos_col, inv2, sign)


def _head_norm(y):
    mu = jnp.mean(y, axis=-1, keepdims=True)
    d = y - mu
    var = jnp.mean(d * d, axis=-1, keepdims=True)
    return d * lax.rsqrt(var + LN_EPS)


def _mean_lanes_mxu(x):
    hi = x.astype(BF16)
    lo = (x - hi.astype(F32)).astype(BF16)
    avg = jnp.full((2 * HEAD_DIM, HEAD_DIM), 1.0 / HEAD_DIM, BF16)
    return _dot(jnp.concatenate([hi, lo], axis=1), avg)


def _head_norm_mxu(y):
    d = y - _mean_lanes_mxu(y)
    return d * lax.rsqrt(_mean_lanes_mxu(d * d) + LN_EPS)


def _sigmoid(x):
    return 1.0 / (1.0 + jnp.exp(-x))


def _dot_nt(a, b, precision=None):
    return lax.dot_general(a, b, (((1,), (1,)), ((), ())), precision=precision, preferred_element_type=F32)


def _dot(a, b):
    return jnp.dot(a, b, preferred_element_type=F32)


def _dot_tn(a, b):
    return lax.dot_general(a, b, (((0,), (0,)), ((), ())), preferred_element_type=F32)


def _head_lanes(h):
    return slice(h * HEAD_DIM, (h + 1) * HEAD_DIM)


def _layer_spec(stacked, layer):
    return pl.BlockSpec((None,) + stacked.shape[1:], lambda *_: (layer, 0, 0))


def _ret_log_gamma(h):
    return float(np.log(np.float32(1.0) - np.float32(2.0) ** np.float32(-5.0 - h)))


def _ret_kernel(q_ref, k_ref, v_ref, g_ref, cos_ref, sin_ref, w_ref, o_ref,
                state_ref, decay_ref, qdec_ref, kdec_ref, qb_buf, s_buf, kdt_buf, intra_buf, incr_buf, *, nchunk):
    rb = pl.program_id(0)

    @pl.when(rb == 0)
    def _():
        state_ref[...] = jnp.zeros_like(state_ref)
        ii = lax.broadcasted_iota(jnp.int32, (CHUNK, CHUNK), 0)
        jj = lax.broadcasted_iota(jnp.int32, (CHUNK, CHUNK), 1)
        rel = (ii - jj).astype(F32)
        idx = ii.astype(F32)
        for h in range(RET_HEADS):
            lg = _ret_log_gamma(h)
            decay_ref[h] = jnp.where(rel >= 0.0, jnp.exp(lg * jnp.maximum(rel, 0.0)), 0.0)
            qdec_ref[h] = jnp.exp(lg * (idx + 1.0))
            kdec_ref[h] = jnp.exp(lg * (CHUNK - 1.0 - idx))

    for c in range(nchunk):
        rows = slice(c * CHUNK, (c + 1) * CHUNK)
        cos = cos_ref[rows, :]
        sin = sin_ref[rows, :]
        for h in range(RET_HEADS):
            ln = _head_lanes(h)
            q = q_ref[rows, ln]
            k = k_ref[rows, ln]
            qr = q * cos + pltpu.roll(q, HEAD_DIM // 2, 1) * sin
            kr = (k * cos + pltpu.roll(k, HEAD_DIM // 2, 1) * sin) * QK_SCALE
            qb = qr.astype(BF16)
            qb_buf[rows, ln] = qb
            s_buf[rows, ln] = (_dot_nt(qb, kr.astype(BF16)) * decay_ref[h]).astype(BF16)
            kdt_buf[rows, ln] = jnp.transpose((kr * kdec_ref[h]).astype(BF16))

    for c in range(nchunk):
        rows = slice(c * CHUNK, (c + 1) * CHUNK)
        for h in range(RET_HEADS):
            ln = _head_lanes(h)
            lhs = jnp.concatenate([s_buf[rows, ln], kdt_buf[rows, ln]], axis=0)
            both = _dot(lhs, v_ref[rows, ln].astype(BF16))
            intra_buf[rows, ln] = both[:CHUNK]
            incr_buf[c, h] = both[CHUNK:]

    for c in range(nchunk):
        rows = slice(c * CHUNK, (c + 1) * CHUNK)
        for h in range(RET_HEADS):
            ln = _head_lanes(h)
            st = state_ref[h]
            out = intra_buf[rows, ln] + _dot(qb_buf[rows, ln], st.astype(BF16)) * qdec_ref[h]
            state_ref[h] = st * math.exp(_ret_log_gamma(h) * CHUNK) + incr_buf[c, h]
            g = g_ref[rows, ln]
            y = _head_norm(out) * w_ref[:, ln] * (g * _sigmoid(g))
            o_ref[rows, ln] = y.astype(o_ref.dtype)


def retention_block(proj, cos_t, sin_t, norm_w, layer, tb):
    l = proj.shape[0]

    def col(j):
        return pl.BlockSpec((tb, RET_WIDTH), lambda r, j=j: (r, j))

    tab = pl.BlockSpec((tb, LANES), lambda r: (r, 0))
    hsq = pltpu.VMEM((RET_HEADS, CHUNK, CHUNK), F32)
    half = pltpu.VMEM((tb, RET_WIDTH), BF16)
    return pl.pallas_call(
        functools.partial(_ret_kernel, nchunk=tb // CHUNK),
        out_shape=jax.ShapeDtypeStruct((l, RET_WIDTH), BF16),
        grid=(l // tb,),
        in_specs=[col(0), col(1), col(2), col(3), tab, tab, _layer_spec(norm_w, layer)],
        out_specs=pl.BlockSpec((tb, RET_WIDTH), lambda r: (r, 0)),
        scratch_shapes=[hsq, hsq, hsq, hsq, half, half, half, pltpu.VMEM((tb, RET_WIDTH), F32),
                        pltpu.VMEM((tb // CHUNK, RET_HEADS, HEAD_DIM, HEAD_DIM), F32)],
        compiler_params=_cparams(("arbitrary",)),
    )(proj, proj, proj, proj, cos_t, sin_t, norm_w)


def _log_sigmoid(x):
    return -(jnp.maximum(-x, 0.0) + jnp.log1p(jnp.exp(-jnp.abs(x))))


def _mlstm_kernel(q_ref, k_ref, v_ref, og_ref, gate_ref, cw_ref, cb_ref, gb_ref, nw_ref,
                  o_ref, c_st, m_st, qbuf, kbuf, qs, ks, gs, r_buf, cmax_buf, bcum_buf, mall_buf, floor_buf,
                  p_buf, u_buf, wq_buf, s_buf, kwt_buf, *, nchunk, tb):
    rb = pl.program_id(0)
    pad = SUBLANES
    nh = MLSTM_HEADS

    @pl.when(rb == 0)
    def _():
        c_st[...] = jnp.zeros_like(c_st)
        m_st[...] = jnp.zeros_like(m_st)
        qbuf[0:pad, :] = jnp.zeros((pad, MLSTM_WIDTH), F32)
        kbuf[0:pad, :] = jnp.zeros((pad, MLSTM_WIDTH), F32)

    qbuf[pad:pad + tb, :] = q_ref[...]
    kbuf[pad:pad + tb, :] = k_ref[...]
    for r0 in range(0, tb, CONV_ROWS):
        for buf, woff, dst, scale in ((qbuf, 0, qs, None), (kbuf, MLSTM_WIDTH, ks, QK_SCALE)):
            wl = slice(woff, woff + MLSTM_WIDTH)
            acc = jnp.broadcast_to(cb_ref[:, wl], (CONV_ROWS, MLSTM_WIDTH))
            for tap in range(CONV_WIDTH):
                off = r0 + pad - (CONV_WIDTH - 1) + tap
                acc = acc + buf[off:off + CONV_ROWS, :] * cw_ref[tap:tap + 1, wl]
            act = acc * _sigmoid(acc)
            dst[r0:r0 + CONV_ROWS, :] = act if scale is None else act * scale
    qbuf[0:pad, :] = q_ref[tb - pad:tb, :]
    kbuf[0:pad, :] = k_ref[tb - pad:tb, :]

    lane_row = lax.broadcasted_iota(jnp.int32, (1, LANES), 1)
    graw = pltpu.roll(gate_ref[...] + gb_ref[...], LANES - GATE_LANE0, 1)
    gs[...] = jnp.where(lane_row < nh, graw, jnp.where(lane_row < 2 * nh, _log_sigmoid(graw), 0.0))

    ii = lax.broadcasted_iota(jnp.int32, (CHUNK, CHUNK), 0)
    jj = lax.broadcasted_iota(jnp.int32, (CHUNK, CHUNK), 1)
    causal = ii >= jj
    tril = causal.astype(F32)
    lane8 = lax.broadcasted_iota(jnp.int32, (SUBLANES, LANES), 1)
    ones_b = jnp.ones((CHUNK, HEAD_DIM), BF16)
    full = (CHUNK, CHUNK)

    def chunk_rows(c):
        return slice(c * CHUNK, (c + 1) * CHUNK)

    def aug_lanes(h):
        return slice(h * 2 * HEAD_DIM, (h + 1) * 2 * HEAD_DIM)

    for c in range(nchunk):
        rows = chunk_rows(c)
        g = gs[rows, :]
        gcum = jnp.dot(tril, g, precision=lax.Precision.HIGHEST, preferred_element_type=F32)
        bcum = pltpu.roll(gcum, LANES - nh, 1)
        r8 = jnp.transpose(g - bcum)[0:SUBLANES, :]
        cmax = r8
        for sh in (1, 2, 4, 8, 16, 32, 64):
            cmax = jnp.maximum(cmax, jnp.where(lane8 >= sh, pltpu.roll(cmax, sh, 1), -jnp.inf))
        r_buf[c] = r8
        cmax_buf[rows, :] = jnp.transpose(
            jnp.concatenate([cmax, jnp.zeros((CHUNK - SUBLANES, LANES), F32)], axis=0))
        bcum_buf[rows, :] = bcum

    m_prev = m_st[0:1, :]
    m_prevs, keep_rows = [], []
    for c in range(nchunk):
        rows = chunk_rows(c)
        m_all = jnp.maximum(cmax_buf[rows, :], m_prev)
        floor_buf[rows, :] = jnp.exp(-(bcum_buf[rows, :] + m_all))
        mall_buf[rows, :] = m_all
        m_last = m_all[CHUNK - 1:CHUNK, :]
        m_prevs.append(m_prev)
        keep_rows.append(jnp.exp(m_prev - m_last))
        m_prev = bcum_buf[c * CHUNK + CHUNK - 1:(c + 1) * CHUNK, :] + m_last
    m_st[0:1, :] = m_prev

    for c in range(nchunk):
        rows = chunk_rows(c)
        for h in range(nh):
            ln = _head_lanes(h)
            qc = qs[rows, ln]
            kb = ks[rows, ln].astype(BF16)
            m_bc = jnp.broadcast_to(mall_buf[rows, h:h + 1], full)
            dmat = jnp.exp(jnp.where(causal, r_buf[c, h:h + 1, :] - m_bc, -jnp.inf))
            s_buf[rows, ln] = (_dot_nt(qc.astype(BF16), kb) * dmat).astype(BF16)
            kwt_buf[rows, ln] = (jnp.transpose(kb).astype(F32) * dmat[CHUNK - 1:CHUNK, :]).astype(BF16)
            wq_buf[rows, ln] = (jnp.exp(m_prevs[c][:, h:h + 1] - m_bc) * qc).astype(BF16)

    for c in range(nchunk):
        rows = chunk_rows(c)
        for h in range(nh):
            ln = _head_lanes(h)
            v_aug = jnp.concatenate([v_ref[rows, ln].astype(BF16), ones_b], axis=1)
            both = _dot(jnp.concatenate([s_buf[rows, ln], kwt_buf[rows, ln]], axis=0), v_aug)
            p_buf[rows, aug_lanes(h)] = both[:CHUNK]
            u_buf[c, h] = both[CHUNK:]

    for c in range(nchunk):
        rows = chunk_rows(c)
        for h in range(nh):
            ln = _head_lanes(h)
            cs = c_st[h]
            res = p_buf[rows, aug_lanes(h)] + _dot(wq_buf[rows, ln], cs.astype(BF16))
            c_st[h] = keep_rows[c][:, h:h + 1] * cs + u_buf[c, h]
            den = res[:, HEAD_DIM + h:HEAD_DIM + h + 1]
            inv = 1.0 / jnp.maximum(jnp.abs(den), floor_buf[rows, h:h + 1])
            y = _head_norm(res[:, :HEAD_DIM] * inv) * nw_ref[:, ln] * _sigmoid(og_ref[rows, ln])
            o_ref[rows, ln] = y.astype(o_ref.dtype)


def mlstm_block(proj, tail, conv_w, conv_b, gate_bias, norm_w, layer, tb):
    l = proj.shape[0]
    base = (4 * RET_WIDTH) // MLSTM_WIDTH

    def col(j):
        return pl.BlockSpec((tb, MLSTM_WIDTH), lambda r, j=j: (r, base + j))

    nchunk = tb // CHUNK
    wide = pltpu.VMEM((tb, MLSTM_WIDTH), F32)
    hist = pltpu.VMEM((tb + SUBLANES, MLSTM_WIDTH), F32)
    narrow = pltpu.VMEM((tb, LANES), F32)
    half = pltpu.VMEM((tb, MLSTM_WIDTH), BF16)
    rows8 = pltpu.VMEM((nchunk, SUBLANES, LANES), F32)
    return pl.pallas_call(
        functools.partial(_mlstm_kernel, nchunk=nchunk, tb=tb),
        out_shape=jax.ShapeDtypeStruct((l, MLSTM_WIDTH), BF16),
        grid=(l // tb,),
        in_specs=[col(0), col(1), col(2), col(3),
                  pl.BlockSpec((tb, LANES), lambda r: (r, GATE_COLBLK)),
                  _layer_spec(conv_w, layer), _layer_spec(conv_b, layer), _layer_spec(gate_bias, layer),
                  _layer_spec(norm_w, layer)],
        out_specs=pl.BlockSpec((tb, MLSTM_WIDTH), lambda r: (r, 0)),
        scratch_shapes=[pltpu.VMEM((MLSTM_HEADS, HEAD_DIM, 2 * HEAD_DIM), F32), pltpu.VMEM((SUBLANES, LANES), F32),
                        hist, hist, wide, wide, narrow, rows8, narrow, narrow, narrow, narrow,
                        pltpu.VMEM((tb, 2 * MLSTM_WIDTH), F32),
                        pltpu.VMEM((nchunk, MLSTM_HEADS, HEAD_DIM, 2 * HEAD_DIM), F32),
                        half, half, half],
        compiler_params=_cparams(("arbitrary",)),
    )(proj, proj, proj, proj, tail, conv_w, conv_b, gate_bias, norm_w)


def _s5_prep_kernel(lre_ref, lim_ref, ldt_ref, btre_ref, btim_ref, ctre_ref, ctim_ref,
                    toep_ref, vre_ref, vim_ref, wre_ref, wim_ref, apre_ref, apim_ref):
    lre = lre_ref[...]
    lim = lim_ref[...]
    dt = jnp.exp(ldt_ref[...])
    mag = jnp.exp(lre * dt)
    ang = lim * dt
    zr = mag * jnp.cos(ang) - 1.0
    zi = mag * jnp.sin(ang)
    den = lre * lre + lim * lim
    w_re = (zr * lre + zi * lim) / den
    w_im = (zi * lre - zr * lim) / den
    row_g = lax.broadcasted_iota(jnp.int32, (LANES, S5_SB), 0) // S5_GROUP
    col_g = lax.broadcasted_iota(jnp.int32, (LANES, S5_SB), 1) // S5_STATE
    same = row_g == col_g
    bt_re = btre_ref[...]
    bt_im = btim_ref[...]
    bb_re = jnp.where(same, w_re * bt_re - w_im * bt_im, 0.0)
    bb_im = jnp.where(same, w_re * bt_im + w_im * bt_re, 0.0)
    ct_re = jnp.where(same, ctre_ref[...], 0.0)
    ct_im = jnp.where(same, ctim_ref[...], 0.0)
    hp = lax.Precision.HIGHEST
    kblk = []
    for d in range(S5_T + 1):
        pm = jnp.exp(lre * dt * float(d))
        pa = lim * dt * float(d)
        p_re = pm * jnp.cos(pa)
        p_im = pm * jnp.sin(pa)
        if d < S5_T:
            ab_re = p_re * bb_re - p_im * bb_im
            ab_im = p_re * bb_im + p_im * bb_re
            srow = slice((S5_T - 1 - d) * LANES, (S5_T - d) * LANES)
            vre_ref[0, srow, :] = ab_re.astype(BF16)
            vim_ref[0, srow, :] = ab_im.astype(BF16)
            kblk.append((_dot_nt(ab_re, ct_re, hp) - _dot_nt(ab_im, ct_im, hp)).astype(BF16))
        if d >= 1:
            trow = slice((d - 1) * LANES, d * LANES)
            wre_ref[0, trow, :] = (ct_re * p_re - ct_im * p_im).astype(BF16)
            wim_ref[0, trow, :] = (-(ct_re * p_im + ct_im * p_re)).astype(BF16)
    apre_ref[...] = jnp.zeros_like(apre_ref)
    apim_ref[...] = jnp.zeros_like(apim_ref)
    for k in range(SUBLANES + 1):
        pm = jnp.exp(lre * dt * float(S5_T * k))
        pa = lim * dt * float(S5_T * k)
        apre_ref[k:k + 1, :] = pm * jnp.cos(pa)
        apim_ref[k:k + 1, :] = pm * jnp.sin(pa)
    zero = jnp.zeros((LANES, LANES), BF16)
    for s in range(S5_T):
        for t in range(S5_T):
            toep_ref[0, s * LANES:(s + 1) * LANES, t * LANES:(t + 1) * LANES] = kblk[t - s] if t >= s else zero


def s5_operator_inputs(lam_re, lam_im, log_dt, b_re, b_im, c_re, c_im):
    depth = lam_re.shape[0]
    nst = S5_GROUPS * S5_STATE

    def tiled(m):
        return jnp.tile(m.astype(F32).reshape(depth, S5_WIDTH, S5_STATE), (1, 1, S5_GPB))

    lre = lam_re.astype(F32).reshape(depth, 1, nst)
    lim = lam_im.astype(F32).reshape(depth, 1, nst)
    ldt = jnp.repeat(log_dt.astype(F32), S5_STATE, axis=-1).reshape(depth, 1, nst)
    return (lre, lim, ldt, tiled(jnp.swapaxes(b_re, -1, -2)), tiled(jnp.swapaxes(b_im, -1, -2)),
            tiled(c_re), tiled(c_im))


def s5_operators(prep_inputs, layer):
    nst = S5_GROUPS * S5_STATE
    lane_in = pl.BlockSpec((None, 1, S5_SB), lambda g: (layer, 0, g))
    par_in = pl.BlockSpec((None, LANES, S5_SB), lambda g: (layer, g, 0))
    pow_out = pl.BlockSpec((2 * SUBLANES, S5_SB), lambda g: (0, g))

    def out_blk(rows, cols):
        return pl.BlockSpec((1, rows, cols), lambda g: (g, 0, 0))

    return pl.pallas_call(
        _s5_prep_kernel,
        out_shape=[jax.ShapeDtypeStruct((S5_GB, S5_KT, S5_KT), BF16)]
        + [jax.ShapeDtypeStruct((S5_GB, S5_KT, S5_SB), BF16)] * 4
        + [jax.ShapeDtypeStruct((2 * SUBLANES, nst), F32)] * 2,
        grid=(S5_GB,),
        in_specs=[lane_in, lane_in, lane_in, par_in, par_in, par_in, par_in],
        out_specs=[out_blk(S5_KT, S5_KT)] + [out_blk(S5_KT, S5_SB)] * 4 + [pow_out, pow_out],
        compiler_params=_cparams(("parallel",)),
    )(*prep_inputs)


def _gelu_tanh(x):
    c = math.sqrt(2.0 / math.pi)
    return 0.5 * x * (1.0 + jnp.tanh(c * (x + 0.044715 * (x * x * x))))


def _s5_kernel(u_ref, toep_ref, vre_ref, vim_ref, wre_ref, wim_ref, apre_ref, apim_ref, d_ref, y_ref,
               car_re, car_im, x_re, x_im, *, tmc):
    rb = pl.program_id(1)

    @pl.when(rb == 0)
    def _():
        car_re[...] = jnp.zeros_like(car_re)
        car_im[...] = jnp.zeros_like(car_im)

    us = [u_ref[pl.ds(s, tmc, stride=S5_T), :] for s in range(S5_T)]
    ucat = jnp.concatenate([u.astype(BF16) for u in us], axis=1)
    y_re = _dot(ucat, vre_ref[0])
    y_im = _dot(ucat, vim_ref[0])

    row_in_tile = lax.broadcasted_iota(jnp.int32, (tmc, 1), 0) & (SUBLANES - 1)

    def shifted(v, s):
        return jnp.where(row_in_tile >= s, pltpu.roll(v, s, 0), 0.0)

    for s in (1, 2, 4):
        a_r = apre_ref[s:s + 1, :]
        a_i = apim_ref[s:s + 1, :]
        s_re, s_im = shifted(y_re, s), shifted(y_im, s)
        y_re, y_im = y_re + a_r * s_re - a_i * s_im, y_im + a_r * s_im + a_i * s_re
    e_re, e_im = shifted(y_re, 1), shifted(y_im, 1)
    p_re = apre_ref[0:SUBLANES, :]
    p_im = apim_ref[0:SUBLANES, :]
    a8_r = apre_ref[SUBLANES:SUBLANES + 1, :]
    a8_i = apim_ref[SUBLANES:SUBLANES + 1, :]
    xr = car_re[0:1, :]
    xi = car_im[0:1, :]
    for t in range(tmc // SUBLANES):
        rows = slice(t * SUBLANES, (t + 1) * SUBLANES)
        x_re[rows, :] = e_re[rows] + p_re * xr - p_im * xi
        x_im[rows, :] = e_im[rows] + p_re * xi + p_im * xr
        last = (t + 1) * SUBLANES - 1
        xr, xi = (a8_r * xr - a8_i * xi + y_re[last:last + 1], a8_r * xi + a8_i * xr + y_im[last:last + 1])
    car_re[0:1, :] = xr
    car_im[0:1, :] = xi
    y = (_dot(ucat, toep_ref[0]) + _dot_nt(x_re[...].astype(BF16), wre_ref[0])
         + _dot_nt(x_im[...].astype(BF16), wim_ref[0]))
    for t in range(S5_T):
        yt = y[:, t * LANES:(t + 1) * LANES] + d_ref[...] * us[t]
        y_ref[pl.ds(t, tmc, stride=S5_T), :] = _gelu_tanh(yt)


def s5_block(tail, ops, d_skip, layer, tb):
    l = tail.shape[0]
    toep, v_re, v_im, w_re, w_im, ap_re, ap_im = ops
    tmc = tb // S5_T

    def per_gb(cols):
        return pl.BlockSpec((1, S5_KT, cols), lambda g, r: (g, 0, 0))

    pow_blk = pl.BlockSpec((2 * SUBLANES, S5_SB), lambda g, r: (0, g))
    fold = pltpu.VMEM((tmc, S5_SB), F32)
    carry = pltpu.VMEM((SUBLANES, S5_SB), F32)
    return pl.pallas_call(
        functools.partial(_s5_kernel, tmc=tmc),
        out_shape=jax.ShapeDtypeStruct((l, S5_WIDTH), F32),
        grid=(S5_GB, l // tb),
        in_specs=[pl.BlockSpec((tb, LANES), lambda g, r: (r, g)),
                  per_gb(S5_KT), per_gb(S5_SB), per_gb(S5_SB), per_gb(S5_SB), per_gb(S5_SB),
                  pow_blk, pow_blk, pl.BlockSpec((None, 1, LANES), lambda g, r: (layer, 0, g))],
        out_specs=pl.BlockSpec((tb, LANES), lambda g, r: (r, g)),
        scratch_shapes=[carry, carry, fold, fold],
        compiler_params=_cparams(("parallel", "arbitrary")),
    )(tail, toep, v_re, v_im, w_re, w_im, ap_re, ap_im, d_skip)


def _glu_kernel(y_ref, w_ref, b_ref, o_ref):
    y = y_ref[...]
    z = _dot(y.astype(BF16), w_ref[...]) + b_ref[...]
    o_ref[...] = (y * _sigmoid(z)).astype(o_ref.dtype)


def glu_block(y, w, b, layer, tm):
    l, n = y.shape
    return pl.pallas_call(
        _glu_kernel,
        out_shape=jax.ShapeDtypeStruct((l, n), BF16),
        grid=(l // tm,),
        in_specs=[pl.BlockSpec((tm, n), lambda i: (i, 0)), _layer_spec(w, layer), _layer_spec(b, layer)],
        out_specs=pl.BlockSpec((tm, n), lambda i: (i, 0)),
        compiler_params=_cparams(("parallel",)),
    )(y, w, b)


def _layer_norm(z, w, b):
    mu = jnp.mean(z, axis=-1, keepdims=True)
    d = z - mu
    var = jnp.mean(d * d, axis=-1, keepdims=True)
    return d * lax.rsqrt(var + LN_EPS) * w + b


def _outproj_kernel(a1_ref, a2_ref, a3_ref, w1_ref, w2_ref, w3_ref, x_ref, lw_ref, lb_ref, o_ref, ob_ref):
    mix = _dot(a1_ref[...], w1_ref[...]) + _dot(a2_ref[...], w2_ref[...]) + _dot(a3_ref[...], w3_ref[...])
    x1 = _layer_norm(ALPHA * x_ref[...] + mix, lw_ref[...], lb_ref[...])
    o_ref[...] = x1
    ob_ref[...] = x1.astype(BF16)


def outproj_block(y_ret, y_m, y_s, w_out, x, ln_w, ln_b, layer, tm):
    l = x.shape[0]

    def rows(n):
        return pl.BlockSpec((tm, n), lambda i: (i, 0))

    def wrows(n, blk):
        return pl.BlockSpec((None, n, D_MODEL), lambda i: (layer, blk, 0))

    return pl.pallas_call(
        _outproj_kernel,
        out_shape=[jax.ShapeDtypeStruct((l, D_MODEL), F32), jax.ShapeDtypeStruct((l, D_MODEL), BF16)],
        grid=(l // tm,),
        in_specs=[rows(RET_WIDTH), rows(MLSTM_WIDTH), rows(S5_WIDTH),
                  wrows(RET_WIDTH, 0), wrows(MLSTM_WIDTH, 1), wrows(S5_WIDTH, (RET_WIDTH + MLSTM_WIDTH) // S5_WIDTH),
                  rows(D_MODEL), _layer_spec(ln_w, layer), _layer_spec(ln_b, layer)],
        out_specs=[rows(D_MODEL), rows(D_MODEL)],
        compiler_params=_cparams(("parallel",)),
    )(y_ret, y_m, y_s, w_out, w_out, w_out, x, ln_w, ln_b)


def _down_kernel(h_ref, w_ref, x1_ref, r_ref, rb_ref):
    r = ALPHA * x1_ref[...] + _dot(h_ref[...], w_ref[...])
    r_ref[...] = r
    rb_ref[...] = r.astype(BF16)


def down_block(hid, w_down, x1, layer, tm):
    l, ff = hid.shape
    rows = pl.BlockSpec((tm, D_MODEL), lambda i: (i, 0))
    return pl.pallas_call(
        _down_kernel,
        out_shape=[jax.ShapeDtypeStruct((l, D_MODEL), F32), jax.ShapeDtypeStruct((l, D_MODEL), BF16)],
        grid=(l // tm,),
        in_specs=[pl.BlockSpec((tm, ff), lambda i: (i, 0)),
                  pl.BlockSpec((None, ff, D_MODEL), lambda i: (layer, 0, 0), pipeline_mode=pl.Buffered(1)), rows],
        out_specs=[rows, rows],
        compiler_params=_cparams(("parallel",)),
    )(hid, w_down, x1)


def _final_kernel(r_ref, rb_ref, p_ref, wg_ref, wp_ref, lw_ref, lb_ref, o_ref, ob_ref):
    gate = _sigmoid(_dot(rb_ref[...], wg_ref[...]))
    ple = gate * _dot(p_ref[...], wp_ref[...])
    out = _layer_norm(r_ref[...] + ple, lw_ref[...], lb_ref[...])
    o_ref[...] = out
    ob_ref[...] = out.astype(BF16)


def final_block(r, rb, p_b, w_gate, w_ple, ln_w, ln_b, layer, tm):
    l = r.shape[0]
    rows = pl.BlockSpec((tm, D_MODEL), lambda i: (i, 0))
    return pl.pallas_call(
        _final_kernel,
        out_shape=[jax.ShapeDtypeStruct((l, D_MODEL), F32), jax.ShapeDtypeStruct((l, D_MODEL), BF16)],
        grid=(l // tm,),
        in_specs=[rows, rows, pl.BlockSpec((None, tm, PLE_DIM), lambda i: (layer, i, 0)),
                  _layer_spec(w_gate, layer), _layer_spec(w_ple, layer), _layer_spec(ln_w, layer),
                  _layer_spec(ln_b, layer)],
        out_specs=[rows, rows],
        compiler_params=_cparams(("parallel",)),
    )(r, rb, p_b, w_gate, w_ple, ln_w, ln_b)


def _rows3(v):
    return v.astype(F32).reshape(v.shape[0], 1, -1)


def _tiles(l):
    return {"mm_m": min(1024, l), "mix": min(512, l), "s5": min(2048, l), "row": min(512, l), "down": min(256, l),
            "rope": min(2048, l)}


def kernel(x, p, positions, w_in, mlstm_conv_w, mlstm_conv_b, mlstm_i_bias, mlstm_f_bias, ret_norm_w, mlstm_norm_w, s5_lambda_re, s5_lambda_im, s5_log_dt, s5_B_re, s5_B_im, s5_C_re, s5_C_im, s5_D, s5_glu_w, s5_glu_b, w_out, ln1_w, ln1_b, w_up, w_down, w_gate, w_ple, ln2_w, ln2_b):
    bsz, l, _ = x.shape
    depth = w_in.shape[0]
    assert bsz == 1
    t = _tiles(l)
    cos_t, sin_t = rope_tables(positions.astype(F32).reshape(l, 1), t["rope"])
    xf = x.reshape(l, D_MODEL).astype(F32)
    xb = xf.astype(BF16)
    p_b = p.reshape(depth, l, PLE_DIM).astype(BF16)

    w_in_t = jnp.transpose(w_in, (2, 0, 1))
    w_in_b = cast_cols_transposed(w_in_t, 0, MAIN_WIDTH, LANES)
    w_tail_b = cast_cols_transposed(w_in_t, MAIN_WIDTH, TAIL_WIDTH, LANES)
    w_out_b, w_up_b, w_down_b = w_out.astype(BF16), w_up.astype(BF16), w_down.astype(BF16)
    w_gate_b, w_ple_b, glu_w_b = w_gate.astype(BF16), w_ple.astype(BF16), s5_glu_w.astype(BF16)
    gate_bias = jnp.concatenate([jnp.zeros((depth, GATE_LANE0), F32), mlstm_i_bias.astype(F32),
                                 mlstm_f_bias.astype(F32)], axis=1).reshape(depth, 1, LANES)
    conv_w = mlstm_conv_w.astype(F32)
    conv_b, ret_nw, mlstm_nw = _rows3(mlstm_conv_b), _rows3(ret_norm_w), _rows3(mlstm_norm_w)
    s5_d, glu_b = _rows3(s5_D), _rows3(s5_glu_b)
    ln1w, ln1b, ln2w, ln2b = _rows3(ln1_w), _rows3(ln1_b), _rows3(ln2_w), _rows3(ln2_b)
    s5_in = s5_operator_inputs(s5_lambda_re, s5_lambda_im, s5_log_dt, s5_B_re, s5_B_im, s5_C_re, s5_C_im)

    for i in range(depth):
        proj = matmul(xb, w_in_b, i, out_dtype=F32, tm=t["mm_m"], tn=768, b_cols_major=True)
        tail = matmul(xb, w_tail_b, i, out_dtype=F32, tm=t["mm_m"], tn=TAIL_WIDTH, act="rotate_tail",
                      b_cols_major=True)
        y_ret = retention_block(proj, cos_t, sin_t, ret_nw, i, t["mix"])
        y_m = mlstm_block(proj, tail, conv_w, conv_b, gate_bias, mlstm_nw, i, t["mix"])
        y_s = s5_block(tail, s5_operators(s5_in, i), s5_d, i, t["s5"])
        y_s = glu_block(y_s, glu_w_b, glu_b, i, t["row"])
        x1, x1b = outproj_block(y_ret, y_m, y_s, w_out_b, xf, ln1w, ln1b, i, t["row"])
        hid = matmul(x1b, w_up_b, i, out_dtype=BF16, tm=t["mm_m"], tn=1024, act="relu2")
        r, rb = down_block(hid, w_down_b, x1, i, t["down"])
        xf, xb = final_block(r, rb, p_b, w_gate_b, w_ple_b, ln2w, ln2b, i, t["row"])
    return xf.reshape(bsz, l, D_MODEL)
```

```python
import functools
import math

import jax
import jax.numpy as jnp
import numpy as np
from jax import lax
from jax.experimental import pallas as pl
from jax.experimental.pallas import tpu as pltpu

F32 = jnp.float32
BF16 = jnp.bfloat16

D_MODEL = 2048
DEPTH = 2
HEAD_DIM = 128
RET_HEADS = 6
MLSTM_HEADS = 6
RET_WIDTH = RET_HEADS * HEAD_DIM
MLSTM_WIDTH = MLSTM_HEADS * HEAD_DIM
S5_WIDTH = D_MODEL - RET_WIDTH - MLSTM_WIDTH
S5_GROUP = 16
S5_GROUPS = S5_WIDTH // S5_GROUP
S5_STATE = 64
CONV_WIDTH = 4
CHUNK = 128
D_FF = 4 * D_MODEL
PLE_DIM = 256
ROPE_BASE = 10000.0
LN_EPS = 1e-5
ALPHA = (2 * DEPTH) ** 0.25
QK_SCALE = HEAD_DIM ** -0.5

LANES = 128
SUBLANES = 8
MAIN_WIDTH = 4 * RET_WIDTH + 4 * MLSTM_WIDTH
TAIL_COLS = 2 * MLSTM_HEADS + S5_WIDTH
TAIL_WIDTH = 5 * LANES
GATE_LANE0 = LANES - 2 * MLSTM_HEADS
GATE_COLBLK = S5_WIDTH // LANES
LN_TILE = 256
CONV_ROWS = 64
S5_T = 16
S5_GB = S5_WIDTH // LANES
S5_GPB = LANES // S5_GROUP
S5_SB = S5_GPB * S5_STATE
S5_KT = S5_T * LANES
VMEM_LIMIT = 56 * 1024 * 1024


def _cparams(sem):
    return pltpu.CompilerParams(dimension_semantics=sem, vmem_limit_bytes=VMEM_LIMIT)


def _mm_kernel(a_ref, b_ref, o_ref, *scratch, nk, act, b_cols_major):
    def finish(r):
        if act == "relu2":
            r = jnp.square(jnp.maximum(r, 0.0))
        elif act == "rotate_tail":
            r = pltpu.roll(r, TAIL_WIDTH - 2 * MLSTM_HEADS, 1)
        o_ref[...] = r.astype(o_ref.dtype)

    if nk == 1:
        finish(_dot_nt(a_ref[...], b_ref[...]) if b_cols_major else _dot(a_ref[...], b_ref[...]))
        return
    acc_ref, = scratch
    k = pl.program_id(2)

    @pl.when(k == 0)
    def _():
        acc_ref[...] = jnp.zeros_like(acc_ref)

    acc_ref[...] += _dot(a_ref[...], b_ref[...])

    @pl.when(k == nk - 1)
    def _():
        finish(acc_ref[...])


def matmul(a, b, layer, *, out_dtype, tm, tn, tk=None, act=None, b_cols_major=False):
    m, kdim = a.shape
    n = b.shape[1] if b_cols_major else b.shape[2]
    tk = kdim if tk is None else tk
    nk = kdim // tk
    assert m % tm == 0 and n % tn == 0 and kdim % tk == 0 and b.shape[2 if b_cols_major else 1] == kdim
    assert nk == 1 or not b_cols_major
    b_spec = (pl.BlockSpec((None, tn, tk), lambda i, j, k: (layer, j, k)) if b_cols_major
              else pl.BlockSpec((None, tk, tn), lambda i, j, k: (layer, k, j)))
    return pl.pallas_call(
        functools.partial(_mm_kernel, nk=nk, act=act, b_cols_major=b_cols_major),
        out_shape=jax.ShapeDtypeStruct((m, n), out_dtype),
        grid=(m // tm, n // tn, nk),
        in_specs=[pl.BlockSpec((tm, tk), lambda i, j, k: (i, k)), b_spec],
        out_specs=pl.BlockSpec((tm, tn), lambda i, j, k: (i, j)),
        scratch_shapes=[] if nk == 1 else [pltpu.VMEM((tm, tn), F32)],
        compiler_params=_cparams(("parallel", "parallel", "arbitrary")),
    )(a, b)


def _cast_cols_kernel(x_ref, o_ref, *, col0, ncols_total):
    tc = x_ref.shape[0]
    col = col0 + pl.program_id(0) * tc + lax.broadcasted_iota(jnp.int32, (tc, 1), 0)
    for d in range(x_ref.shape[1]):
        o_ref[d] = jnp.where(col < ncols_total, x_ref[:, d, :], 0.0).astype(o_ref.dtype)


def cast_cols_transposed(w_t, col0, ncols, tc):
    ctot, depth, kdim = w_t.shape
    assert col0 % tc == 0 and ncols % tc == 0
    return pl.pallas_call(
        functools.partial(_cast_cols_kernel, col0=col0, ncols_total=ctot),
        out_shape=jax.ShapeDtypeStruct((depth, ncols, kdim), BF16),
        grid=(ncols // tc,),
        in_specs=[pl.BlockSpec((tc, depth, kdim), lambda i: (col0 // tc + i, 0, 0))],
        out_specs=pl.BlockSpec((depth, tc, kdim), lambda i: (0, i, 0)),
        compiler_params=_cparams(("parallel",)),
    )(w_t)


def _rope_kernel(pos_ref, inv_ref, sign_ref, cos_ref, sin_ref):
    ang = pos_ref[...] * inv_ref[...]
    cos_ref[...] = jnp.cos(ang)
    sin_ref[...] = jnp.sin(ang) * sign_ref[...]


def rope_tables(pos_col, tl):
    l = pos_col.shape[0]
    half = np.arange(0, HEAD_DIM, 2, dtype=np.float32) / np.float32(HEAD_DIM)
    inv = (np.float32(ROPE_BASE) ** (-half)).astype(np.float32)
    inv2 = jnp.asarray(np.concatenate([inv, inv])[None, :])
    sign = jnp.asarray(np.concatenate([-np.ones(64, np.float32), np.ones(64, np.float32)])[None, :])
    row = pl.BlockSpec((1, LANES), lambda i: (0, 0))
    return pl.pallas_call(
        _rope_kernel,
        out_shape=[jax.ShapeDtypeStruct((l, LANES), F32)] * 2,
        grid=(l // tl,),
        in_specs=[pl.BlockSpec((tl, 1), lambda i: (i, 0)), row, row],
        out_specs=[pl.BlockSpec((tl, LANES), lambda i: (i, 0))] * 2,
        compiler_params=_cparams(("parallel",)),
    )(pos_col, inv2, sign)


def _head_norm(y):
    mu = jnp.mean(y, axis=-1, keepdims=True)
    d = y - mu
    var = jnp.mean(d * d, axis=-1, keepdims=True)
    return d * lax.rsqrt(var + LN_EPS)


def _mean_lanes_mxu(x):
    hi = x.astype(BF16)
    lo = (x - hi.astype(F32)).astype(BF16)
    avg = jnp.full((2 * HEAD_DIM, HEAD_DIM), 1.0 / HEAD_DIM, BF16)
    return _dot(jnp.concatenate([hi, lo], axis=1), avg)


def _head_norm_mxu(y):
    d = y - _mean_lanes_mxu(y)
    return d * lax.rsqrt(_mean_lanes_mxu(d * d) + LN_EPS)


def _sigmoid(x):
    return 1.0 / (1.0 + jnp.exp(-x))


def _dot_nt(a, b, precision=None):
    return lax.dot_general(a, b, (((1,), (1,)), ((), ())), precision=precision, preferred_element_type=F32)


def _dot(a, b):
    return jnp.dot(a, b, preferred_element_type=F32)


def _dot_tn(a, b):
    return lax.dot_general(a, b, (((0,), (0,)), ((), ())), preferred_element_type=F32)


def _head_lanes(h):
    return slice(h * HEAD_DIM, (h + 1) * HEAD_DIM)


def _layer_spec(stacked, layer):
    return pl.BlockSpec((None,) + stacked.shape[1:], lambda *_: (layer, 0, 0))


def _ret_log_gamma(h):
    return float(np.log(np.float32(1.0) - np.float32(2.0) ** np.float32(-5.0 - h)))


def _ret_kernel(q_ref, k_ref, v_ref, g_ref, cos_ref, sin_ref, w_ref, o_ref,
                state_ref, decay_ref, qdec_ref, kdec_ref, qb_buf, s_buf, kdt_buf, intra_buf, incr_buf, *, nchunk):
    rb = pl.program_id(0)

    @pl.when(rb == 0)
    def _():
        state_ref[...] = jnp.zeros_like(state_ref)
        ii = lax.broadcasted_iota(jnp.int32, (CHUNK, CHUNK), 0)
        jj = lax.broadcasted_iota(jnp.int32, (CHUNK, CHUNK), 1)
        rel = (ii - jj).astype(F32)
        idx = ii.astype(F32)
        for h in range(RET_HEADS):
            lg = _ret_log_gamma(h)
            decay_ref[h] = jnp.where(rel >= 0.0, jnp.exp(lg * jnp.maximum(rel, 0.0)), 0.0)
            qdec_ref[h] = jnp.exp(lg * (idx + 1.0))
            kdec_ref[h] = jnp.exp(lg * (CHUNK - 1.0 - idx))

    for c in range(nchunk):
        rows = slice(c * CHUNK, (c + 1) * CHUNK)
        cos = cos_ref[rows, :]
        sin = sin_ref[rows, :]
        for h in range(RET_HEADS):
            ln = _head_lanes(h)
            q = q_ref[rows, ln]
            k = k_ref[rows, ln]
            qr = q * cos + pltpu.roll(q, HEAD_DIM // 2, 1) * sin
            kr = (k * cos + pltpu.roll(k, HEAD_DIM // 2, 1) * sin) * QK_SCALE
            qb = qr.astype(BF16)
            qb_buf[rows, ln] = qb
            s_buf[rows, ln] = (_dot_nt(qb, kr.astype(BF16)) * decay_ref[h]).astype(BF16)
            kdt_buf[rows, ln] = jnp.transpose((kr * kdec_ref[h]).astype(BF16))

    for c in range(nchunk):
        rows = slice(c * CHUNK, (c + 1) * CHUNK)
        for h in range(RET_HEADS):
            ln = _head_lanes(h)
            lhs = jnp.concatenate([s_buf[rows, ln], kdt_buf[rows, ln]], axis=0)
            both = _dot(lhs, v_ref[rows, ln].astype(BF16))
            intra_buf[rows, ln] = both[:CHUNK]
            incr_buf[c, h] = both[CHUNK:]

    for c in range(nchunk):
        rows = slice(c * CHUNK, (c + 1) * CHUNK)
        for h in range(RET_HEADS):
            ln = _head_lanes(h)
            st = state_ref[h]
            out = intra_buf[rows, ln] + _dot(qb_buf[rows, ln], st.astype(BF16)) * qdec_ref[h]
            state_ref[h] = st * math.exp(_ret_log_gamma(h) * CHUNK) + incr_buf[c, h]
            g = g_ref[rows, ln]
            y = _head_norm(out) * w_ref[:, ln] * (g * _sigmoid(g))
            o_ref[rows, ln] = y.astype(o_ref.dtype)


def retention_block(proj, cos_t, sin_t, norm_w, layer, tb):
    l = proj.shape[0]

    def col(j):
        return pl.BlockSpec((tb, RET_WIDTH), lambda r, j=j: (r, j))

    tab = pl.BlockSpec((tb, LANES), lambda r: (r, 0))
    hsq = pltpu.VMEM((RET_HEADS, CHUNK, CHUNK), F32)
    half = pltpu.VMEM((tb, RET_WIDTH), BF16)
    return pl.pallas_call(
        functools.partial(_ret_kernel, nchunk=tb // CHUNK),
        out_shape=jax.ShapeDtypeStruct((l, RET_WIDTH), BF16),
        grid=(l // tb,),
        in_specs=[col(0), col(1), col(2), col(3), tab, tab, _layer_spec(norm_w, layer)],
        out_specs=pl.BlockSpec((tb, RET_WIDTH), lambda r: (r, 0)),
        scratch_shapes=[hsq, hsq, hsq, hsq, half, half, half, pltpu.VMEM((tb, RET_WIDTH), F32),
                        pltpu.VMEM((tb // CHUNK, RET_HEADS, HEAD_DIM, HEAD_DIM), F32)],
        compiler_params=_cparams(("arbitrary",)),
    )(proj, proj, proj, proj, cos_t, sin_t, norm_w)


def _log_sigmoid(x):
    return -(jnp.maximum(-x, 0.0) + jnp.log1p(jnp.exp(-jnp.abs(x))))


def _mlstm_kernel(q_ref, k_ref, v_ref, og_ref, gate_ref, cw_ref, cb_ref, gb_ref, nw_ref,
                  o_ref, c_st, m_st, qbuf, kbuf, qs, ks, gs, r_buf, cmax_buf, bcum_buf, mall_buf, floor_buf,
                  p_buf, u_buf, wq_buf, s_buf, kwt_buf, *, nchunk, tb):
    rb = pl.program_id(0)
    pad = SUBLANES
    nh = MLSTM_HEADS

    @pl.when(rb == 0)
    def _():
        c_st[...] = jnp.zeros_like(c_st)
        m_st[...] = jnp.zeros_like(m_st)
        qbuf[0:pad, :] = jnp.zeros((pad, MLSTM_WIDTH), F32)
        kbuf[0:pad, :] = jnp.zeros((pad, MLSTM_WIDTH), F32)

    qbuf[pad:pad + tb, :] = q_ref[...]
    kbuf[pad:pad + tb, :] = k_ref[...]
    for r0 in range(0, tb, CONV_ROWS):
        for buf, woff, dst, scale in ((qbuf, 0, qs, None), (kbuf, MLSTM_WIDTH, ks, QK_SCALE)):
            wl = slice(woff, woff + MLSTM_WIDTH)
            acc = jnp.broadcast_to(cb_ref[:, wl], (CONV_ROWS, MLSTM_WIDTH))
            for tap in range(CONV_WIDTH):
                off = r0 + pad - (CONV_WIDTH - 1) + tap
                acc = acc + buf[off:off + CONV_ROWS, :] * cw_ref[tap:tap + 1, wl]
            act = acc * _sigmoid(acc)
            dst[r0:r0 + CONV_ROWS, :] = act if scale is None else act * scale
    qbuf[0:pad, :] = q_ref[tb - pad:tb, :]
    kbuf[0:pad, :] = k_ref[tb - pad:tb, :]

    lane_row = lax.broadcasted_iota(jnp.int32, (1, LANES), 1)
    graw = pltpu.roll(gate_ref[...] + gb_ref[...], LANES - GATE_LANE0, 1)
    gs[...] = jnp.where(lane_row < nh, graw, jnp.where(lane_row < 2 * nh, _log_sigmoid(graw), 0.0))

    ii = lax.broadcasted_iota(jnp.int32, (CHUNK, CHUNK), 0)
    jj = lax.broadcasted_iota(jnp.int32, (CHUNK, CHUNK), 1)
    causal = ii >= jj
    tril = causal.astype(F32)
    lane8 = lax.broadcasted_iota(jnp.int32, (SUBLANES, LANES), 1)
    ones_b = jnp.ones((CHUNK, HEAD_DIM), BF16)
    full = (CHUNK, CHUNK)

    def chunk_rows(c):
        return slice(c * CHUNK, (c + 1) * CHUNK)

    def aug_lanes(h):
        return slice(h * 2 * HEAD_DIM, (h + 1) * 2 * HEAD_DIM)

    for c in range(nchunk):
        rows = chunk_rows(c)
        g = gs[rows, :]
        gcum = jnp.dot(tril, g, precision=lax.Precision.HIGHEST, preferred_element_type=F32)
        bcum = pltpu.roll(gcum, LANES - nh, 1)
        r8 = jnp.transpose(g - bcum)[0:SUBLANES, :]
        cmax = r8
        for sh in (1, 2, 4, 8, 16, 32, 64):
            cmax = jnp.maximum(cmax, jnp.where(lane8 >= sh, pltpu.roll(cmax, sh, 1), -jnp.inf))
        r_buf[c] = r8
        cmax_buf[rows, :] = jnp.transpose(
            jnp.concatenate([cmax, jnp.zeros((CHUNK - SUBLANES, LANES), F32)], axis=0))
        bcum_buf[rows, :] = bcum

    m_prev = m_st[0:1, :]
    m_prevs, keep_rows = [], []
    for c in range(nchunk):
        rows = chunk_rows(c)
        m_all = jnp.maximum(cmax_buf[rows, :], m_prev)
        floor_buf[rows, :] = jnp.exp(-(bcum_buf[rows, :] + m_all))
        mall_buf[rows, :] = m_all
        m_last = m_all[CHUNK - 1:CHUNK, :]
        m_prevs.append(m_prev)
        keep_rows.append(jnp.exp(m_prev - m_last))
        m_prev = bcum_buf[c * CHUNK + CHUNK - 1:(c + 1) * CHUNK, :] + m_last
    m_st[0:1, :] = m_prev

    for c in range(nchunk):
        rows = chunk_rows(c)
        for h in range(nh):
            ln = _head_lanes(h)
            qc = qs[rows, ln]
            kb = ks[rows, ln].astype(BF16)
            m_bc = jnp.broadcast_to(mall_buf[rows, h:h + 1], full)
            dmat = jnp.exp(jnp.where(causal, r_buf[c, h:h + 1, :] - m_bc, -jnp.inf))
            s_buf[rows, ln] = (_dot_nt(qc.astype(BF16), kb) * dmat).astype(BF16)
            kwt_buf[rows, ln] = (jnp.transpose(kb).astype(F32) * dmat[CHUNK - 1:CHUNK, :]).astype(BF16)
            wq_buf[rows, ln] = (jnp.exp(m_prevs[c][:, h:h + 1] - m_bc) * qc).astype(BF16)

    for c in range(nchunk):
        rows = chunk_rows(c)
        for h in range(nh):
            ln = _head_lanes(h)
            v_aug = jnp.concatenate([v_ref[rows, ln].astype(BF16), ones_b], axis=1)
            both = _dot(jnp.concatenate([s_buf[rows, ln], kwt_buf[rows, ln]], axis=0), v_aug)
            p_buf[rows, aug_lanes(h)] = both[:CHUNK]
            u_buf[c, h] = both[CHUNK:]

    for c in range(nchunk):
        rows = chunk_rows(c)
        for h in range(nh):
            ln = _head_lanes(h)
            cs = c_st[h]
            res = p_buf[rows, aug_lanes(h)] + _dot(wq_buf[rows, ln], cs.astype(BF16))
            c_st[h] = keep_rows[c][:, h:h + 1] * cs + u_buf[c, h]
            den = res[:, HEAD_DIM + h:HEAD_DIM + h + 1]
            inv = 1.0 / jnp.maximum(jnp.abs(den), floor_buf[rows, h:h + 1])
            y = _head_norm(res[:, :HEAD_DIM] * inv) * nw_ref[:, ln] * _sigmoid(og_ref[rows, ln])
            o_ref[rows, ln] = y.astype(o_ref.dtype)


def mlstm_block(proj, tail, conv_w, conv_b, gate_bias, norm_w, layer, tb):
    l = proj.shape[0]
    base = (4 * RET_WIDTH) // MLSTM_WIDTH

    def col(j):
        return pl.BlockSpec((tb, MLSTM_WIDTH), lambda r, j=j: (r, base + j))

    nchunk = tb // CHUNK
    wide = pltpu.VMEM((tb, MLSTM_WIDTH), F32)
    hist = pltpu.VMEM((tb + SUBLANES, MLSTM_WIDTH), F32)
    narrow = pltpu.VMEM((tb, LANES), F32)
    half = pltpu.VMEM((tb, MLSTM_WIDTH), BF16)
    rows8 = pltpu.VMEM((nchunk, SUBLANES, LANES), F32)
    return pl.pallas_call(
        functools.partial(_mlstm_kernel, nchunk=nchunk, tb=tb),
        out_shape=jax.ShapeDtypeStruct((l, MLSTM_WIDTH), BF16),
        grid=(l // tb,),
        in_specs=[col(0), col(1), col(2), col(3),
                  pl.BlockSpec((tb, LANES), lambda r: (r, GATE_COLBLK)),
                  _layer_spec(conv_w, layer), _layer_spec(conv_b, layer), _layer_spec(gate_bias, layer),
                  _layer_spec(norm_w, layer)],
        out_specs=pl.BlockSpec((tb, MLSTM_WIDTH), lambda r: (r, 0)),
        scratch_shapes=[pltpu.VMEM((MLSTM_HEADS, HEAD_DIM, 2 * HEAD_DIM), F32), pltpu.VMEM((SUBLANES, LANES), F32),
                        hist, hist, wide, wide, narrow, rows8, narrow, narrow, narrow, narrow,
                        pltpu.VMEM((tb, 2 * MLSTM_WIDTH), F32),
                        pltpu.VMEM((nchunk, MLSTM_HEADS, HEAD_DIM, 2 * HEAD_DIM), F32),
                        half, half, half],
        compiler_params=_cparams(("arbitrary",)),
    )(proj, proj, proj, proj, tail, conv_w, conv_b, gate_bias, norm_w)


def _s5_prep_kernel(lre_ref, lim_ref, ldt_ref, btre_ref, btim_ref, ctre_ref, ctim_ref,
                    toep_ref, vre_ref, vim_ref, wre_ref, wim_ref, apre_ref, apim_ref):
    lre = lre_ref[...]
    lim = lim_ref[...]
    dt = jnp.exp(ldt_ref[...])
    mag = jnp.exp(lre * dt)
    ang = lim * dt
    zr = mag * jnp.cos(ang) - 1.0
    zi = mag * jnp.sin(ang)
    den = lre * lre + lim * lim
    w_re = (zr * lre + zi * lim) / den
    w_im = (zi * lre - zr * lim) / den
    row_g = lax.broadcasted_iota(jnp.int32, (LANES, S5_SB), 0) // S5_GROUP
    col_g = lax.broadcasted_iota(jnp.int32, (LANES, S5_SB), 1) // S5_STATE
    same = row_g == col_g
    bt_re = btre_ref[...]
    bt_im = btim_ref[...]
    bb_re = jnp.where(same, w_re * bt_re - w_im * bt_im, 0.0)
    bb_im = jnp.where(same, w_re * bt_im + w_im * bt_re, 0.0)
    ct_re = jnp.where(same, ctre_ref[...], 0.0)
    ct_im = jnp.where(same, ctim_ref[...], 0.0)
    hp = lax.Precision.HIGHEST
    kblk = []
    for d in range(S5_T + 1):
        pm = jnp.exp(lre * dt * float(d))
        pa = lim * dt * float(d)
        p_re = pm * jnp.cos(pa)
        p_im = pm * jnp.sin(pa)
        if d < S5_T:
            ab_re = p_re * bb_re - p_im * bb_im
            ab_im = p_re * bb_im + p_im * bb_re
            srow = slice((S5_T - 1 - d) * LANES, (S5_T - d) * LANES)
            vre_ref[0, srow, :] = ab_re.astype(BF16)
            vim_ref[0, srow, :] = ab_im.astype(BF16)
            kblk.append((_dot_nt(ab_re, ct_re, hp) - _dot_nt(ab_im, ct_im, hp)).astype(BF16))
        if d >= 1:
            trow = slice((d - 1) * LANES, d * LANES)
            wre_ref[0, trow, :] = (ct_re * p_re - ct_im * p_im).astype(BF16)
            wim_ref[0, trow, :] = (-(ct_re * p_im + ct_im * p_re)).astype(BF16)
    apre_ref[...] = jnp.zeros_like(apre_ref)
    apim_ref[...] = jnp.zeros_like(apim_ref)
    for k in range(SUBLANES + 1):
        pm = jnp.exp(lre * dt * float(S5_T * k))
        pa = lim * dt * float(S5_T * k)
        apre_ref[k:k + 1, :] = pm * jnp.cos(pa)
        apim_ref[k:k + 1, :] = pm * jnp.sin(pa)
    zero = jnp.zeros((LANES, LANES), BF16)
    for s in range(S5_T):
        for t in range(S5_T):
            toep_ref[0, s * LANES:(s + 1) * LANES, t * LANES:(t + 1) * LANES] = kblk[t - s] if t >= s else zero


def s5_operator_inputs(lam_re, lam_im, log_dt, b_re, b_im, c_re, c_im):
    depth = lam_re.shape[0]
    nst = S5_GROUPS * S5_STATE

    def tiled(m):
        return jnp.tile(m.astype(F32).reshape(depth, S5_WIDTH, S5_STATE), (1, 1, S5_GPB))

    lre = lam_re.astype(F32).reshape(depth, 1, nst)
    lim = lam_im.astype(F32).reshape(depth, 1, nst)
    ldt = jnp.repeat(log_dt.astype(F32), S5_STATE, axis=-1).reshape(depth, 1, nst)
    return (lre, lim, ldt, tiled(jnp.swapaxes(b_re, -1, -2)), tiled(jnp.swapaxes(b_im, -1, -2)),
            tiled(c_re), tiled(c_im))


def s5_operators(prep_inputs, layer):
    nst = S5_GROUPS * S5_STATE
    lane_in = pl.BlockSpec((None, 1, S5_SB), lambda g: (layer, 0, g))
    par_in = pl.BlockSpec((None, LANES, S5_SB), lambda g: (layer, g, 0))
    pow_out = pl.BlockSpec((2 * SUBLANES, S5_SB), lambda g: (0, g))

    def out_blk(rows, cols):
        return pl.BlockSpec((1, rows, cols), lambda g: (g, 0, 0))

    return pl.pallas_call(
        _s5_prep_kernel,
        out_shape=[jax.ShapeDtypeStruct((S5_GB, S5_KT, S5_KT), BF16)]
        + [jax.ShapeDtypeStruct((S5_GB, S5_KT, S5_SB), BF16)] * 4
        + [jax.ShapeDtypeStruct((2 * SUBLANES, nst), F32)] * 2,
        grid=(S5_GB,),
        in_specs=[lane_in, lane_in, lane_in, par_in, par_in, par_in, par_in],
        out_specs=[out_blk(S5_KT, S5_KT)] + [out_blk(S5_KT, S5_SB)] * 4 + [pow_out, pow_out],
        compiler_params=_cparams(("parallel",)),
    )(*prep_inputs)


def _gelu_tanh(x):
    c = math.sqrt(2.0 / math.pi)
    return 0.5 * x * (1.0 + jnp.tanh(c * (x + 0.044715 * (x * x * x))))


def _s5_kernel(u_ref, toep_ref, vre_ref, vim_ref, wre_ref, wim_ref, apre_ref, apim_ref, d_ref, y_ref,
               car_re, car_im, x_re, x_im, *, tmc):
    rb = pl.program_id(1)

    @pl.when(rb == 0)
    def _():
        car_re[...] = jnp.zeros_like(car_re)
        car_im[...] = jnp.zeros_like(car_im)

    us = [u_ref[pl.ds(s, tmc, stride=S5_T), :] for s in range(S5_T)]
    ucat = jnp.concatenate([u.astype(BF16) for u in us], axis=1)
    y_re = _dot(ucat, vre_ref[0])
    y_im = _dot(ucat, vim_ref[0])

    row_in_tile = lax.broadcasted_iota(jnp.int32, (tmc, 1), 0) & (SUBLANES - 1)

    def shifted(v, s):
        return jnp.where(row_in_tile >= s, pltpu.roll(v, s, 0), 0.0)

    for s in (1, 2, 4):
        a_r = apre_ref[s:s + 1, :]
        a_i = apim_ref[s:s + 1, :]
        s_re, s_im = shifted(y_re, s), shifted(y_im, s)
        y_re, y_im = y_re + a_r * s_re - a_i * s_im, y_im + a_r * s_im + a_i * s_re
    e_re, e_im = shifted(y_re, 1), shifted(y_im, 1)
    p_re = apre_ref[0:SUBLANES, :]
    p_im = apim_ref[0:SUBLANES, :]
    a8_r = apre_ref[SUBLANES:SUBLANES + 1, :]
    a8_i = apim_ref[SUBLANES:SUBLANES + 1, :]
    xr = car_re[0:1, :]
    xi = car_im[0:1, :]
    for t in range(tmc // SUBLANES):
        rows = slice(t * SUBLANES, (t + 1) * SUBLANES)
        x_re[rows, :] = e_re[rows] + p_re * xr - p_im * xi
        x_im[rows, :] = e_im[rows] + p_re * xi + p_im * xr
        last = (t + 1) * SUBLANES - 1
        xr, xi = (a8_r * xr - a8_i * xi + y_re[last:last + 1], a8_r * xi + a8_i * xr + y_im[last:last + 1])
    car_re[0:1, :] = xr
    car_im[0:1, :] = xi
    y = (_dot(ucat, toep_ref[0]) + _dot_nt(x_re[...].astype(BF16), wre_ref[0])
         + _dot_nt(x_im[...].astype(BF16), wim_ref[0]))
    for t in range(S5_T):
        yt = y[:, t * LANES:(t + 1) * LANES] + d_ref[...] * us[t]
        y_ref[pl.ds(t, tmc, stride=S5_T), :] = _gelu_tanh(yt)


def s5_block(tail, ops, d_skip, layer, tb):
    l = tail.shape[0]
    toep, v_re, v_im, w_re, w_im, ap_re, ap_im = ops
    tmc = tb // S5_T

    def per_gb(cols):
        return pl.BlockSpec((1, S5_KT, cols), lambda g, r: (g, 0, 0))

    pow_blk = pl.BlockSpec((2 * SUBLANES, S5_SB), lambda g, r: (0, g))
    fold = pltpu.VMEM((tmc, S5_SB), F32)
    carry = pltpu.VMEM((SUBLANES, S5_SB), F32)
    return pl.pallas_call(
        functools.partial(_s5_kernel, tmc=tmc),
        out_shape=jax.ShapeDtypeStruct((l, S5_WIDTH), F32),
        grid=(S5_GB, l // tb),
        in_specs=[pl.BlockSpec((tb, LANES), lambda g, r: (r, g)),
                  per_gb(S5_KT), per_gb(S5_SB), per_gb(S5_SB), per_gb(S5_SB), per_gb(S5_SB),
                  pow_blk, pow_blk, pl.BlockSpec((None, 1, LANES), lambda g, r: (layer, 0, g))],
        out_specs=pl.BlockSpec((tb, LANES), lambda g, r: (r, g)),
        scratch_shapes=[carry, carry, fold, fold],
        compiler_params=_cparams(("parallel", "arbitrary")),
    )(tail, toep, v_re, v_im, w_re, w_im, ap_re, ap_im, d_skip)


def _glu_kernel(y_ref, w_ref, b_ref, o_ref):
    y = y_ref[...]
    z = _dot(y.astype(BF16), w_ref[...]) + b_ref[...]
    o_ref[...] = (y * _sigmoid(z)).astype(o_ref.dtype)


def glu_block(y, w, b, layer, tm):
    l, n = y.shape
    return pl.pallas_call(
        _glu_kernel,
        out_shape=jax.ShapeDtypeStruct((l, n), BF16),
        grid=(l // tm,),
        in_specs=[pl.BlockSpec((tm, n), lambda i: (i, 0)), _layer_spec(w, layer), _layer_spec(b, layer)],
        out_specs=pl.BlockSpec((tm, n), lambda i: (i, 0)),
        compiler_params=_cparams(("parallel",)),
    )(y, w, b)


def _layer_norm_by_column_tiles(z_tile, z_buf, lw_ref, lb_ref, o_ref, ob_ref):
    rows = z_buf.shape[0]
    pivot = None
    s1 = jnp.zeros((rows, LANES), F32)
    s2 = jnp.zeros((rows, LANES), F32)
    for n in range(D_MODEL // LN_TILE):
        cols = slice(n * LN_TILE, (n + 1) * LN_TILE)
        z = z_tile(cols)
        z_buf[:, cols] = z
        if pivot is None:
            pivot = jnp.mean(z, axis=-1, keepdims=True)
        for j in range(LN_TILE // LANES):
            dz = z[:, j * LANES:(j + 1) * LANES] - pivot
            s1 = s1 + dz
            s2 = s2 + dz * dz
    m1 = jnp.sum(s1, axis=-1, keepdims=True) * (1.0 / D_MODEL)
    var = jnp.sum(s2, axis=-1, keepdims=True) * (1.0 / D_MODEL) - m1 * m1
    mu = pivot + m1
    rstd = lax.rsqrt(var + LN_EPS)
    for n in range(D_MODEL // LN_TILE):
        cols = slice(n * LN_TILE, (n + 1) * LN_TILE)
        out = (z_buf[:, cols] - mu) * rstd * lw_ref[:, cols] + lb_ref[:, cols]
        o_ref[:, cols] = out
        ob_ref[:, cols] = out.astype(BF16)


def _outproj_kernel(a1_ref, a2_ref, a3_ref, w1_ref, w2_ref, w3_ref, x_ref, lw_ref, lb_ref, o_ref, ob_ref, z_buf):
    a1, a2, a3 = a1_ref[...], a2_ref[...], a3_ref[...]

    def z_tile(cols):
        mix = _dot(a1, w1_ref[:, cols]) + _dot(a2, w2_ref[:, cols]) + _dot(a3, w3_ref[:, cols])
        return ALPHA * x_ref[:, cols] + mix

    _layer_norm_by_column_tiles(z_tile, z_buf, lw_ref, lb_ref, o_ref, ob_ref)


def outproj_block(y_ret, y_m, y_s, w_out, x, ln_w, ln_b, layer, tm):
    l = x.shape[0]

    def rows(n):
        return pl.BlockSpec((tm, n), lambda i: (i, 0))

    def wrows(n, blk):
        return pl.BlockSpec((None, n, D_MODEL), lambda i: (layer, blk, 0))

    return pl.pallas_call(
        _outproj_kernel,
        out_shape=[jax.ShapeDtypeStruct((l, D_MODEL), F32), jax.ShapeDtypeStruct((l, D_MODEL), BF16)],
        grid=(l // tm,),
        in_specs=[rows(RET_WIDTH), rows(MLSTM_WIDTH), rows(S5_WIDTH),
                  wrows(RET_WIDTH, 0), wrows(MLSTM_WIDTH, 1), wrows(S5_WIDTH, (RET_WIDTH + MLSTM_WIDTH) // S5_WIDTH),
                  rows(D_MODEL), _layer_spec(ln_w, layer), _layer_spec(ln_b, layer)],
        out_specs=[rows(D_MODEL), rows(D_MODEL)],
        scratch_shapes=[pltpu.VMEM((tm, D_MODEL), F32)],
        compiler_params=_cparams(("parallel",)),
    )(y_ret, y_m, y_s, w_out, w_out, w_out, x, ln_w, ln_b)


def _down_kernel(h_ref, w_ref, x1_ref, r_ref, rb_ref):
    r = ALPHA * x1_ref[...] + _dot(h_ref[...], w_ref[...])
    r_ref[...] = r
    rb_ref[...] = r.astype(BF16)


def down_block(hid, w_down, x1, layer, tm):
    l, ff = hid.shape
    rows = pl.BlockSpec((tm, D_MODEL), lambda i: (i, 0))
    return pl.pallas_call(
        _down_kernel,
        out_shape=[jax.ShapeDtypeStruct((l, D_MODEL), F32), jax.ShapeDtypeStruct((l, D_MODEL), BF16)],
        grid=(l // tm,),
        in_specs=[pl.BlockSpec((tm, ff), lambda i: (i, 0)),
                  pl.BlockSpec((None, ff, D_MODEL), lambda i: (layer, 0, 0), pipeline_mode=pl.Buffered(1)), rows],
        out_specs=[rows, rows],
        compiler_params=_cparams(("parallel",)),
    )(hid, w_down, x1)


def _final_kernel(r_ref, rb_ref, p_ref, wg_ref, wp_ref, lw_ref, lb_ref, o_ref, ob_ref, z_buf):
    rb, pb = rb_ref[...], p_ref[...]

    def z_tile(cols):
        gate = _sigmoid(_dot(rb, wg_ref[:, cols]))
        return r_ref[:, cols] + gate * _dot(pb, wp_ref[:, cols])

    _layer_norm_by_column_tiles(z_tile, z_buf, lw_ref, lb_ref, o_ref, ob_ref)


def final_block(r, rb, p_b, w_gate, w_ple, ln_w, ln_b, layer, tm):
    l = r.shape[0]
    rows = pl.BlockSpec((tm, D_MODEL), lambda i: (i, 0))
    return pl.pallas_call(
        _final_kernel,
        out_shape=[jax.ShapeDtypeStruct((l, D_MODEL), F32), jax.ShapeDtypeStruct((l, D_MODEL), BF16)],
        grid=(l // tm,),
        in_specs=[rows, rows, pl.BlockSpec((None, tm, PLE_DIM), lambda i: (layer, i, 0)),
                  _layer_spec(w_gate, layer), _layer_spec(w_ple, layer), _layer_spec(ln_w, layer),
                  _layer_spec(ln_b, layer)],
        out_specs=[rows, rows],
        scratch_shapes=[pltpu.VMEM((tm, D_MODEL), F32)],
        compiler_params=_cparams(("parallel",)),
    )(r, rb, p_b, w_gate, w_ple, ln_w, ln_b)


def _rows3(v):
    return v.astype(F32).reshape(v.shape[0], 1, -1)


def _tiles(l):
    return {"mm_m": min(1024, l), "mix": min(512, l), "s5": min(2048, l), "row": min(512, l), "down": min(256, l), "in_n": 1536, "up_n": 2048,
            "rope": min(2048, l)}


def kernel(x, p, positions, w_in, mlstm_conv_w, mlstm_conv_b, mlstm_i_bias, mlstm_f_bias, ret_norm_w, mlstm_norm_w, s5_lambda_re, s5_lambda_im, s5_log_dt, s5_B_re, s5_B_im, s5_C_re, s5_C_im, s5_D, s5_glu_w, s5_glu_b, w_out, ln1_w, ln1_b, w_up, w_down, w_gate, w_ple, ln2_w, ln2_b):
    bsz, l, _ = x.shape
    depth = w_in.shape[0]
    assert bsz == 1
    t = _tiles(l)
    cos_t, sin_t = rope_tables(positions.astype(F32).reshape(l, 1), t["rope"])
    xf = x.reshape(l, D_MODEL).astype(F32)
    xb = xf.astype(BF16)
    p_b = p.reshape(depth, l, PLE_DIM).astype(BF16)

    w_in_t = jnp.transpose(w_in, (2, 0, 1))
    w_in_b = cast_cols_transposed(w_in_t, 0, MAIN_WIDTH, LANES)
    w_tail_b = cast_cols_transposed(w_in_t, MAIN_WIDTH, TAIL_WIDTH, LANES)
    w_out_b, w_up_b, w_down_b = w_out.astype(BF16), w_up.astype(BF16), w_down.astype(BF16)
    w_gate_b, w_ple_b, glu_w_b = w_gate.astype(BF16), w_ple.astype(BF16), s5_glu_w.astype(BF16)
    gate_bias = jnp.concatenate([jnp.zeros((depth, GATE_LANE0), F32), mlstm_i_bias.astype(F32),
                                 mlstm_f_bias.astype(F32)], axis=1).reshape(depth, 1, LANES)
    conv_w = mlstm_conv_w.astype(F32)
    conv_b, ret_nw, mlstm_nw = _rows3(mlstm_conv_b), _rows3(ret_norm_w), _rows3(mlstm_norm_w)
    s5_d, glu_b = _rows3(s5_D), _rows3(s5_glu_b)
    ln1w, ln1b, ln2w, ln2b = _rows3(ln1_w), _rows3(ln1_b), _rows3(ln2_w), _rows3(ln2_b)
    s5_in = s5_operator_inputs(s5_lambda_re, s5_lambda_im, s5_log_dt, s5_B_re, s5_B_im, s5_C_re, s5_C_im)

    for i in range(depth):
        proj = matmul(xb, w_in_b, i, out_dtype=F32, tm=t["mm_m"], tn=t["in_n"], b_cols_major=True)
        tail = matmul(xb, w_tail_b, i, out_dtype=F32, tm=t["mm_m"], tn=TAIL_WIDTH, act="rotate_tail",
                      b_cols_major=True)
        y_ret = retention_block(proj, cos_t, sin_t, ret_nw, i, t["mix"])
        y_m = mlstm_block(proj, tail, conv_w, conv_b, gate_bias, mlstm_nw, i, t["mix"])
        y_s = s5_block(tail, s5_operators(s5_in, i), s5_d, i, t["s5"])
        y_s = glu_block(y_s, glu_w_b, glu_b, i, t["row"])
        x1, x1b = outproj_block(y_ret, y_m, y_s, w_out_b, xf, ln1w, ln1b, i, t["row"])
        hid = matmul(x1b, w_up_b, i, out_dtype=BF16, tm=t["mm_m"], tn=t["up_n"], act="relu2")
        r, rb = down_block(hid, w_down_b, x1, i, t["down"])
        xf, xb = final_block(r, rb, p_b, w_gate_b, w_ple_b, ln2w, ln2b, i, t["row"])
    return xf.reshape(bsz, l, D_MODEL)
```

```python
import functools
import math

import jax
import jax.numpy as jnp
import numpy as np
from jax import lax
from jax.experimental import pallas as pl
from jax.experimental.pallas import tpu as pltpu

F32 = jnp.float32
BF16 = jnp.bfloat16

D_MODEL = 2048
DEPTH = 2
HEAD_DIM = 128
RET_HEADS = 6
MLSTM_HEADS = 6
RET_WIDTH = RET_HEADS * HEAD_DIM
MLSTM_WIDTH = MLSTM_HEADS * HEAD_DIM
S5_WIDTH = D_MODEL - RET_WIDTH - MLSTM_WIDTH
S5_GROUP = 16
S5_GROUPS = S5_WIDTH // S5_GROUP
S5_STATE = 64
CONV_WIDTH = 4
CHUNK = 128
D_FF = 4 * D_MODEL
PLE_DIM = 256
ROPE_BASE = 10000.0
LN_EPS = 1e-5
ALPHA = (2 * DEPTH) ** 0.25
QK_SCALE = HEAD_DIM ** -0.5

LANES = 128
SUBLANES = 8
MAIN_WIDTH = 4 * RET_WIDTH + 4 * MLSTM_WIDTH
TAIL_COLS = 2 * MLSTM_HEADS + S5_WIDTH
TAIL_WIDTH = 5 * LANES
GATE_LANE0 = LANES - 2 * MLSTM_HEADS
GATE_COLBLK = S5_WIDTH // LANES
LN_TILE = 256
CONV_ROWS = 64
S5_T = 16
S5_GB = S5_WIDTH // LANES
S5_GPB = LANES // S5_GROUP
S5_SB = S5_GPB * S5_STATE
S5_KT = S5_T * LANES
VMEM_LIMIT = 56 * 1024 * 1024


def _cparams(sem):
    return pltpu.CompilerParams(dimension_semantics=sem, vmem_limit_bytes=VMEM_LIMIT)


def _mm_kernel(a_ref, b_ref, o_ref, *scratch, nk, act, b_cols_major):
    def finish(r):
        if act == "relu2":
            r = jnp.square(jnp.maximum(r, 0.0))
        elif act == "rotate_tail":
            r = pltpu.roll(r, TAIL_WIDTH - 2 * MLSTM_HEADS, 1)
        o_ref[...] = r.astype(o_ref.dtype)

    if nk == 1:
        finish(_dot_nt(a_ref[...], b_ref[...]) if b_cols_major else _dot(a_ref[...], b_ref[...]))
        return
    acc_ref, = scratch
    k = pl.program_id(2)

    @pl.when(k == 0)
    def _():
        acc_ref[...] = jnp.zeros_like(acc_ref)

    acc_ref[...] += _dot(a_ref[...], b_ref[...])

    @pl.when(k == nk - 1)
    def _():
        finish(acc_ref[...])


def matmul(a, b, layer, *, out_dtype, tm, tn, tk=None, act=None, b_cols_major=False):
    m, kdim = a.shape
    n = b.shape[1] if b_cols_major else b.shape[2]
    tk = kdim if tk is None else tk
    nk = kdim // tk
    assert m % tm == 0 and n % tn == 0 and kdim % tk == 0 and b.shape[2 if b_cols_major else 1] == kdim
    assert nk == 1 or not b_cols_major
    b_spec = (pl.BlockSpec((None, tn, tk), lambda i, j, k: (layer, j, k)) if b_cols_major
              else pl.BlockSpec((None, tk, tn), lambda i, j, k: (layer, k, j)))
    return pl.pallas_call(
        functools.partial(_mm_kernel, nk=nk, act=act, b_cols_major=b_cols_major),
        out_shape=jax.ShapeDtypeStruct((m, n), out_dtype),
        grid=(m // tm, n // tn, nk),
        in_specs=[pl.BlockSpec((tm, tk), lambda i, j, k: (i, k)), b_spec],
        out_specs=pl.BlockSpec((tm, tn), lambda i, j, k: (i, j)),
        scratch_shapes=[] if nk == 1 else [pltpu.VMEM((tm, tn), F32)],
        compiler_params=_cparams(("parallel", "parallel", "arbitrary")),
    )(a, b)


def _mm_cast_kernel(a_ref, b_ref, o_ref, ab_ref, a_bf):
    @pl.when(pl.program_id(1) == 0)
    def _():
        a_bf[...] = a_ref[...].astype(BF16)
        ab_ref[...] = a_bf[...]

    o_ref[...] = _dot_nt(a_bf[...], b_ref[...])


def matmul_cast_a(a, b, layer, *, tm, tn):
    m, kdim = a.shape
    n = b.shape[1]
    assert m % tm == 0 and n % tn == 0 and b.shape[2] == kdim
    a_blk = pl.BlockSpec((tm, kdim), lambda i, j: (i, 0))
    return pl.pallas_call(
        _mm_cast_kernel,
        out_shape=[jax.ShapeDtypeStruct((m, n), F32), jax.ShapeDtypeStruct((m, kdim), BF16)],
        grid=(m // tm, n // tn),
        in_specs=[a_blk, pl.BlockSpec((None, tn, kdim), lambda i, j: (layer, j, 0))],
        out_specs=[pl.BlockSpec((tm, tn), lambda i, j: (i, j)), a_blk],
        scratch_shapes=[pltpu.VMEM((tm, kdim), BF16)],
        compiler_params=_cparams(("parallel", "arbitrary")),
    )(a, b)


def _cast_cols_kernel(x_ref, o_ref, *, col0, ncols_total):
    tc = x_ref.shape[0]
    col = col0 + pl.program_id(0) * tc + lax.broadcasted_iota(jnp.int32, (tc, 1), 0)
    for d in range(x_ref.shape[1]):
        o_ref[d] = jnp.where(col < ncols_total, x_ref[:, d, :], 0.0).astype(o_ref.dtype)


def cast_cols_transposed(w_t, col0, ncols, tc):
    ctot, depth, kdim = w_t.shape
    assert col0 % tc == 0 and ncols % tc == 0
    return pl.pallas_call(
        functools.partial(_cast_cols_kernel, col0=col0, ncols_total=ctot),
        out_shape=jax.ShapeDtypeStruct((depth, ncols, kdim), BF16),
        grid=(ncols // tc,),
        in_specs=[pl.BlockSpec((tc, depth, kdim), lambda i: (col0 // tc + i, 0, 0))],
        out_specs=pl.BlockSpec((depth, tc, kdim), lambda i: (0, i, 0)),
        compiler_params=_cparams(("parallel",)),
    )(w_t)


def _rope_kernel(pos_ref, inv_ref, sign_ref, cos_ref, sin_ref):
    ang = pos_ref[...] * inv_ref[...]
    cos_ref[...] = jnp.cos(ang)
    sin_ref[...] = jnp.sin(ang) * sign_ref[...]


def rope_tables(pos_col, tl):
    l = pos_col.shape[0]
    half = np.arange(0, HEAD_DIM, 2, dtype=np.float32) / np.float32(HEAD_DIM)
    inv = (np.float32(ROPE_BASE) ** (-half)).astype(np.float32)
    inv2 = jnp.asarray(np.concatenate([inv, inv])[None, :])
    sign = jnp.asarray(np.concatenate([-np.ones(64, np.float32), np.ones(64, np.float32)])[None, :])
    row = pl.BlockSpec((1, LANES), lambda i: (0, 0))
    return pl.pallas_call(
        _rope_kernel,
        out_shape=[jax.ShapeDtypeStruct((l, LANES), F32)] * 2,
        grid=(l // tl,),
        in_specs=[pl.BlockSpec((tl, 1), lambda i: (i, 0)), row, row],
        out_specs=[pl.BlockSpec((tl, LANES), lambda i: (i, 0))] * 2,
        compiler_params=_cparams(("parallel",)),
    )(pos_col, inv2, sign)


def _head_norm(y):
    mu = jnp.mean(y, axis=-1, keepdims=True)
    d = y - mu
    var = jnp.mean(d * d, axis=-1, keepdims=True)
    return d * lax.rsqrt(var + LN_EPS)


def _mean_lanes_mxu(x):
    hi = x.astype(BF16)
    lo = (x - hi.astype(F32)).astype(BF16)
    avg = jnp.full((2 * HEAD_DIM, HEAD_DIM), 1.0 / HEAD_DIM, BF16)
    return _dot(jnp.concatenate([hi, lo], axis=1), avg)


def _head_norm_mxu(y):
    d = y - _mean_lanes_mxu(y)
    return d * lax.rsqrt(_mean_lanes_mxu(d * d) + LN_EPS)


def _sigmoid(x):
    return 1.0 / (1.0 + jnp.exp(-x))


def _dot_nt(a, b, precision=None):
    return lax.dot_general(a, b, (((1,), (1,)), ((), ())), precision=precision, preferred_element_type=F32)


def _dot(a, b):
    return jnp.dot(a, b, preferred_element_type=F32)


def _dot_tn(a, b):
    return lax.dot_general(a, b, (((0,), (0,)), ((), ())), preferred_element_type=F32)


def _head_lanes(h):
    return slice(h * HEAD_DIM, (h + 1) * HEAD_DIM)


def _layer_spec(stacked, layer):
    return pl.BlockSpec((None,) + stacked.shape[1:], lambda *_: (layer, 0, 0))


def _ret_log_gamma(h):
    return float(np.log(np.float32(1.0) - np.float32(2.0) ** np.float32(-5.0 - h)))


def _ret_kernel(q_ref, k_ref, v_ref, g_ref, cos_ref, sin_ref, w_ref, o_ref,
                state_ref, decay_ref, qdec_ref, kdec_ref, qb_buf, s_buf, kdt_buf, intra_buf, incr_buf, *, nchunk):
    rb = pl.program_id(0)

    @pl.when(rb == 0)
    def _():
        state_ref[...] = jnp.zeros_like(state_ref)
        ii = lax.broadcasted_iota(jnp.int32, (CHUNK, CHUNK), 0)
        jj = lax.broadcasted_iota(jnp.int32, (CHUNK, CHUNK), 1)
        rel = (ii - jj).astype(F32)
        idx = ii.astype(F32)
        for h in range(RET_HEADS):
            lg = _ret_log_gamma(h)
            decay_ref[h] = jnp.where(rel >= 0.0, jnp.exp(lg * jnp.maximum(rel, 0.0)), 0.0)
            qdec_ref[h] = jnp.exp(lg * (idx + 1.0))
            kdec_ref[h] = jnp.exp(lg * (CHUNK - 1.0 - idx))

    for c in range(nchunk):
        rows = slice(c * CHUNK, (c + 1) * CHUNK)
        cos = cos_ref[rows, :]
        sin = sin_ref[rows, :]
        for h in range(RET_HEADS):
            ln = _head_lanes(h)
            q = q_ref[rows, ln]
            k = k_ref[rows, ln]
            qr = q * cos + pltpu.roll(q, HEAD_DIM // 2, 1) * sin
            kr = (k * cos + pltpu.roll(k, HEAD_DIM // 2, 1) * sin) * QK_SCALE
            qb = qr.astype(BF16)
            qb_buf[rows, ln] = qb
            s_buf[rows, ln] = (_dot_nt(qb, kr.astype(BF16)) * decay_ref[h]).astype(BF16)
            kdt_buf[rows, ln] = jnp.transpose((kr * kdec_ref[h]).astype(BF16))

    for c in range(nchunk):
        rows = slice(c * CHUNK, (c + 1) * CHUNK)
        for h in range(RET_HEADS):
            ln = _head_lanes(h)
            lhs = jnp.concatenate([s_buf[rows, ln], kdt_buf[rows, ln]], axis=0)
            both = _dot(lhs, v_ref[rows, ln].astype(BF16))
            intra_buf[rows, ln] = both[:CHUNK]
            incr_buf[c, h] = both[CHUNK:]

    for c in range(nchunk):
        rows = slice(c * CHUNK, (c + 1) * CHUNK)
        for h in range(RET_HEADS):
            ln = _head_lanes(h)
            st = state_ref[h]
            out = intra_buf[rows, ln] + _dot(qb_buf[rows, ln], st.astype(BF16)) * qdec_ref[h]
            state_ref[h] = st * math.exp(_ret_log_gamma(h) * CHUNK) + incr_buf[c, h]
            g = g_ref[rows, ln]
            y = _head_norm(out) * w_ref[:, ln] * (g * _sigmoid(g))
            o_ref[rows, ln] = y.astype(o_ref.dtype)


def retention_block(proj, cos_t, sin_t, norm_w, layer, tb):
    l = proj.shape[0]

    def col(j):
        return pl.BlockSpec((tb, RET_WIDTH), lambda r, j=j: (r, j))

    tab = pl.BlockSpec((tb, LANES), lambda r: (r, 0))
    hsq = pltpu.VMEM((RET_HEADS, CHUNK, CHUNK), F32)
    half = pltpu.VMEM((tb, RET_WIDTH), BF16)
    return pl.pallas_call(
        functools.partial(_ret_kernel, nchunk=tb // CHUNK),
        out_shape=jax.ShapeDtypeStruct((l, RET_WIDTH), BF16),
        grid=(l // tb,),
        in_specs=[col(0), col(1), col(2), col(3), tab, tab, _layer_spec(norm_w, layer)],
        out_specs=pl.BlockSpec((tb, RET_WIDTH), lambda r: (r, 0)),
        scratch_shapes=[hsq, hsq, hsq, hsq, half, half, half, pltpu.VMEM((tb, RET_WIDTH), F32),
                        pltpu.VMEM((tb // CHUNK, RET_HEADS, HEAD_DIM, HEAD_DIM), F32)],
        compiler_params=_cparams(("arbitrary",)),
    )(proj, proj, proj, proj, cos_t, sin_t, norm_w)


def _log_sigmoid(x):
    return -(jnp.maximum(-x, 0.0) + jnp.log1p(jnp.exp(-jnp.abs(x))))


def _mlstm_kernel(q_ref, k_ref, v_ref, og_ref, gate_ref, cw_ref, cb_ref, gb_ref, nw_ref,
                  o_ref, c_st, m_st, qbuf, kbuf, qs, ks, gs, r_buf, cmax_buf, bcum_buf, mall_buf, floor_buf,
                  p_buf, u_buf, wq_buf, s_buf, kwt_buf, *, nchunk, tb):
    rb = pl.program_id(0)
    pad = SUBLANES
    nh = MLSTM_HEADS

    @pl.when(rb == 0)
    def _():
        c_st[...] = jnp.zeros_like(c_st)
        m_st[...] = jnp.zeros_like(m_st)
        qbuf[0:pad, :] = jnp.zeros((pad, MLSTM_WIDTH), F32)
        kbuf[0:pad, :] = jnp.zeros((pad, MLSTM_WIDTH), F32)

    qbuf[pad:pad + tb, :] = q_ref[...]
    kbuf[pad:pad + tb, :] = k_ref[...]
    for r0 in range(0, tb, CONV_ROWS):
        for buf, woff, dst, scale in ((qbuf, 0, qs, None), (kbuf, MLSTM_WIDTH, ks, QK_SCALE)):
            wl = slice(woff, woff + MLSTM_WIDTH)
            acc = jnp.broadcast_to(cb_ref[:, wl], (CONV_ROWS, MLSTM_WIDTH))
            for tap in range(CONV_WIDTH):
                off = r0 + pad - (CONV_WIDTH - 1) + tap
                acc = acc + buf[off:off + CONV_ROWS, :] * cw_ref[tap:tap + 1, wl]
            act = acc * _sigmoid(acc)
            dst[r0:r0 + CONV_ROWS, :] = act if scale is None else act * scale
    qbuf[0:pad, :] = q_ref[tb - pad:tb, :]
    kbuf[0:pad, :] = k_ref[tb - pad:tb, :]

    lane_row = lax.broadcasted_iota(jnp.int32, (1, LANES), 1)
    graw = pltpu.roll(gate_ref[...] + gb_ref[...], LANES - GATE_LANE0, 1)
    gs[...] = jnp.where(lane_row < nh, graw, jnp.where(lane_row < 2 * nh, _log_sigmoid(graw), 0.0))

    ii = lax.broadcasted_iota(jnp.int32, (CHUNK, CHUNK), 0)
    jj = lax.broadcasted_iota(jnp.int32, (CHUNK, CHUNK), 1)
    causal = ii >= jj
    tril = causal.astype(F32)
    lane8 = lax.broadcasted_iota(jnp.int32, (SUBLANES, LANES), 1)
    ones_b = jnp.ones((CHUNK, HEAD_DIM), BF16)
    full = (CHUNK, CHUNK)

    def chunk_rows(c):
        return slice(c * CHUNK, (c + 1) * CHUNK)

    def aug_lanes(h):
        return slice(h * 2 * HEAD_DIM, (h + 1) * 2 * HEAD_DIM)

    for c in range(nchunk):
        rows = chunk_rows(c)
        g = gs[rows, :]
        gcum = jnp.dot(tril, g, precision=lax.Precision.HIGHEST, preferred_element_type=F32)
        bcum = pltpu.roll(gcum, LANES - nh, 1)
        r8 = jnp.transpose(g - bcum)[0:SUBLANES, :]
        cmax = r8
        for sh in (1, 2, 4, 8, 16, 32, 64):
            cmax = jnp.maximum(cmax, jnp.where(lane8 >= sh, pltpu.roll(cmax, sh, 1), -jnp.inf))
        r_buf[c] = r8
        cmax_buf[rows, :] = jnp.transpose(
            jnp.concatenate([cmax, jnp.zeros((CHUNK - SUBLANES, LANES), F32)], axis=0))
        bcum_buf[rows, :] = bcum

    m_prev = m_st[0:1, :]
    m_prevs, keep_rows = [], []
    for c in range(nchunk):
        rows = chunk_rows(c)
        m_all = jnp.maximum(cmax_buf[rows, :], m_prev)
        floor_buf[rows, :] = jnp.exp(-(bcum_buf[rows, :] + m_all))
        mall_buf[rows, :] = m_all
        m_last = m_all[CHUNK - 1:CHUNK, :]
        m_prevs.append(m_prev)
        keep_rows.append(jnp.exp(m_prev - m_last))
        m_prev = bcum_buf[c * CHUNK + CHUNK - 1:(c + 1) * CHUNK, :] + m_last
    m_st[0:1, :] = m_prev

    for c in range(nchunk):
        rows = chunk_rows(c)
        for h in range(nh):
            ln = _head_lanes(h)
            qc = qs[rows, ln]
            kb = ks[rows, ln].astype(BF16)
            m_bc = jnp.broadcast_to(mall_buf[rows, h:h + 1], full)
            dmat = jnp.exp(jnp.where(causal, r_buf[c, h:h + 1, :] - m_bc, -jnp.inf))
            s_buf[rows, ln] = (_dot_nt(qc.astype(BF16), kb) * dmat).astype(BF16)
            kwt_buf[rows, ln] = (jnp.transpose(kb).astype(F32) * dmat[CHUNK - 1:CHUNK, :]).astype(BF16)
            wq_buf[rows, ln] = (jnp.exp(m_prevs[c][:, h:h + 1] - m_bc) * qc).astype(BF16)

    for c in range(nchunk):
        rows = chunk_rows(c)
        for h in range(nh):
            ln = _head_lanes(h)
            v_aug = jnp.concatenate([v_ref[rows, ln].astype(BF16), ones_b], axis=1)
            both = _dot(jnp.concatenate([s_buf[rows, ln], kwt_buf[rows, ln]], axis=0), v_aug)
            p_buf[rows, aug_lanes(h)] = both[:CHUNK]
            u_buf[c, h] = both[CHUNK:]

    for c in range(nchunk):
        rows = chunk_rows(c)
        for h in range(nh):
            ln = _head_lanes(h)
            cs = c_st[h]
            res = p_buf[rows, aug_lanes(h)] + _dot(wq_buf[rows, ln], cs.astype(BF16))
            c_st[h] = keep_rows[c][:, h:h + 1] * cs + u_buf[c, h]
            den = res[:, HEAD_DIM + h:HEAD_DIM + h + 1]
            inv = 1.0 / jnp.maximum(jnp.abs(den), floor_buf[rows, h:h + 1])
            y = _head_norm(res[:, :HEAD_DIM] * inv) * nw_ref[:, ln] * _sigmoid(og_ref[rows, ln])
            o_ref[rows, ln] = y.astype(o_ref.dtype)


def mlstm_block(proj, tail, conv_w, conv_b, gate_bias, norm_w, layer, tb):
    l = proj.shape[0]
    base = (4 * RET_WIDTH) // MLSTM_WIDTH

    def col(j):
        return pl.BlockSpec((tb, MLSTM_WIDTH), lambda r, j=j: (r, base + j))

    nchunk = tb // CHUNK
    wide = pltpu.VMEM((tb, MLSTM_WIDTH), F32)
    hist = pltpu.VMEM((tb + SUBLANES, MLSTM_WIDTH), F32)
    narrow = pltpu.VMEM((tb, LANES), F32)
    half = pltpu.VMEM((tb, MLSTM_WIDTH), BF16)
    rows8 = pltpu.VMEM((nchunk, SUBLANES, LANES), F32)
    return pl.pallas_call(
        functools.partial(_mlstm_kernel, nchunk=nchunk, tb=tb),
        out_shape=jax.ShapeDtypeStruct((l, MLSTM_WIDTH), BF16),
        grid=(l // tb,),
        in_specs=[col(0), col(1), col(2), col(3),
                  pl.BlockSpec((tb, LANES), lambda r: (r, GATE_COLBLK)),
                  _layer_spec(conv_w, layer), _layer_spec(conv_b, layer), _layer_spec(gate_bias, layer),
                  _layer_spec(norm_w, layer)],
        out_specs=pl.BlockSpec((tb, MLSTM_WIDTH), lambda r: (r, 0)),
        scratch_shapes=[pltpu.VMEM((MLSTM_HEADS, HEAD_DIM, 2 * HEAD_DIM), F32), pltpu.VMEM((SUBLANES, LANES), F32),
                        hist, hist, wide, wide, narrow, rows8, narrow, narrow, narrow, narrow,
                        pltpu.VMEM((tb, 2 * MLSTM_WIDTH), F32),
                        pltpu.VMEM((nchunk, MLSTM_HEADS, HEAD_DIM, 2 * HEAD_DIM), F32),
                        half, half, half],
        compiler_params=_cparams(("arbitrary",)),
    )(proj, proj, proj, proj, tail, conv_w, conv_b, gate_bias, norm_w)


def _s5_prep_kernel(lre_ref, lim_ref, ldt_ref, btre_ref, btim_ref, ctre_ref, ctim_ref,
                    toep_ref, vre_ref, vim_ref, wre_ref, wim_ref, apre_ref, apim_ref):
    lre = lre_ref[...]
    lim = lim_ref[...]
    dt = jnp.exp(ldt_ref[...])
    mag = jnp.exp(lre * dt)
    ang = lim * dt
    zr = mag * jnp.cos(ang) - 1.0
    zi = mag * jnp.sin(ang)
    den = lre * lre + lim * lim
    w_re = (zr * lre + zi * lim) / den
    w_im = (zi * lre - zr * lim) / den
    row_g = lax.broadcasted_iota(jnp.int32, (LANES, S5_SB), 0) // S5_GROUP
    col_g = lax.broadcasted_iota(jnp.int32, (LANES, S5_SB), 1) // S5_STATE
    same = row_g == col_g
    bt_re = btre_ref[...]
    bt_im = btim_ref[...]
    bb_re = jnp.where(same, w_re * bt_re - w_im * bt_im, 0.0)
    bb_im = jnp.where(same, w_re * bt_im + w_im * bt_re, 0.0)
    ct_re = jnp.where(same, ctre_ref[...], 0.0)
    ct_im = jnp.where(same, ctim_ref[...], 0.0)
    hp = lax.Precision.HIGHEST
    kblk = []
    for d in range(S5_T + 1):
        pm = jnp.exp(lre * dt * float(d))
        pa = lim * dt * float(d)
        p_re = pm * jnp.cos(pa)
        p_im = pm * jnp.sin(pa)
        if d < S5_T:
            ab_re = p_re * bb_re - p_im * bb_im
            ab_im = p_re * bb_im + p_im * bb_re
            srow = slice((S5_T - 1 - d) * LANES, (S5_T - d) * LANES)
            vre_ref[0, srow, :] = ab_re.astype(BF16)
            vim_ref[0, srow, :] = ab_im.astype(BF16)
            kblk.append((_dot_nt(ab_re, ct_re, hp) - _dot_nt(ab_im, ct_im, hp)).astype(BF16))
        if d >= 1:
            trow = slice((d - 1) * LANES, d * LANES)
            wre_ref[0, trow, :] = (ct_re * p_re - ct_im * p_im).astype(BF16)
            wim_ref[0, trow, :] = (-(ct_re * p_im + ct_im * p_re)).astype(BF16)
    apre_ref[...] = jnp.zeros_like(apre_ref)
    apim_ref[...] = jnp.zeros_like(apim_ref)
    for k in range(SUBLANES + 1):
        pm = jnp.exp(lre * dt * float(S5_T * k))
        pa = lim * dt * float(S5_T * k)
        apre_ref[k:k + 1, :] = pm * jnp.cos(pa)
        apim_ref[k:k + 1, :] = pm * jnp.sin(pa)
    zero = jnp.zeros((LANES, LANES), BF16)
    for s in range(S5_T):
        for t in range(S5_T):
            toep_ref[0, s * LANES:(s + 1) * LANES, t * LANES:(t + 1) * LANES] = kblk[t - s] if t >= s else zero


def s5_operator_inputs(lam_re, lam_im, log_dt, b_re, b_im, c_re, c_im):
    depth = lam_re.shape[0]
    nst = S5_GROUPS * S5_STATE

    def tiled(m):
        return jnp.tile(m.astype(F32).reshape(depth, S5_WIDTH, S5_STATE), (1, 1, S5_GPB))

    lre = lam_re.astype(F32).reshape(depth, 1, nst)
    lim = lam_im.astype(F32).reshape(depth, 1, nst)
    ldt = jnp.repeat(log_dt.astype(F32), S5_STATE, axis=-1).reshape(depth, 1, nst)
    return (lre, lim, ldt, tiled(jnp.swapaxes(b_re, -1, -2)), tiled(jnp.swapaxes(b_im, -1, -2)),
            tiled(c_re), tiled(c_im))


def s5_operators(prep_inputs, layer):
    nst = S5_GROUPS * S5_STATE
    lane_in = pl.BlockSpec((None, 1, S5_SB), lambda g: (layer, 0, g))
    par_in = pl.BlockSpec((None, LANES, S5_SB), lambda g: (layer, g, 0))
    pow_out = pl.BlockSpec((2 * SUBLANES, S5_SB), lambda g: (0, g))

    def out_blk(rows, cols):
        return pl.BlockSpec((1, rows, cols), lambda g: (g, 0, 0))

    return pl.pallas_call(
        _s5_prep_kernel,
        out_shape=[jax.ShapeDtypeStruct((S5_GB, S5_KT, S5_KT), BF16)]
        + [jax.ShapeDtypeStruct((S5_GB, S5_KT, S5_SB), BF16)] * 4
        + [jax.ShapeDtypeStruct((2 * SUBLANES, nst), F32)] * 2,
        grid=(S5_GB,),
        in_specs=[lane_in, lane_in, lane_in, par_in, par_in, par_in, par_in],
        out_specs=[out_blk(S5_KT, S5_KT)] + [out_blk(S5_KT, S5_SB)] * 4 + [pow_out, pow_out],
        compiler_params=_cparams(("parallel",)),
    )(*prep_inputs)


def _gelu_tanh(x):
    c = math.sqrt(2.0 / math.pi)
    return 0.5 * x * (1.0 + jnp.tanh(c * (x + 0.044715 * (x * x * x))))


def _s5_kernel(u_ref, toep_ref, vre_ref, vim_ref, wre_ref, wim_ref, apre_ref, apim_ref, d_ref, y_ref,
               car_re, car_im, x_re, x_im, *, tmc):
    rb = pl.program_id(1)

    @pl.when(rb == 0)
    def _():
        car_re[...] = jnp.zeros_like(car_re)
        car_im[...] = jnp.zeros_like(car_im)

    us = [u_ref[pl.ds(s, tmc, stride=S5_T), :] for s in range(S5_T)]
    ucat = jnp.concatenate([u.astype(BF16) for u in us], axis=1)
    y_re = _dot(ucat, vre_ref[0])
    y_im = _dot(ucat, vim_ref[0])

    row_in_tile = lax.broadcasted_iota(jnp.int32, (tmc, 1), 0) & (SUBLANES - 1)

    def shifted(v, s):
        return jnp.where(row_in_tile >= s, pltpu.roll(v, s, 0), 0.0)

    for s in (1, 2, 4):
        a_r = apre_ref[s:s + 1, :]
        a_i = apim_ref[s:s + 1, :]
        s_re, s_im = shifted(y_re, s), shifted(y_im, s)
        y_re, y_im = y_re + a_r * s_re - a_i * s_im, y_im + a_r * s_im + a_i * s_re
    e_re, e_im = shifted(y_re, 1), shifted(y_im, 1)
    p_re = apre_ref[0:SUBLANES, :]
    p_im = apim_ref[0:SUBLANES, :]
    a8_r = apre_ref[SUBLANES:SUBLANES + 1, :]
    a8_i = apim_ref[SUBLANES:SUBLANES + 1, :]
    xr = car_re[0:1, :]
    xi = car_im[0:1, :]
    for t in range(tmc // SUBLANES):
        rows = slice(t * SUBLANES, (t + 1) * SUBLANES)
        x_re[rows, :] = e_re[rows] + p_re * xr - p_im * xi
        x_im[rows, :] = e_im[rows] + p_re * xi + p_im * xr
        last = (t + 1) * SUBLANES - 1
        xr, xi = (a8_r * xr - a8_i * xi + y_re[last:last + 1], a8_r * xi + a8_i * xr + y_im[last:last + 1])
    car_re[0:1, :] = xr
    car_im[0:1, :] = xi
    y = (_dot(ucat, toep_ref[0]) + _dot_nt(x_re[...].astype(BF16), wre_ref[0])
         + _dot_nt(x_im[...].astype(BF16), wim_ref[0]))
    for t in range(S5_T):
        yt = y[:, t * LANES:(t + 1) * LANES] + d_ref[...] * us[t]
        y_ref[pl.ds(t, tmc, stride=S5_T), :] = _gelu_tanh(yt)


def s5_block(tail, ops, d_skip, layer, tb):
    l = tail.shape[0]
    toep, v_re, v_im, w_re, w_im, ap_re, ap_im = ops
    tmc = tb // S5_T

    def per_gb(cols):
        return pl.BlockSpec((1, S5_KT, cols), lambda g, r: (g, 0, 0))

    pow_blk = pl.BlockSpec((2 * SUBLANES, S5_SB), lambda g, r: (0, g))
    fold = pltpu.VMEM((tmc, S5_SB), F32)
    carry = pltpu.VMEM((SUBLANES, S5_SB), F32)
    return pl.pallas_call(
        functools.partial(_s5_kernel, tmc=tmc),
        out_shape=jax.ShapeDtypeStruct((l, S5_WIDTH), F32),
        grid=(S5_GB, l // tb),
        in_specs=[pl.BlockSpec((tb, LANES), lambda g, r: (r, g)),
                  per_gb(S5_KT), per_gb(S5_SB), per_gb(S5_SB), per_gb(S5_SB), per_gb(S5_SB),
                  pow_blk, pow_blk, pl.BlockSpec((None, 1, LANES), lambda g, r: (layer, 0, g))],
        out_specs=pl.BlockSpec((tb, LANES), lambda g, r: (r, g)),
        scratch_shapes=[carry, carry, fold, fold],
        compiler_params=_cparams(("parallel", "arbitrary")),
    )(tail, toep, v_re, v_im, w_re, w_im, ap_re, ap_im, d_skip)


def _layer_norm_by_column_tiles(z_tile, z_buf, lw_ref, lb_ref, o_ref, ob_ref):
    rows = z_buf.shape[0]
    pivot = None
    s1 = jnp.zeros((rows, LANES), F32)
    s2 = jnp.zeros((rows, LANES), F32)
    for n in range(D_MODEL // LN_TILE):
        cols = slice(n * LN_TILE, (n + 1) * LN_TILE)
        z = z_tile(cols)
        z_buf[:, cols] = z
        if pivot is None:
            pivot = jnp.mean(z, axis=-1, keepdims=True)
        for j in range(LN_TILE // LANES):
            dz = z[:, j * LANES:(j + 1) * LANES] - pivot
            s1 = s1 + dz
            s2 = s2 + dz * dz
    m1 = jnp.sum(s1, axis=-1, keepdims=True) * (1.0 / D_MODEL)
    var = jnp.sum(s2, axis=-1, keepdims=True) * (1.0 / D_MODEL) - m1 * m1
    mu = pivot + m1
    rstd = lax.rsqrt(var + LN_EPS)
    for n in range(D_MODEL // LN_TILE):
        cols = slice(n * LN_TILE, (n + 1) * LN_TILE)
        out = (z_buf[:, cols] - mu) * rstd * lw_ref[:, cols] + lb_ref[:, cols]
        o_ref[:, cols] = out
        ob_ref[:, cols] = out.astype(BF16)


def _outproj_kernel(a1_ref, a2_ref, ys_ref, gw_ref, gb_ref, w1_ref, w2_ref, w3_ref, x_ref, lw_ref, lb_ref,
                    o_ref, ob_ref, z_buf):
    a1, a2 = a1_ref[...], a2_ref[...]
    ys = ys_ref[...]
    a3 = (ys * _sigmoid(_dot(ys.astype(BF16), gw_ref[...]) + gb_ref[...])).astype(BF16)

    def z_tile(cols):
        mix = _dot(a1, w1_ref[:, cols]) + _dot(a2, w2_ref[:, cols]) + _dot(a3, w3_ref[:, cols])
        return ALPHA * x_ref[:, cols] + mix

    _layer_norm_by_column_tiles(z_tile, z_buf, lw_ref, lb_ref, o_ref, ob_ref)


def outproj_block(y_ret, y_m, y_s, glu_w, glu_b, w_out, x, ln_w, ln_b, layer, tm):
    l = x.shape[0]

    def rows(n):
        return pl.BlockSpec((tm, n), lambda i: (i, 0))

    def wrows(n, blk):
        return pl.BlockSpec((None, n, D_MODEL), lambda i: (layer, blk, 0))

    return pl.pallas_call(
        _outproj_kernel,
        out_shape=[jax.ShapeDtypeStruct((l, D_MODEL), F32), jax.ShapeDtypeStruct((l, D_MODEL), BF16)],
        grid=(l // tm,),
        in_specs=[rows(RET_WIDTH), rows(MLSTM_WIDTH), rows(S5_WIDTH),
                  _layer_spec(glu_w, layer), _layer_spec(glu_b, layer), wrows(RET_WIDTH, 0), wrows(MLSTM_WIDTH, 1), wrows(S5_WIDTH, (RET_WIDTH + MLSTM_WIDTH) // S5_WIDTH),
                  rows(D_MODEL), _layer_spec(ln_w, layer), _layer_spec(ln_b, layer)],
        out_specs=[rows(D_MODEL), rows(D_MODEL)],
        scratch_shapes=[pltpu.VMEM((tm, D_MODEL), F32)],
        compiler_params=_cparams(("parallel",)),
    )(y_ret, y_m, y_s, glu_w, glu_b, w_out, w_out, w_out, x, ln_w, ln_b)


def _down_kernel(h_ref, w_ref, x1_ref, r_ref, rb_ref):
    r = ALPHA * x1_ref[...] + _dot(h_ref[...], w_ref[...])
    r_ref[...] = r
    rb_ref[...] = r.astype(BF16)


def down_block(hid, w_down, x1, layer, tm):
    l, ff = hid.shape
    rows = pl.BlockSpec((tm, D_MODEL), lambda i: (i, 0))
    return pl.pallas_call(
        _down_kernel,
        out_shape=[jax.ShapeDtypeStruct((l, D_MODEL), F32), jax.ShapeDtypeStruct((l, D_MODEL), BF16)],
        grid=(l // tm,),
        in_specs=[pl.BlockSpec((tm, ff), lambda i: (i, 0)),
                  pl.BlockSpec((None, ff, D_MODEL), lambda i: (layer, 0, 0), pipeline_mode=pl.Buffered(1)), rows],
        out_specs=[rows, rows],
        compiler_params=_cparams(("parallel",)),
    )(hid, w_down, x1)


def _final_kernel(r_ref, rb_ref, p_ref, wg_ref, wp_ref, lw_ref, lb_ref, o_ref, ob_ref, z_buf):
    rb, pb = rb_ref[...], p_ref[...]

    def z_tile(cols):
        gate = _sigmoid(_dot(rb, wg_ref[:, cols]))
        return r_ref[:, cols] + gate * _dot(pb, wp_ref[:, cols])

    _layer_norm_by_column_tiles(z_tile, z_buf, lw_ref, lb_ref, o_ref, ob_ref)


def final_block(r, rb, p_b, w_gate, w_ple, ln_w, ln_b, layer, tm):
    l = r.shape[0]
    rows = pl.BlockSpec((tm, D_MODEL), lambda i: (i, 0))
    return pl.pallas_call(
        _final_kernel,
        out_shape=[jax.ShapeDtypeStruct((l, D_MODEL), F32), jax.ShapeDtypeStruct((l, D_MODEL), BF16)],
        grid=(l // tm,),
        in_specs=[rows, rows, pl.BlockSpec((None, tm, PLE_DIM), lambda i: (layer, i, 0)),
                  _layer_spec(w_gate, layer), _layer_spec(w_ple, layer), _layer_spec(ln_w, layer),
                  _layer_spec(ln_b, layer)],
        out_specs=[rows, rows],
        scratch_shapes=[pltpu.VMEM((tm, D_MODEL), F32)],
        compiler_params=_cparams(("parallel",)),
    )(r, rb, p_b, w_gate, w_ple, ln_w, ln_b)


def _rows3(v):
    return v.astype(F32).reshape(v.shape[0], 1, -1)


def _tiles(l):
    return {"mm_m": min(1024, l), "mix": min(512, l), "s5": min(2048, l), "row": min(512, l), "down": min(256, l), "in_n": 1536, "up_n": 2048,
            "rope": min(2048, l)}


def kernel(x, p, positions, w_in, mlstm_conv_w, mlstm_conv_b, mlstm_i_bias, mlstm_f_bias, ret_norm_w, mlstm_norm_w, s5_lambda_re, s5_lambda_im, s5_log_dt, s5_B_re, s5_B_im, s5_C_re, s5_C_im, s5_D, s5_glu_w, s5_glu_b, w_out, ln1_w, ln1_b, w_up, w_down, w_gate, w_ple, ln2_w, ln2_b):
    bsz, l, _ = x.shape
    depth = w_in.shape[0]
    assert bsz == 1
    t = _tiles(l)
    cos_t, sin_t = rope_tables(positions.astype(F32).reshape(l, 1), t["rope"])
    xf = x.reshape(l, D_MODEL).astype(F32)
    xb = None
    p_b = p.reshape(depth, l, PLE_DIM).astype(BF16)

    w_in_t = jnp.transpose(w_in, (2, 0, 1))
    w_in_b = cast_cols_transposed(w_in_t, 0, MAIN_WIDTH, LANES)
    w_tail_b = cast_cols_transposed(w_in_t, MAIN_WIDTH, TAIL_WIDTH, LANES)
    w_out_b, w_up_b, w_down_b = w_out.astype(BF16), w_up.astype(BF16), w_down.astype(BF16)
    w_gate_b, w_ple_b, glu_w_b = w_gate.astype(BF16), w_ple.astype(BF16), s5_glu_w.astype(BF16)
    gate_bias = jnp.concatenate([jnp.zeros((depth, GATE_LANE0), F32), mlstm_i_bias.astype(F32),
                                 mlstm_f_bias.astype(F32)], axis=1).reshape(depth, 1, LANES)
    conv_w = mlstm_conv_w.astype(F32)
    conv_b, ret_nw, mlstm_nw = _rows3(mlstm_conv_b), _rows3(ret_norm_w), _rows3(mlstm_norm_w)
    s5_d, glu_b = _rows3(s5_D), _rows3(s5_glu_b)
    ln1w, ln1b, ln2w, ln2b = _rows3(ln1_w), _rows3(ln1_b), _rows3(ln2_w), _rows3(ln2_b)
    s5_in = s5_operator_inputs(s5_lambda_re, s5_lambda_im, s5_log_dt, s5_B_re, s5_B_im, s5_C_re, s5_C_im)

    for i in range(depth):
        if xb is None:
            proj, xb = matmul_cast_a(xf, w_in_b, i, tm=t["mm_m"], tn=t["in_n"])
        else:
            proj = matmul(xb, w_in_b, i, out_dtype=F32, tm=t["mm_m"], tn=t["in_n"], b_cols_major=True)
        tail = matmul(xb, w_tail_b, i, out_dtype=F32, tm=t["mm_m"], tn=TAIL_WIDTH, act="rotate_tail",
                      b_cols_major=True)
        y_ret = retention_block(proj, cos_t, sin_t, ret_nw, i, t["mix"])
        y_m = mlstm_block(proj, tail, conv_w, conv_b, gate_bias, mlstm_nw, i, t["mix"])
        y_s = s5_block(tail, s5_operators(s5_in, i), s5_d, i, t["s5"])
        x1, x1b = outproj_block(y_ret, y_m, y_s, glu_w_b, glu_b, w_out_b, xf, ln1w, ln1b, i, t["row"])
        hid = matmul(x1b, w_up_b, i, out_dtype=BF16, tm=t["mm_m"], tn=t["up_n"], act="relu2")
        r, rb = down_block(hid, w_down_b, x1, i, t["down"])
        xf, xb = final_block(r, rb, p_b, w_gate_b, w_ple_b, ln2w, ln2b, i, t["row"])
    return xf.reshape(bsz, l, D_MODEL)
```

```python
import functools
import math

import jax
import jax.numpy as jnp
import numpy as np
from jax import lax
from jax.experimental import pallas as pl
from jax.experimental.pallas import tpu as pltpu

F32 = jnp.float32
BF16 = jnp.bfloat16

D_MODEL = 2048
DEPTH = 2
HEAD_DIM = 128
RET_HEADS = 6
MLSTM_HEADS = 6
RET_WIDTH = RET_HEADS * HEAD_DIM
MLSTM_WIDTH = MLSTM_HEADS * HEAD_DIM
S5_WIDTH = D_MODEL - RET_WIDTH - MLSTM_WIDTH
S5_GROUP = 16
S5_GROUPS = S5_WIDTH // S5_GROUP
S5_STATE = 64
CONV_WIDTH = 4
CHUNK = 128
D_FF = 4 * D_MODEL
PLE_DIM = 256
ROPE_BASE = 10000.0
LN_EPS = 1e-5
ALPHA = (2 * DEPTH) ** 0.25
QK_SCALE = HEAD_DIM ** -0.5

LANES = 128
SUBLANES = 8
MAIN_WIDTH = 4 * RET_WIDTH + 4 * MLSTM_WIDTH
TAIL_COLS = 2 * MLSTM_HEADS + S5_WIDTH
TAIL_WIDTH = 5 * LANES
GATE_LANE0 = LANES - 2 * MLSTM_HEADS
GATE_COLBLK = S5_WIDTH // LANES
ROT_COL_TILE = 0
ROT_GROUP = 256
LN_TILE = 256
CONV_ROWS = 64
S5_T = 16
S5_GB = S5_WIDTH // LANES
S5_GPB = LANES // S5_GROUP
S5_SB = S5_GPB * S5_STATE
S5_KT = S5_T * LANES
VMEM_LIMIT = 56 * 1024 * 1024


def _cparams(sem):
    return pltpu.CompilerParams(dimension_semantics=sem, vmem_limit_bytes=VMEM_LIMIT)


def _mm_kernel(a_ref, b_ref, o_ref, *scratch, nk, act, b_cols_major):
    def finish(r):
        if act == "relu2":
            r = jnp.square(jnp.maximum(r, 0.0))
        elif act == "rotate_tail":
            r = pltpu.roll(r, TAIL_WIDTH - 2 * MLSTM_HEADS, 1)
        o_ref[...] = r.astype(o_ref.dtype)

    if nk == 1:
        finish(_dot_nt(a_ref[...], b_ref[...]) if b_cols_major else _dot(a_ref[...], b_ref[...]))
        return
    acc_ref, = scratch
    k = pl.program_id(2)

    @pl.when(k == 0)
    def _():
        acc_ref[...] = jnp.zeros_like(acc_ref)

    acc_ref[...] += _dot(a_ref[...], b_ref[...])

    @pl.when(k == nk - 1)
    def _():
        finish(acc_ref[...])


def matmul(a, b, layer, *, out_dtype, tm, tn, tk=None, act=None, b_cols_major=False):
    m, kdim = a.shape
    n = b.shape[1] if b_cols_major else b.shape[2]
    tk = kdim if tk is None else tk
    nk = kdim // tk
    assert m % tm == 0 and n % tn == 0 and kdim % tk == 0 and b.shape[2 if b_cols_major else 1] == kdim
    assert nk == 1 or not b_cols_major
    b_spec = (pl.BlockSpec((None, tn, tk), lambda i, j, k: (layer, j, k)) if b_cols_major
              else pl.BlockSpec((None, tk, tn), lambda i, j, k: (layer, k, j)))
    return pl.pallas_call(
        functools.partial(_mm_kernel, nk=nk, act=act, b_cols_major=b_cols_major),
        out_shape=jax.ShapeDtypeStruct((m, n), out_dtype),
        grid=(m // tm, n // tn, nk),
        in_specs=[pl.BlockSpec((tm, tk), lambda i, j, k: (i, k)), b_spec],
        out_specs=pl.BlockSpec((tm, tn), lambda i, j, k: (i, j)),
        scratch_shapes=[] if nk == 1 else [pltpu.VMEM((tm, tn), F32)],
        compiler_params=_cparams(("parallel", "parallel", "arbitrary")),
    )(a, b)


def _in_proj_kernel(a_ref, b_ref, cos_ref, sin_ref, *rest, cast_a):
    j = pl.program_id(1)
    if cast_a:
        o_ref, ab_ref = rest

        @pl.when(j == 0)
        def _():
            ab_ref[...] = a_ref[...].astype(BF16)
        a_src = ab_ref
    else:
        o_ref, = rest
        a_src = a_ref

    @pl.when(j != ROT_COL_TILE)
    def _():
        o_ref[...] = _dot_nt(a_src[...], b_ref[...])

    @pl.when(j == ROT_COL_TILE)
    def _():
        a = a_src[...]
        cos = cos_ref[...]
        sin = sin_ref[...]
        for g in range(o_ref.shape[1] // ROT_GROUP):
            z = _dot_nt(a, b_ref[g * ROT_GROUP:(g + 1) * ROT_GROUP, :])
            for h in range(ROT_GROUP // HEAD_DIM):
                c0 = g * ROT_GROUP + h * HEAD_DIM
                t = z[:, h * HEAD_DIM:(h + 1) * HEAD_DIM]
                rot = t * cos + pltpu.roll(t, HEAD_DIM // 2, 1) * sin
                o_ref[:, c0:c0 + HEAD_DIM] = rot * QK_SCALE if c0 >= RET_WIDTH else rot


def in_projection(a, b, cos_t, sin_t, layer, *, tm, tn):
    m, kdim = a.shape
    n = b.shape[1]
    cast_a = a.dtype != BF16
    assert m % tm == 0 and n % tn == 0 and b.shape[2] == kdim and tn == 2 * RET_WIDTH
    a_blk = pl.BlockSpec((tm, kdim), lambda i, j: (i, 0))
    tab = pl.BlockSpec((tm, LANES), lambda i, j: (i, 0))
    out_shape = [jax.ShapeDtypeStruct((m, n), F32)]
    out_specs = [pl.BlockSpec((tm, tn), lambda i, j: (i, j))]
    if cast_a:
        out_shape.append(jax.ShapeDtypeStruct((m, kdim), BF16))
        out_specs.append(a_blk)
    out = pl.pallas_call(
        functools.partial(_in_proj_kernel, cast_a=cast_a),
        out_shape=out_shape,
        grid=(m // tm, n // tn),
        in_specs=[a_blk, pl.BlockSpec((None, tn, kdim), lambda i, j: (layer, j, 0)), tab, tab],
        out_specs=out_specs,
        compiler_params=_cparams(("parallel", "arbitrary")),
    )(a, b, cos_t, sin_t)
    return (out[0], out[1]) if cast_a else (out[0], a)


def _cast_cols_kernel(x_ref, o_ref, *, col0, ncols_total):
    tc = x_ref.shape[0]
    col = col0 + pl.program_id(0) * tc + lax.broadcasted_iota(jnp.int32, (tc, 1), 0)
    for d in range(x_ref.shape[1]):
        o_ref[d] = jnp.where(col < ncols_total, x_ref[:, d, :], 0.0).astype(o_ref.dtype)


def cast_cols_transposed(w_t, col0, ncols, tc):
    ctot, depth, kdim = w_t.shape
    assert col0 % tc == 0 and ncols % tc == 0
    return pl.pallas_call(
        functools.partial(_cast_cols_kernel, col0=col0, ncols_total=ctot),
        out_shape=jax.ShapeDtypeStruct((depth, ncols, kdim), BF16),
        grid=(ncols // tc,),
        in_specs=[pl.BlockSpec((tc, depth, kdim), lambda i: (col0 // tc + i, 0, 0))],
        out_specs=pl.BlockSpec((depth, tc, kdim), lambda i: (0, i, 0)),
        compiler_params=_cparams(("parallel",)),
    )(w_t)


def _rope_kernel(pos_ref, inv_ref, sign_ref, cos_ref, sin_ref):
    ang = pos_ref[...] * inv_ref[...]
    cos_ref[...] = jnp.cos(ang)
    sin_ref[...] = jnp.sin(ang) * sign_ref[...]


def rope_tables(pos_col, tl):
    l = pos_col.shape[0]
    half = np.arange(0, HEAD_DIM, 2, dtype=np.float32) / np.float32(HEAD_DIM)
    inv = (np.float32(ROPE_BASE) ** (-half)).astype(np.float32)
    inv2 = jnp.asarray(np.concatenate([inv, inv])[None, :])
    sign = jnp.asarray(np.concatenate([-np.ones(64, np.float32), np.ones(64, np.float32)])[None, :])
    row = pl.BlockSpec((1, LANES), lambda i: (0, 0))
    return pl.pallas_call(
        _rope_kernel,
        out_shape=[jax.ShapeDtypeStruct((l, LANES), F32)] * 2,
        grid=(l // tl,),
        in_specs=[pl.BlockSpec((tl, 1), lambda i: (i, 0)), row, row],
        out_specs=[pl.BlockSpec((tl, LANES), lambda i: (i, 0))] * 2,
        compiler_params=_cparams(("parallel",)),
    )(pos_col, inv2, sign)


def _head_norm(y):
    mu = jnp.mean(y, axis=-1, keepdims=True)
    d = y - mu
    var = jnp.mean(d * d, axis=-1, keepdims=True)
    return d * lax.rsqrt(var + LN_EPS)


def _mean_lanes_mxu(x):
    hi = x.astype(BF16)
    lo = (x - hi.astype(F32)).astype(BF16)
    avg = jnp.full((2 * HEAD_DIM, HEAD_DIM), 1.0 / HEAD_DIM, BF16)
    return _dot(jnp.concatenate([hi, lo], axis=1), avg)


def _head_norm_mxu(y):
    d = y - _mean_lanes_mxu(y)
    return d * lax.rsqrt(_mean_lanes_mxu(d * d) + LN_EPS)


def _sigmoid(x):
    return 1.0 / (1.0 + jnp.exp(-x))


def _dot_nt(a, b, precision=None):
    return lax.dot_general(a, b, (((1,), (1,)), ((), ())), precision=precision, preferred_element_type=F32)


def _dot(a, b):
    return jnp.dot(a, b, preferred_element_type=F32)


def _dot_tn(a, b):
    return lax.dot_general(a, b, (((0,), (0,)), ((), ())), preferred_element_type=F32)


def _head_lanes(h):
    return slice(h * HEAD_DIM, (h + 1) * HEAD_DIM)


def _layer_spec(stacked, layer):
    return pl.BlockSpec((None,) + stacked.shape[1:], lambda *_: (layer, 0, 0))


def _ret_log_gamma(h):
    return float(np.log(np.float32(1.0) - np.float32(2.0) ** np.float32(-5.0 - h)))


def _ret_kernel(q_ref, k_ref, v_ref, g_ref, w_ref, o_ref,
                state_ref, decay_ref, qdec_ref, kdec_ref, qb_buf, s_buf, kdt_buf, intra_buf, incr_buf, *, nchunk):
    rb = pl.program_id(0)

    @pl.when(rb == 0)
    def _():
        state_ref[...] = jnp.zeros_like(state_ref)
        ii = lax.broadcasted_iota(jnp.int32, (CHUNK, CHUNK), 0)
        jj = lax.broadcasted_iota(jnp.int32, (CHUNK, CHUNK), 1)
        rel = (ii - jj).astype(F32)
        idx = ii.astype(F32)
        for h in range(RET_HEADS):
            lg = _ret_log_gamma(h)
            decay_ref[h] = jnp.where(rel >= 0.0, jnp.exp(lg * jnp.maximum(rel, 0.0)), 0.0)
            qdec_ref[h] = jnp.exp(lg * (idx + 1.0))
            kdec_ref[h] = jnp.exp(lg * (CHUNK - 1.0 - idx))

    for c in range(nchunk):
        rows = slice(c * CHUNK, (c + 1) * CHUNK)
        for h in range(RET_HEADS):
            ln = _head_lanes(h)
            kr = k_ref[rows, ln]
            qb = q_ref[rows, ln].astype(BF16)
            qb_buf[rows, ln] = qb
            s_buf[rows, ln] = (_dot_nt(qb, kr.astype(BF16)) * decay_ref[h]).astype(BF16)
            kdt_buf[rows, ln] = jnp.transpose((kr * kdec_ref[h]).astype(BF16))

    for c in range(nchunk):
        rows = slice(c * CHUNK, (c + 1) * CHUNK)
        for h in range(RET_HEADS):
            ln = _head_lanes(h)
            lhs = jnp.concatenate([s_buf[rows, ln], kdt_buf[rows, ln]], axis=0)
            both = _dot(lhs, v_ref[rows, ln].astype(BF16))
            intra_buf[rows, ln] = both[:CHUNK]
            incr_buf[c, h] = both[CHUNK:]

    for c in range(nchunk):
        rows = slice(c * CHUNK, (c + 1) * CHUNK)
        for h in range(RET_HEADS):
            ln = _head_lanes(h)
            st = state_ref[h]
            out = intra_buf[rows, ln] + _dot(qb_buf[rows, ln], st.astype(BF16)) * qdec_ref[h]
            state_ref[h] = st * math.exp(_ret_log_gamma(h) * CHUNK) + incr_buf[c, h]
            g = g_ref[rows, ln]
            y = _head_norm(out) * w_ref[:, ln] * (g * _sigmoid(g))
            o_ref[rows, ln] = y.astype(o_ref.dtype)


def retention_block(proj, norm_w, layer, tb):
    l = proj.shape[0]

    def col(j):
        return pl.BlockSpec((tb, RET_WIDTH), lambda r, j=j: (r, j))

    hsq = pltpu.VMEM((RET_HEADS, CHUNK, CHUNK), F32)
    half = pltpu.VMEM((tb, RET_WIDTH), BF16)
    return pl.pallas_call(
        functools.partial(_ret_kernel, nchunk=tb // CHUNK),
        out_shape=jax.ShapeDtypeStruct((l, RET_WIDTH), BF16),
        grid=(l // tb,),
        in_specs=[col(0), col(1), col(2), col(3), _layer_spec(norm_w, layer)],
        out_specs=pl.BlockSpec((tb, RET_WIDTH), lambda r: (r, 0)),
        scratch_shapes=[hsq, hsq, hsq, hsq, half, half, half, pltpu.VMEM((tb, RET_WIDTH), F32),
                        pltpu.VMEM((tb // CHUNK, RET_HEADS, HEAD_DIM, HEAD_DIM), F32)],
        compiler_params=_cparams(("arbitrary",)),
    )(proj, proj, proj, proj, norm_w)


def _log_sigmoid(x):
    return -(jnp.maximum(-x, 0.0) + jnp.log1p(jnp.exp(-jnp.abs(x))))


def _mlstm_kernel(q_ref, k_ref, v_ref, og_ref, gate_ref, cw_ref, cb_ref, gb_ref, nw_ref,
                  o_ref, c_st, m_st, qbuf, kbuf, qs, ks, gs, r_buf, cmax_buf, bcum_buf, mall_buf, floor_buf,
                  p_buf, u_buf, wq_buf, s_buf, kwt_buf, *, nchunk, tb):
    rb = pl.program_id(0)
    pad = SUBLANES
    nh = MLSTM_HEADS

    @pl.when(rb == 0)
    def _():
        c_st[...] = jnp.zeros_like(c_st)
        m_st[...] = jnp.zeros_like(m_st)
        qbuf[0:pad, :] = jnp.zeros((pad, MLSTM_WIDTH), F32)
        kbuf[0:pad, :] = jnp.zeros((pad, MLSTM_WIDTH), F32)

    qbuf[pad:pad + tb, :] = q_ref[...]
    kbuf[pad:pad + tb, :] = k_ref[...]
    for r0 in range(0, tb, CONV_ROWS):
        for buf, woff, dst, scale in ((qbuf, 0, qs, None), (kbuf, MLSTM_WIDTH, ks, QK_SCALE)):
            wl = slice(woff, woff + MLSTM_WIDTH)
            acc = jnp.broadcast_to(cb_ref[:, wl], (CONV_ROWS, MLSTM_WIDTH))
            for tap in range(CONV_WIDTH):
                off = r0 + pad - (CONV_WIDTH - 1) + tap
                acc = acc + buf[off:off + CONV_ROWS, :] * cw_ref[tap:tap + 1, wl]
            act = acc * _sigmoid(acc)
            dst[r0:r0 + CONV_ROWS, :] = act if scale is None else act * scale
    qbuf[0:pad, :] = q_ref[tb - pad:tb, :]
    kbuf[0:pad, :] = k_ref[tb - pad:tb, :]

    lane_row = lax.broadcasted_iota(jnp.int32, (1, LANES), 1)
    graw = pltpu.roll(gate_ref[...] + gb_ref[...], LANES - GATE_LANE0, 1)
    gs[...] = jnp.where(lane_row < nh, graw, jnp.where(lane_row < 2 * nh, _log_sigmoid(graw), 0.0))

    ii = lax.broadcasted_iota(jnp.int32, (CHUNK, CHUNK), 0)
    jj = lax.broadcasted_iota(jnp.int32, (CHUNK, CHUNK), 1)
    causal = ii >= jj
    tril = causal.astype(F32)
    lane8 = lax.broadcasted_iota(jnp.int32, (SUBLANES, LANES), 1)
    ones_b = jnp.ones((CHUNK, HEAD_DIM), BF16)
    full = (CHUNK, CHUNK)

    def chunk_rows(c):
        return slice(c * CHUNK, (c + 1) * CHUNK)

    def aug_lanes(h):
        return slice(h * 2 * HEAD_DIM, (h + 1) * 2 * HEAD_DIM)

    for c in range(nchunk):
        rows = chunk_rows(c)
        g = gs[rows, :]
        gcum = jnp.dot(tril, g, precision=lax.Precision.HIGHEST, preferred_element_type=F32)
        bcum = pltpu.roll(gcum, LANES - nh, 1)
        r8 = jnp.transpose(g - bcum)[0:SUBLANES, :]
        cmax = r8
        for sh in (1, 2, 4, 8, 16, 32, 64):
            cmax = jnp.maximum(cmax, jnp.where(lane8 >= sh, pltpu.roll(cmax, sh, 1), -jnp.inf))
        r_buf[c] = r8
        cmax_buf[rows, :] = jnp.transpose(
            jnp.concatenate([cmax, jnp.zeros((CHUNK - SUBLANES, LANES), F32)], axis=0))
        bcum_buf[rows, :] = bcum

    m_prev = m_st[0:1, :]
    m_prevs, keep_rows = [], []
    for c in range(nchunk):
        rows = chunk_rows(c)
        m_all = jnp.maximum(cmax_buf[rows, :], m_prev)
        floor_buf[rows, :] = jnp.exp(-(bcum_buf[rows, :] + m_all))
        mall_buf[rows, :] = m_all
        m_last = m_all[CHUNK - 1:CHUNK, :]
        m_prevs.append(m_prev)
        keep_rows.append(jnp.exp(m_prev - m_last))
        m_prev = bcum_buf[c * CHUNK + CHUNK - 1:(c + 1) * CHUNK, :] + m_last
    m_st[0:1, :] = m_prev

    for c in range(nchunk):
        rows = chunk_rows(c)
        for h in range(nh):
            ln = _head_lanes(h)
            qc = qs[rows, ln]
            kb = ks[rows, ln].astype(BF16)
            m_bc = jnp.broadcast_to(mall_buf[rows, h:h + 1], full)
            dmat = jnp.exp(jnp.where(causal, r_buf[c, h:h + 1, :] - m_bc, -jnp.inf))
            s_buf[rows, ln] = (_dot_nt(qc.astype(BF16), kb) * dmat).astype(BF16)
            kwt_buf[rows, ln] = (jnp.transpose(kb).astype(F32) * dmat[CHUNK - 1:CHUNK, :]).astype(BF16)
            wq_buf[rows, ln] = (jnp.exp(m_prevs[c][:, h:h + 1] - m_bc) * qc).astype(BF16)

    for c in range(nchunk):
        rows = chunk_rows(c)
        for h in range(nh):
            ln = _head_lanes(h)
            v_aug = jnp.concatenate([v_ref[rows, ln].astype(BF16), ones_b], axis=1)
            both = _dot(jnp.concatenate([s_buf[rows, ln], kwt_buf[rows, ln]], axis=0), v_aug)
            p_buf[rows, aug_lanes(h)] = both[:CHUNK]
            u_buf[c, h] = both[CHUNK:]

    for c in range(nchunk):
        rows = chunk_rows(c)
        for h in range(nh):
            ln = _head_lanes(h)
            cs = c_st[h]
            res = p_buf[rows, aug_lanes(h)] + _dot(wq_buf[rows, ln], cs.astype(BF16))
            c_st[h] = keep_rows[c][:, h:h + 1] * cs + u_buf[c, h]
            den = res[:, HEAD_DIM + h:HEAD_DIM + h + 1]
            inv = 1.0 / jnp.maximum(jnp.abs(den), floor_buf[rows, h:h + 1])
            y = _head_norm(res[:, :HEAD_DIM] * inv) * nw_ref[:, ln] * _sigmoid(og_ref[rows, ln])
            o_ref[rows, ln] = y.astype(o_ref.dtype)


def mlstm_block(proj, tail, conv_w, conv_b, gate_bias, norm_w, layer, tb):
    l = proj.shape[0]
    base = (4 * RET_WIDTH) // MLSTM_WIDTH

    def col(j):
        return pl.BlockSpec((tb, MLSTM_WIDTH), lambda r, j=j: (r, base + j))

    nchunk = tb // CHUNK
    wide = pltpu.VMEM((tb, MLSTM_WIDTH), F32)
    hist = pltpu.VMEM((tb + SUBLANES, MLSTM_WIDTH), F32)
    narrow = pltpu.VMEM((tb, LANES), F32)
    half = pltpu.VMEM((tb, MLSTM_WIDTH), BF16)
    rows8 = pltpu.VMEM((nchunk, SUBLANES, LANES), F32)
    return pl.pallas_call(
        functools.partial(_mlstm_kernel, nchunk=nchunk, tb=tb),
        out_shape=jax.ShapeDtypeStruct((l, MLSTM_WIDTH), BF16),
        grid=(l // tb,),
        in_specs=[col(0), col(1), col(2), col(3),
                  pl.BlockSpec((tb, LANES), lambda r: (r, GATE_COLBLK)),
                  _layer_spec(conv_w, layer), _layer_spec(conv_b, layer), _layer_spec(gate_bias, layer),
                  _layer_spec(norm_w, layer)],
        out_specs=pl.BlockSpec((tb, MLSTM_WIDTH), lambda r: (r, 0)),
        scratch_shapes=[pltpu.VMEM((MLSTM_HEADS, HEAD_DIM, 2 * HEAD_DIM), F32), pltpu.VMEM((SUBLANES, LANES), F32),
                        hist, hist, wide, wide, narrow, rows8, narrow, narrow, narrow, narrow,
                        pltpu.VMEM((tb, 2 * MLSTM_WIDTH), F32),
                        pltpu.VMEM((nchunk, MLSTM_HEADS, HEAD_DIM, 2 * HEAD_DIM), F32),
                        half, half, half],
        compiler_params=_cparams(("arbitrary",)),
    )(proj, proj, proj, proj, tail, conv_w, conv_b, gate_bias, norm_w)


def _s5_prep_kernel(lre_ref, lim_ref, ldt_ref, btre_ref, btim_ref, ctre_ref, ctim_ref,
                    toep_ref, vre_ref, vim_ref, wre_ref, wim_ref, apre_ref, apim_ref):
    lre = lre_ref[...]
    lim = lim_ref[...]
    dt = jnp.exp(ldt_ref[...])
    mag = jnp.exp(lre * dt)
    ang = lim * dt
    zr = mag * jnp.cos(ang) - 1.0
    zi = mag * jnp.sin(ang)
    den = lre * lre + lim * lim
    w_re = (zr * lre + zi * lim) / den
    w_im = (zi * lre - zr * lim) / den
    row_g = lax.broadcasted_iota(jnp.int32, (LANES, S5_SB), 0) // S5_GROUP
    col_g = lax.broadcasted_iota(jnp.int32, (LANES, S5_SB), 1) // S5_STATE
    same = row_g == col_g
    bt_re = btre_ref[...]
    bt_im = btim_ref[...]
    bb_re = jnp.where(same, w_re * bt_re - w_im * bt_im, 0.0)
    bb_im = jnp.where(same, w_re * bt_im + w_im * bt_re, 0.0)
    ct_re = jnp.where(same, ctre_ref[...], 0.0)
    ct_im = jnp.where(same, ctim_ref[...], 0.0)
    hp = lax.Precision.HIGHEST
    kblk = []
    for d in range(S5_T + 1):
        pm = jnp.exp(lre * dt * float(d))
        pa = lim * dt * float(d)
        p_re = pm * jnp.cos(pa)
        p_im = pm * jnp.sin(pa)
        if d < S5_T:
            ab_re = p_re * bb_re - p_im * bb_im
            ab_im = p_re * bb_im + p_im * bb_re
            srow = slice((S5_T - 1 - d) * LANES, (S5_T - d) * LANES)
            vre_ref[0, srow, :] = ab_re.astype(BF16)
            vim_ref[0, srow, :] = ab_im.astype(BF16)
            kblk.append((_dot_nt(ab_re, ct_re, hp) - _dot_nt(ab_im, ct_im, hp)).astype(BF16))
        if d >= 1:
            trow = slice((d - 1) * LANES, d * LANES)
            wre_ref[0, trow, :] = (ct_re * p_re - ct_im * p_im).astype(BF16)
            wim_ref[0, trow, :] = (-(ct_re * p_im + ct_im * p_re)).astype(BF16)
    apre_ref[...] = jnp.zeros_like(apre_ref)
    apim_ref[...] = jnp.zeros_like(apim_ref)
    for k in range(SUBLANES + 1):
        pm = jnp.exp(lre * dt * float(S5_T * k))
        pa = lim * dt * float(S5_T * k)
        apre_ref[k:k + 1, :] = pm * jnp.cos(pa)
        apim_ref[k:k + 1, :] = pm * jnp.sin(pa)
    zero = jnp.zeros((LANES, LANES), BF16)
    for s in range(S5_T):
        for t in range(S5_T):
            toep_ref[0, s * LANES:(s + 1) * LANES, t * LANES:(t + 1) * LANES] = kblk[t - s] if t >= s else zero


def s5_operator_inputs(lam_re, lam_im, log_dt, b_re, b_im, c_re, c_im):
    depth = lam_re.shape[0]
    nst = S5_GROUPS * S5_STATE

    def tiled(m):
        return jnp.tile(m.astype(F32).reshape(depth, S5_WIDTH, S5_STATE), (1, 1, S5_GPB))

    lre = lam_re.astype(F32).reshape(depth, 1, nst)
    lim = lam_im.astype(F32).reshape(depth, 1, nst)
    ldt = jnp.repeat(log_dt.astype(F32), S5_STATE, axis=-1).reshape(depth, 1, nst)
    return (lre, lim, ldt, tiled(jnp.swapaxes(b_re, -1, -2)), tiled(jnp.swapaxes(b_im, -1, -2)),
            tiled(c_re), tiled(c_im))


def s5_operators(prep_inputs, layer):
    nst = S5_GROUPS * S5_STATE
    lane_in = pl.BlockSpec((None, 1, S5_SB), lambda g: (layer, 0, g))
    par_in = pl.BlockSpec((None, LANES, S5_SB), lambda g: (layer, g, 0))
    pow_out = pl.BlockSpec((2 * SUBLANES, S5_SB), lambda g: (0, g))

    def out_blk(rows, cols):
        return pl.BlockSpec((1, rows, cols), lambda g: (g, 0, 0))

    return pl.pallas_call(
        _s5_prep_kernel,
        out_shape=[jax.ShapeDtypeStruct((S5_GB, S5_KT, S5_KT), BF16)]
        + [jax.ShapeDtypeStruct((S5_GB, S5_KT, S5_SB), BF16)] * 4
        + [jax.ShapeDtypeStruct((2 * SUBLANES, nst), F32)] * 2,
        grid=(S5_GB,),
        in_specs=[lane_in, lane_in, lane_in, par_in, par_in, par_in, par_in],
        out_specs=[out_blk(S5_KT, S5_KT)] + [out_blk(S5_KT, S5_SB)] * 4 + [pow_out, pow_out],
        compiler_params=_cparams(("parallel",)),
    )(*prep_inputs)


def _gelu_tanh(x):
    c = math.sqrt(2.0 / math.pi)
    return 0.5 * x * (1.0 + jnp.tanh(c * (x + 0.044715 * (x * x * x))))


def _s5_kernel(u_ref, toep_ref, vre_ref, vim_ref, wre_ref, wim_ref, apre_ref, apim_ref, d_ref, y_ref,
               car_re, car_im, x_re, x_im, *, tmc):
    rb = pl.program_id(1)

    @pl.when(rb == 0)
    def _():
        car_re[...] = jnp.zeros_like(car_re)
        car_im[...] = jnp.zeros_like(car_im)

    us = [u_ref[pl.ds(s, tmc, stride=S5_T), :] for s in range(S5_T)]
    ucat = jnp.concatenate([u.astype(BF16) for u in us], axis=1)
    y_re = _dot(ucat, vre_ref[0])
    y_im = _dot(ucat, vim_ref[0])

    row_in_tile = lax.broadcasted_iota(jnp.int32, (tmc, 1), 0) & (SUBLANES - 1)

    def shifted(v, s):
        return jnp.where(row_in_tile >= s, pltpu.roll(v, s, 0), 0.0)

    for s in (1, 2, 4):
        a_r = apre_ref[s:s + 1, :]
        a_i = apim_ref[s:s + 1, :]
        s_re, s_im = shifted(y_re, s), shifted(y_im, s)
        y_re, y_im = y_re + a_r * s_re - a_i * s_im, y_im + a_r * s_im + a_i * s_re
    e_re, e_im = shifted(y_re, 1), shifted(y_im, 1)
    p_re = apre_ref[0:SUBLANES, :]
    p_im = apim_ref[0:SUBLANES, :]
    a8_r = apre_ref[SUBLANES:SUBLANES + 1, :]
    a8_i = apim_ref[SUBLANES:SUBLANES + 1, :]
    xr = car_re[0:1, :]
    xi = car_im[0:1, :]
    for t in range(tmc // SUBLANES):
        rows = slice(t * SUBLANES, (t + 1) * SUBLANES)
        x_re[rows, :] = e_re[rows] + p_re * xr - p_im * xi
        x_im[rows, :] = e_im[rows] + p_re * xi + p_im * xr
        last = (t + 1) * SUBLANES - 1
        xr, xi = (a8_r * xr - a8_i * xi + y_re[last:last + 1], a8_r * xi + a8_i * xr + y_im[last:last + 1])
    car_re[0:1, :] = xr
    car_im[0:1, :] = xi
    y = (_dot(ucat, toep_ref[0]) + _dot_nt(x_re[...].astype(BF16), wre_ref[0])
         + _dot_nt(x_im[...].astype(BF16), wim_ref[0]))
    for t in range(S5_T):
        yt = y[:, t * LANES:(t + 1) * LANES] + d_ref[...] * us[t]
        y_ref[pl.ds(t, tmc, stride=S5_T), :] = _gelu_tanh(yt)


def s5_block(tail, ops, d_skip, layer, tb):
    l = tail.shape[0]
    toep, v_re, v_im, w_re, w_im, ap_re, ap_im = ops
    tmc = tb // S5_T

    def per_gb(cols):
        return pl.BlockSpec((1, S5_KT, cols), lambda g, r: (g, 0, 0))

    pow_blk = pl.BlockSpec((2 * SUBLANES, S5_SB), lambda g, r: (0, g))
    fold = pltpu.VMEM((tmc, S5_SB), F32)
    carry = pltpu.VMEM((SUBLANES, S5_SB), F32)
    return pl.pallas_call(
        functools.partial(_s5_kernel, tmc=tmc),
        out_shape=jax.ShapeDtypeStruct((l, S5_WIDTH), F32),
        grid=(S5_GB, l // tb),
        in_specs=[pl.BlockSpec((tb, LANES), lambda g, r: (r, g)),
                  per_gb(S5_KT), per_gb(S5_SB), per_gb(S5_SB), per_gb(S5_SB), per_gb(S5_SB),
                  pow_blk, pow_blk, pl.BlockSpec((None, 1, LANES), lambda g, r: (layer, 0, g))],
        out_specs=pl.BlockSpec((tb, LANES), lambda g, r: (r, g)),
        scratch_shapes=[carry, carry, fold, fold],
        compiler_params=_cparams(("parallel", "arbitrary")),
    )(tail, toep, v_re, v_im, w_re, w_im, ap_re, ap_im, d_skip)


def _layer_norm_by_column_tiles(z_tile, z_buf, lw_ref, lb_ref, o_ref, ob_ref):
    rows = z_buf.shape[0]
    pivot = None
    s1 = jnp.zeros((rows, LANES), F32)
    s2 = jnp.zeros((rows, LANES), F32)
    for n in range(D_MODEL // LN_TILE):
        cols = slice(n * LN_TILE, (n + 1) * LN_TILE)
        z = z_tile(cols)
        z_buf[:, cols] = z
        if pivot is None:
            pivot = jnp.mean(z, axis=-1, keepdims=True)
        for j in range(LN_TILE // LANES):
            dz = z[:, j * LANES:(j + 1) * LANES] - pivot
            s1 = s1 + dz
            s2 = s2 + dz * dz
    m1 = jnp.sum(s1, axis=-1, keepdims=True) * (1.0 / D_MODEL)
    var = jnp.sum(s2, axis=-1, keepdims=True) * (1.0 / D_MODEL) - m1 * m1
    mu = pivot + m1
    rstd = lax.rsqrt(var + LN_EPS)
    for n in range(D_MODEL // LN_TILE):
        cols = slice(n * LN_TILE, (n + 1) * LN_TILE)
        out = (z_buf[:, cols] - mu) * rstd * lw_ref[:, cols] + lb_ref[:, cols]
        o_ref[:, cols] = out
        ob_ref[:, cols] = out.astype(BF16)


def _outproj_kernel(a1_ref, a2_ref, ys_ref, gw_ref, gb_ref, w1_ref, w2_ref, w3_ref, x_ref, lw_ref, lb_ref,
                    o_ref, ob_ref, z_buf):
    a1, a2 = a1_ref[...], a2_ref[...]
    ys = ys_ref[...]
    a3 = (ys * _sigmoid(_dot(ys.astype(BF16), gw_ref[...]) + gb_ref[...])).astype(BF16)

    def z_tile(cols):
        mix = _dot(a1, w1_ref[:, cols]) + _dot(a2, w2_ref[:, cols]) + _dot(a3, w3_ref[:, cols])
        return ALPHA * x_ref[:, cols] + mix

    _layer_norm_by_column_tiles(z_tile, z_buf, lw_ref, lb_ref, o_ref, ob_ref)


def outproj_block(y_ret, y_m, y_s, glu_w, glu_b, w_out, x, ln_w, ln_b, layer, tm):
    l = x.shape[0]

    def rows(n):
        return pl.BlockSpec((tm, n), lambda i: (i, 0))

    def wrows(n, blk):
        return pl.BlockSpec((None, n, D_MODEL), lambda i: (layer, blk, 0))

    return pl.pallas_call(
        _outproj_kernel,
        out_shape=[jax.ShapeDtypeStruct((l, D_MODEL), F32), jax.ShapeDtypeStruct((l, D_MODEL), BF16)],
        grid=(l // tm,),
        in_specs=[rows(RET_WIDTH), rows(MLSTM_WIDTH), rows(S5_WIDTH),
                  _layer_spec(glu_w, layer), _layer_spec(glu_b, layer), wrows(RET_WIDTH, 0), wrows(MLSTM_WIDTH, 1), wrows(S5_WIDTH, (RET_WIDTH + MLSTM_WIDTH) // S5_WIDTH),
                  rows(D_MODEL), _layer_spec(ln_w, layer), _layer_spec(ln_b, layer)],
        out_specs=[rows(D_MODEL), rows(D_MODEL)],
        scratch_shapes=[pltpu.VMEM((tm, D_MODEL), F32)],
        compiler_params=_cparams(("parallel",)),
    )(y_ret, y_m, y_s, glu_w, glu_b, w_out, w_out, w_out, x, ln_w, ln_b)


def _down_kernel(h_ref, w_ref, x1_ref, r_ref, rb_ref):
    r = ALPHA * x1_ref[...] + _dot(h_ref[...], w_ref[...])
    r_ref[...] = r
    rb_ref[...] = r.astype(BF16)


def down_block(hid, w_down, x1, layer, tm):
    l, ff = hid.shape
    rows = pl.BlockSpec((tm, D_MODEL), lambda i: (i, 0))
    return pl.pallas_call(
        _down_kernel,
        out_shape=[jax.ShapeDtypeStruct((l, D_MODEL), F32), jax.ShapeDtypeStruct((l, D_MODEL), BF16)],
        grid=(l // tm,),
        in_specs=[pl.BlockSpec((tm, ff), lambda i: (i, 0)),
                  pl.BlockSpec((None, ff, D_MODEL), lambda i: (layer, 0, 0), pipeline_mode=pl.Buffered(1)), rows],
        out_specs=[rows, rows],
        compiler_params=_cparams(("parallel",)),
    )(hid, w_down, x1)


def _final_kernel(r_ref, rb_ref, p_ref, wg_ref, wp_ref, lw_ref, lb_ref, o_ref, ob_ref, z_buf):
    rb, pb = rb_ref[...], p_ref[...]

    def z_tile(cols):
        gate = _sigmoid(_dot(rb, wg_ref[:, cols]))
        return r_ref[:, cols] + gate * _dot(pb, wp_ref[:, cols])

    _layer_norm_by_column_tiles(z_tile, z_buf, lw_ref, lb_ref, o_ref, ob_ref)


def final_block(r, rb, p_b, w_gate, w_ple, ln_w, ln_b, layer, tm):
    l = r.shape[0]
    rows = pl.BlockSpec((tm, D_MODEL), lambda i: (i, 0))
    return pl.pallas_call(
        _final_kernel,
        out_shape=[jax.ShapeDtypeStruct((l, D_MODEL), F32), jax.ShapeDtypeStruct((l, D_MODEL), BF16)],
        grid=(l // tm,),
        in_specs=[rows, rows, pl.BlockSpec((None, tm, PLE_DIM), lambda i: (layer, i, 0)),
                  _layer_spec(w_gate, layer), _layer_spec(w_ple, layer), _layer_spec(ln_w, layer),
                  _layer_spec(ln_b, layer)],
        out_specs=[rows, rows],
        scratch_shapes=[pltpu.VMEM((tm, D_MODEL), F32)],
        compiler_params=_cparams(("parallel",)),
    )(r, rb, p_b, w_gate, w_ple, ln_w, ln_b)


def _rows3(v):
    return v.astype(F32).reshape(v.shape[0], 1, -1)


def _tiles(l):
    return {"mm_m": min(1024, l), "mix": min(512, l), "s5": min(2048, l), "row": min(512, l), "down": min(256, l), "in_n": 1536, "up_n": 2048,
            "rope": min(2048, l)}


def kernel(x, p, positions, w_in, mlstm_conv_w, mlstm_conv_b, mlstm_i_bias, mlstm_f_bias, ret_norm_w, mlstm_norm_w, s5_lambda_re, s5_lambda_im, s5_log_dt, s5_B_re, s5_B_im, s5_C_re, s5_C_im, s5_D, s5_glu_w, s5_glu_b, w_out, ln1_w, ln1_b, w_up, w_down, w_gate, w_ple, ln2_w, ln2_b):
    bsz, l, _ = x.shape
    depth = w_in.shape[0]
    assert bsz == 1
    t = _tiles(l)
    cos_t, sin_t = rope_tables(positions.astype(F32).reshape(l, 1), t["rope"])
    xf = x.reshape(l, D_MODEL).astype(F32)
    xb = None
    p_b = p.reshape(depth, l, PLE_DIM).astype(BF16)

    w_in_t = jnp.transpose(w_in, (2, 0, 1))
    w_in_b = cast_cols_transposed(w_in_t, 0, MAIN_WIDTH, LANES)
    w_tail_b = cast_cols_transposed(w_in_t, MAIN_WIDTH, TAIL_WIDTH, LANES)
    w_out_b, w_up_b, w_down_b = w_out.astype(BF16), w_up.astype(BF16), w_down.astype(BF16)
    w_gate_b, w_ple_b, glu_w_b = w_gate.astype(BF16), w_ple.astype(BF16), s5_glu_w.astype(BF16)
    gate_bias = jnp.concatenate([jnp.zeros((depth, GATE_LANE0), F32), mlstm_i_bias.astype(F32),
                                 mlstm_f_bias.astype(F32)], axis=1).reshape(depth, 1, LANES)
    conv_w = mlstm_conv_w.astype(F32)
    conv_b, ret_nw, mlstm_nw = _rows3(mlstm_conv_b), _rows3(ret_norm_w), _rows3(mlstm_norm_w)
    s5_d, glu_b = _rows3(s5_D), _rows3(s5_glu_b)
    ln1w, ln1b, ln2w, ln2b = _rows3(ln1_w), _rows3(ln1_b), _rows3(ln2_w), _rows3(ln2_b)
    s5_in = s5_operator_inputs(s5_lambda_re, s5_lambda_im, s5_log_dt, s5_B_re, s5_B_im, s5_C_re, s5_C_im)

    for i in range(depth):
        proj, xb = in_projection(xf if xb is None else xb, w_in_b, cos_t, sin_t, i, tm=t["mm_m"], tn=t["in_n"])
        tail = matmul(xb, w_tail_b, i, out_dtype=F32, tm=t["mm_m"], tn=TAIL_WIDTH, act="rotate_tail",
                      b_cols_major=True)
        y_ret = retention_block(proj, ret_nw, i, t["mix"])
        y_m = mlstm_block(proj, tail, conv_w, conv_b, gate_bias, mlstm_nw, i, t["mix"])
        y_s = s5_block(tail, s5_operators(s5_in, i), s5_d, i, t["s5"])
        x1, x1b = outproj_block(y_ret, y_m, y_s, glu_w_b, glu_b, w_out_b, xf, ln1w, ln1b, i, t["row"])
        hid = matmul(x1b, w_up_b, i, out_dtype=BF16, tm=t["mm_m"], tn=t["up_n"], act="relu2")
        r, rb = down_block(hid, w_down_b, x1, i, t["down"])
        xf, xb = final_block(r, rb, p_b, w_gate_b, w_ple_b, ln2w, ln2b, i, t["row"])
    return xf.reshape(bsz, l, D_MODEL)
```

```python
import functools
import math

import jax
import jax.numpy as jnp
import numpy as np
from jax import lax
from jax.experimental import pallas as pl
from jax.experimental.pallas import tpu as pltpu

F32 = jnp.float32
BF16 = jnp.bfloat16

D_MODEL = 2048
DEPTH = 2
HEAD_DIM = 128
RET_HEADS = 6
MLSTM_HEADS = 6
RET_WIDTH = RET_HEADS * HEAD_DIM
MLSTM_WIDTH = MLSTM_HEADS * HEAD_DIM
S5_WIDTH = D_MODEL - RET_WIDTH - MLSTM_WIDTH
S5_GROUP = 16
S5_GROUPS = S5_WIDTH // S5_GROUP
S5_STATE = 64
CONV_WIDTH = 4
CHUNK = 128
PLE_DIM = 256
ROPE_BASE = 10000.0
LN_EPS = 1e-5
ALPHA = (2 * DEPTH) ** 0.25
QK_SCALE = HEAD_DIM ** -0.5

LANES = 128
SUBLANES = 8
MAIN_WIDTH = 4 * RET_WIDTH + 4 * MLSTM_WIDTH
TAIL_COLS = 2 * MLSTM_HEADS + S5_WIDTH
TAIL_WIDTH = 5 * LANES
GATE_LANE0 = LANES - 2 * MLSTM_HEADS
GATE_COLBLK = S5_WIDTH // LANES
ROT_COL_TILE = 0
ROT_GROUP = 256
LN_TILE = 256
CONV_ROWS = 64
S5_T = 16
S5_GB = S5_WIDTH // LANES
S5_GPB = LANES // S5_GROUP
S5_SB = S5_GPB * S5_STATE
S5_KT = S5_T * LANES
VMEM_LIMIT = 56 * 1024 * 1024


def _cparams(sem):
    return pltpu.CompilerParams(dimension_semantics=sem, vmem_limit_bytes=VMEM_LIMIT)


def _mm_kernel(a_ref, b_ref, o_ref, *scratch, nk, act, b_cols_major):
    def finish(r):
        if act == "relu2":
            r = jnp.square(jnp.maximum(r, 0.0))
        elif act == "rotate_tail":
            r = pltpu.roll(r, TAIL_WIDTH - 2 * MLSTM_HEADS, 1)
        o_ref[...] = r.astype(o_ref.dtype)

    if nk == 1:
        finish(_dot_nt(a_ref[...], b_ref[...]) if b_cols_major else _dot(a_ref[...], b_ref[...]))
        return
    acc_ref, = scratch
    k = pl.program_id(2)

    @pl.when(k == 0)
    def _():
        acc_ref[...] = jnp.zeros_like(acc_ref)

    acc_ref[...] += _dot(a_ref[...], b_ref[...])

    @pl.when(k == nk - 1)
    def _():
        finish(acc_ref[...])


def matmul(a, b, layer, *, out_dtype, tm, tn, tk=None, act=None, b_cols_major=False):
    m, kdim = a.shape
    n = b.shape[1] if b_cols_major else b.shape[2]
    tk = kdim if tk is None else tk
    nk = kdim // tk
    assert m % tm == 0 and n % tn == 0 and kdim % tk == 0 and b.shape[2 if b_cols_major else 1] == kdim
    assert nk == 1 or not b_cols_major
    b_spec = (pl.BlockSpec((None, tn, tk), lambda i, j, k: (layer, j, k)) if b_cols_major
              else pl.BlockSpec((None, tk, tn), lambda i, j, k: (layer, k, j)))
    return pl.pallas_call(
        functools.partial(_mm_kernel, nk=nk, act=act, b_cols_major=b_cols_major),
        out_shape=jax.ShapeDtypeStruct((m, n), out_dtype),
        grid=(m // tm, n // tn, nk),
        in_specs=[pl.BlockSpec((tm, tk), lambda i, j, k: (i, k)), b_spec],
        out_specs=pl.BlockSpec((tm, tn), lambda i, j, k: (i, j)),
        scratch_shapes=[] if nk == 1 else [pltpu.VMEM((tm, tn), F32)],
        compiler_params=_cparams(("parallel", "parallel", "arbitrary")),
    )(a, b)


def _in_proj_kernel(a_ref, b_ref, cos_ref, sin_ref, *rest, cast_a):
    j = pl.program_id(1)
    if cast_a:
        o_ref, ab_ref = rest

        @pl.when(j == 0)
        def _():
            ab_ref[...] = a_ref[...].astype(BF16)
        a_src = ab_ref
    else:
        o_ref, = rest
        a_src = a_ref

    @pl.when(j != ROT_COL_TILE)
    def _():
        o_ref[...] = _dot_nt(a_src[...], b_ref[...])

    @pl.when(j == ROT_COL_TILE)
    def _():
        a = a_src[...]
        cos = cos_ref[...]
        sin = sin_ref[...]
        for g in range(o_ref.shape[1] // ROT_GROUP):
            z = _dot_nt(a, b_ref[g * ROT_GROUP:(g + 1) * ROT_GROUP, :])
            for h in range(ROT_GROUP // HEAD_DIM):
                c0 = g * ROT_GROUP + h * HEAD_DIM
                t = z[:, h * HEAD_DIM:(h + 1) * HEAD_DIM]
                rot = t * cos + pltpu.roll(t, HEAD_DIM // 2, 1) * sin
                o_ref[:, c0:c0 + HEAD_DIM] = rot * QK_SCALE if c0 >= RET_WIDTH else rot


def in_projection(a, b, cos_t, sin_t, layer, *, tm, tn):
    m, kdim = a.shape
    n = b.shape[1]
    cast_a = a.dtype != BF16
    assert m % tm == 0 and n % tn == 0 and b.shape[2] == kdim and tn == 2 * RET_WIDTH
    a_blk = pl.BlockSpec((tm, kdim), lambda i, j: (i, 0))
    tab = pl.BlockSpec((tm, LANES), lambda i, j: (i, 0))
    out_shape = [jax.ShapeDtypeStruct((m, n), F32)]
    out_specs = [pl.BlockSpec((tm, tn), lambda i, j: (i, j))]
    if cast_a:
        out_shape.append(jax.ShapeDtypeStruct((m, kdim), BF16))
        out_specs.append(a_blk)
    out = pl.pallas_call(
        functools.partial(_in_proj_kernel, cast_a=cast_a),
        out_shape=out_shape,
        grid=(m // tm, n // tn),
        in_specs=[a_blk, pl.BlockSpec((None, tn, kdim), lambda i, j: (layer, j, 0)), tab, tab],
        out_specs=out_specs,
        compiler_params=_cparams(("parallel", "arbitrary")),
    )(a, b, cos_t, sin_t)
    return (out[0], out[1]) if cast_a else (out[0], a)


def _cast_cols_kernel(x_ref, o_ref, *, col0, ncols_total):
    tc = x_ref.shape[0]
    col = col0 + pl.program_id(0) * tc + lax.broadcasted_iota(jnp.int32, (tc, 1), 0)
    for d in range(x_ref.shape[1]):
        o_ref[d] = jnp.where(col < ncols_total, x_ref[:, d, :], 0.0).astype(o_ref.dtype)


def cast_cols_transposed(w_t, col0, ncols, tc):
    ctot, depth, kdim = w_t.shape
    assert col0 % tc == 0 and ncols % tc == 0
    return pl.pallas_call(
        functools.partial(_cast_cols_kernel, col0=col0, ncols_total=ctot),
        out_shape=jax.ShapeDtypeStruct((depth, ncols, kdim), BF16),
        grid=(ncols // tc,),
        in_specs=[pl.BlockSpec((tc, depth, kdim), lambda i: (col0 // tc + i, 0, 0))],
        out_specs=pl.BlockSpec((depth, tc, kdim), lambda i: (0, i, 0)),
        compiler_params=_cparams(("parallel",)),
    )(w_t)


def _rope_kernel(pos_ref, inv_ref, sign_ref, cos_ref, sin_ref):
    ang = pos_ref[...] * inv_ref[...]
    cos_ref[...] = jnp.cos(ang)
    sin_ref[...] = jnp.sin(ang) * sign_ref[...]


def rope_tables(pos_col, tl):
    l = pos_col.shape[0]
    half = np.arange(0, HEAD_DIM, 2, dtype=np.float32) / np.float32(HEAD_DIM)
    inv = (np.float32(ROPE_BASE) ** (-half)).astype(np.float32)
    inv2 = jnp.asarray(np.concatenate([inv, inv])[None, :])
    sign = jnp.asarray(np.concatenate([-np.ones(64, np.float32), np.ones(64, np.float32)])[None, :])
    row = pl.BlockSpec((1, LANES), lambda i: (0, 0))
    return pl.pallas_call(
        _rope_kernel,
        out_shape=[jax.ShapeDtypeStruct((l, LANES), F32)] * 2,
        grid=(l // tl,),
        in_specs=[pl.BlockSpec((tl, 1), lambda i: (i, 0)), row, row],
        out_specs=[pl.BlockSpec((tl, LANES), lambda i: (i, 0))] * 2,
        compiler_params=_cparams(("parallel",)),
    )(pos_col, inv2, sign)


def _head_norm(y):
    mu = jnp.mean(y, axis=-1, keepdims=True)
    d = y - mu
    var = jnp.mean(d * d, axis=-1, keepdims=True)
    return d * lax.rsqrt(var + LN_EPS)


def _sigmoid(x):
    return 1.0 / (1.0 + jnp.exp(-x))


def _dot_nt(a, b, precision=None):
    return lax.dot_general(a, b, (((1,), (1,)), ((), ())), precision=precision, preferred_element_type=F32)


def _dot(a, b):
    return jnp.dot(a, b, preferred_element_type=F32)


def _head_lanes(h):
    return slice(h * HEAD_DIM, (h + 1) * HEAD_DIM)


def _layer_spec(stacked, layer):
    return pl.BlockSpec((None,) + stacked.shape[1:], lambda *_: (layer, 0, 0))


def _ret_log_gamma(h):
    return float(np.log(np.float32(1.0) - np.float32(2.0) ** np.float32(-5.0 - h)))


def _ret_kernel(q_ref, k_ref, v_ref, g_ref, w_ref, o_ref,
                state_ref, decay_ref, qdec_ref, kdec_ref, qb_buf, s_buf, kdt_buf, intra_buf, incr_buf, *, nchunk):
    rb = pl.program_id(0)

    @pl.when(rb == 0)
    def _():
        state_ref[...] = jnp.zeros_like(state_ref)
        ii = lax.broadcasted_iota(jnp.int32, (CHUNK, CHUNK), 0)
        jj = lax.broadcasted_iota(jnp.int32, (CHUNK, CHUNK), 1)
        rel = (ii - jj).astype(F32)
        idx = ii.astype(F32)
        for h in range(RET_HEADS):
            lg = _ret_log_gamma(h)
            decay_ref[h] = jnp.where(rel >= 0.0, jnp.exp(lg * jnp.maximum(rel, 0.0)), 0.0)
            qdec_ref[h] = jnp.exp(lg * (idx + 1.0))
            kdec_ref[h] = jnp.exp(lg * (CHUNK - 1.0 - idx))

    for c in range(nchunk):
        rows = slice(c * CHUNK, (c + 1) * CHUNK)
        for h in range(RET_HEADS):
            ln = _head_lanes(h)
            kr = k_ref[rows, ln]
            qb = q_ref[rows, ln].astype(BF16)
            qb_buf[rows, ln] = qb
            s_buf[rows, ln] = (_dot_nt(qb, kr.astype(BF16)) * decay_ref[h]).astype(BF16)
            kdt_buf[rows, ln] = jnp.transpose((kr * kdec_ref[h]).astype(BF16))

    for c in range(nchunk):
        rows = slice(c * CHUNK, (c + 1) * CHUNK)
        for h in range(RET_HEADS):
            ln = _head_lanes(h)
            lhs = jnp.concatenate([s_buf[rows, ln], kdt_buf[rows, ln]], axis=0)
            both = _dot(lhs, v_ref[rows, ln].astype(BF16))
            intra_buf[rows, ln] = both[:CHUNK]
            incr_buf[c, h] = both[CHUNK:]

    for c in range(nchunk):
        rows = slice(c * CHUNK, (c + 1) * CHUNK)
        for h in range(RET_HEADS):
            ln = _head_lanes(h)
            st = state_ref[h]
            out = intra_buf[rows, ln] + _dot(qb_buf[rows, ln], st.astype(BF16)) * qdec_ref[h]
            state_ref[h] = st * math.exp(_ret_log_gamma(h) * CHUNK) + incr_buf[c, h]
            g = g_ref[rows, ln]
            y = _head_norm(out) * w_ref[:, ln] * (g * _sigmoid(g))
            o_ref[rows, ln] = y.astype(o_ref.dtype)


def retention_block(proj, norm_w, layer, tb):
    l = proj.shape[0]

    def col(j):
        return pl.BlockSpec((tb, RET_WIDTH), lambda r, j=j: (r, j))

    hsq = pltpu.VMEM((RET_HEADS, CHUNK, CHUNK), F32)
    half = pltpu.VMEM((tb, RET_WIDTH), BF16)
    return pl.pallas_call(
        functools.partial(_ret_kernel, nchunk=tb // CHUNK),
        out_shape=jax.ShapeDtypeStruct((l, RET_WIDTH), BF16),
        grid=(l // tb,),
        in_specs=[col(0), col(1), col(2), col(3), _layer_spec(norm_w, layer)],
        out_specs=pl.BlockSpec((tb, RET_WIDTH), lambda r: (r, 0)),
        scratch_shapes=[hsq, hsq, hsq, hsq, half, half, half, pltpu.VMEM((tb, RET_WIDTH), F32),
                        pltpu.VMEM((tb // CHUNK, RET_HEADS, HEAD_DIM, HEAD_DIM), F32)],
        compiler_params=_cparams(("arbitrary",)),
    )(proj, proj, proj, proj, norm_w)


def _log_sigmoid(x):
    return -(jnp.maximum(-x, 0.0) + jnp.log1p(jnp.exp(-jnp.abs(x))))


def _mlstm_kernel(q_ref, k_ref, v_ref, og_ref, gate_ref, cw_ref, cb_ref, gb_ref, nw_ref,
                  o_ref, c_st, m_st, qbuf, kbuf, qs, ks, gs, r_buf, cmax_buf, bcum_buf, mall_buf, floor_buf,
                  p_buf, u_buf, wq_buf, s_buf, kwt_buf, *, nchunk, tb):
    rb = pl.program_id(0)
    pad = SUBLANES
    nh = MLSTM_HEADS

    @pl.when(rb == 0)
    def _():
        c_st[...] = jnp.zeros_like(c_st)
        m_st[...] = jnp.zeros_like(m_st)
        qbuf[0:pad, :] = jnp.zeros((pad, MLSTM_WIDTH), F32)
        kbuf[0:pad, :] = jnp.zeros((pad, MLSTM_WIDTH), F32)

    qbuf[pad:pad + tb, :] = q_ref[...]
    kbuf[pad:pad + tb, :] = k_ref[...]
    for r0 in range(0, tb, CONV_ROWS):
        for buf, woff, dst, scale in ((qbuf, 0, qs, None), (kbuf, MLSTM_WIDTH, ks, QK_SCALE)):
            wl = slice(woff, woff + MLSTM_WIDTH)
            acc = jnp.broadcast_to(cb_ref[:, wl], (CONV_ROWS, MLSTM_WIDTH))
            for tap in range(CONV_WIDTH):
                off = r0 + pad - (CONV_WIDTH - 1) + tap
                acc = acc + buf[off:off + CONV_ROWS, :] * cw_ref[tap:tap + 1, wl]
            act = acc * _sigmoid(acc)
            dst[r0:r0 + CONV_ROWS, :] = act if scale is None else act * scale
    qbuf[0:pad, :] = q_ref[tb - pad:tb, :]
    kbuf[0:pad, :] = k_ref[tb - pad:tb, :]

    lane_row = lax.broadcasted_iota(jnp.int32, (1, LANES), 1)
    graw = pltpu.roll(gate_ref[...] + gb_ref[...], LANES - GATE_LANE0, 1)
    gs[...] = jnp.where(lane_row < nh, graw, jnp.where(lane_row < 2 * nh, _log_sigmoid(graw), 0.0))

    ii = lax.broadcasted_iota(jnp.int32, (CHUNK, CHUNK), 0)
    jj = lax.broadcasted_iota(jnp.int32, (CHUNK, CHUNK), 1)
    causal = ii >= jj
    tril = causal.astype(F32)
    lane8 = lax.broadcasted_iota(jnp.int32, (SUBLANES, LANES), 1)
    ones_b = jnp.ones((CHUNK, HEAD_DIM), BF16)
    full = (CHUNK, CHUNK)

    def chunk_rows(c):
        return slice(c * CHUNK, (c + 1) * CHUNK)

    def aug_lanes(h):
        return slice(h * 2 * HEAD_DIM, (h + 1) * 2 * HEAD_DIM)

    for c in range(nchunk):
        rows = chunk_rows(c)
        g = gs[rows, :]
        gcum = jnp.dot(tril, g, precision=lax.Precision.HIGHEST, preferred_element_type=F32)
        bcum = pltpu.roll(gcum, LANES - nh, 1)
        r8 = jnp.transpose(g - bcum)[0:SUBLANES, :]
        cmax = r8
        for sh in (1, 2, 4, 8, 16, 32, 64):
            cmax = jnp.maximum(cmax, jnp.where(lane8 >= sh, pltpu.roll(cmax, sh, 1), -jnp.inf))
        r_buf[c] = r8
        cmax_buf[rows, :] = jnp.transpose(
            jnp.concatenate([cmax, jnp.zeros((CHUNK - SUBLANES, LANES), F32)], axis=0))
        bcum_buf[rows, :] = bcum

    m_prev = m_st[0:1, :]
    m_prevs, keep_rows = [], []
    for c in range(nchunk):
        rows = chunk_rows(c)
        m_all = jnp.maximum(cmax_buf[rows, :], m_prev)
        floor_buf[rows, :] = jnp.exp(-(bcum_buf[rows, :] + m_all))
        mall_buf[rows, :] = m_all
        m_last = m_all[CHUNK - 1:CHUNK, :]
        m_prevs.append(m_prev)
        keep_rows.append(jnp.exp(m_prev - m_last))
        m_prev = bcum_buf[c * CHUNK + CHUNK - 1:(c + 1) * CHUNK, :] + m_last
    m_st[0:1, :] = m_prev

    for c in range(nchunk):
        rows = chunk_rows(c)
        for h in range(nh):
            ln = _head_lanes(h)
            qc = qs[rows, ln]
            kb = ks[rows, ln].astype(BF16)
            m_bc = jnp.broadcast_to(mall_buf[rows, h:h + 1], full)
            dmat = jnp.exp(jnp.where(causal, r_buf[c, h:h + 1, :] - m_bc, -jnp.inf))
            s_buf[rows, ln] = (_dot_nt(qc.astype(BF16), kb) * dmat).astype(BF16)
            kwt_buf[rows, ln] = (jnp.transpose(kb).astype(F32) * dmat[CHUNK - 1:CHUNK, :]).astype(BF16)
            wq_buf[rows, ln] = (jnp.exp(m_prevs[c][:, h:h + 1] - m_bc) * qc).astype(BF16)

    for c in range(nchunk):
        rows = chunk_rows(c)
        for h in range(nh):
            ln = _head_lanes(h)
            v_aug = jnp.concatenate([v_ref[rows, ln].astype(BF16), ones_b], axis=1)
            both = _dot(jnp.concatenate([s_buf[rows, ln], kwt_buf[rows, ln]], axis=0), v_aug)
            p_buf[rows, aug_lanes(h)] = both[:CHUNK]
            u_buf[c, h] = both[CHUNK:]

    for c in range(nchunk):
        rows = chunk_rows(c)
        for h in range(nh):
            ln = _head_lanes(h)
            cs = c_st[h]
            res = p_buf[rows, aug_lanes(h)] + _dot(wq_buf[rows, ln], cs.astype(BF16))
            c_st[h] = keep_rows[c][:, h:h + 1] * cs + u_buf[c, h]
            den = res[:, HEAD_DIM + h:HEAD_DIM + h + 1]
            inv = 1.0 / jnp.maximum(jnp.abs(den), floor_buf[rows, h:h + 1])
            y = _head_norm(res[:, :HEAD_DIM] * inv) * nw_ref[:, ln] * _sigmoid(og_ref[rows, ln])
            o_ref[rows, ln] = y.astype(o_ref.dtype)


def mlstm_block(proj, tail, conv_w, conv_b, gate_bias, norm_w, layer, tb):
    l = proj.shape[0]
    base = (4 * RET_WIDTH) // MLSTM_WIDTH

    def col(j):
        return pl.BlockSpec((tb, MLSTM_WIDTH), lambda r, j=j: (r, base + j))

    nchunk = tb // CHUNK
    wide = pltpu.VMEM((tb, MLSTM_WIDTH), F32)
    hist = pltpu.VMEM((tb + SUBLANES, MLSTM_WIDTH), F32)
    narrow = pltpu.VMEM((tb, LANES), F32)
    half = pltpu.VMEM((tb, MLSTM_WIDTH), BF16)
    rows8 = pltpu.VMEM((nchunk, SUBLANES, LANES), F32)
    return pl.pallas_call(
        functools.partial(_mlstm_kernel, nchunk=nchunk, tb=tb),
        out_shape=jax.ShapeDtypeStruct((l, MLSTM_WIDTH), BF16),
        grid=(l // tb,),
        in_specs=[col(0), col(1), col(2), col(3),
                  pl.BlockSpec((tb, LANES), lambda r: (r, GATE_COLBLK)),
                  _layer_spec(conv_w, layer), _layer_spec(conv_b, layer), _layer_spec(gate_bias, layer),
                  _layer_spec(norm_w, layer)],
        out_specs=pl.BlockSpec((tb, MLSTM_WIDTH), lambda r: (r, 0)),
        scratch_shapes=[pltpu.VMEM((MLSTM_HEADS, HEAD_DIM, 2 * HEAD_DIM), F32), pltpu.VMEM((SUBLANES, LANES), F32),
                        hist, hist, wide, wide, narrow, rows8, narrow, narrow, narrow, narrow,
                        pltpu.VMEM((tb, 2 * MLSTM_WIDTH), F32),
                        pltpu.VMEM((nchunk, MLSTM_HEADS, HEAD_DIM, 2 * HEAD_DIM), F32),
                        half, half, half],
        compiler_params=_cparams(("arbitrary",)),
    )(proj, proj, proj, proj, tail, conv_w, conv_b, gate_bias, norm_w)


def _s5_prep_kernel(lre_ref, lim_ref, ldt_ref, btre_ref, btim_ref, ctre_ref, ctim_ref,
                    toep_ref, vre_ref, vim_ref, wre_ref, wim_ref, apre_ref, apim_ref):
    lre = lre_ref[...]
    lim = lim_ref[...]
    dt = jnp.exp(ldt_ref[...])
    mag = jnp.exp(lre * dt)
    ang = lim * dt
    zr = mag * jnp.cos(ang) - 1.0
    zi = mag * jnp.sin(ang)
    den = lre * lre + lim * lim
    w_re = (zr * lre + zi * lim) / den
    w_im = (zi * lre - zr * lim) / den
    row_g = lax.broadcasted_iota(jnp.int32, (LANES, S5_SB), 0) // S5_GROUP
    col_g = lax.broadcasted_iota(jnp.int32, (LANES, S5_SB), 1) // S5_STATE
    same = row_g == col_g
    bt_re = btre_ref[...]
    bt_im = btim_ref[...]
    bb_re = jnp.where(same, w_re * bt_re - w_im * bt_im, 0.0)
    bb_im = jnp.where(same, w_re * bt_im + w_im * bt_re, 0.0)
    ct_re = jnp.where(same, ctre_ref[...], 0.0)
    ct_im = jnp.where(same, ctim_ref[...], 0.0)
    hp = lax.Precision.HIGHEST
    kblk = []
    for d in range(S5_T + 1):
        pm = jnp.exp(lre * dt * float(d))
        pa = lim * dt * float(d)
        p_re = pm * jnp.cos(pa)
        p_im = pm * jnp.sin(pa)
        if d < S5_T:
            ab_re = p_re * bb_re - p_im * bb_im
            ab_im = p_re * bb_im + p_im * bb_re
            srow = slice((S5_T - 1 - d) * LANES, (S5_T - d) * LANES)
            vre_ref[0, srow, :] = ab_re.astype(BF16)
            vim_ref[0, srow, :] = ab_im.astype(BF16)
            kblk.append((_dot_nt(ab_re, ct_re, hp) - _dot_nt(ab_im, ct_im, hp)).astype(BF16))
        if d >= 1:
            trow = slice((d - 1) * LANES, d * LANES)
            wre_ref[0, trow, :] = (ct_re * p_re - ct_im * p_im).astype(BF16)
            wim_ref[0, trow, :] = (-(ct_re * p_im + ct_im * p_re)).astype(BF16)
    apre_ref[...] = jnp.zeros_like(apre_ref)
    apim_ref[...] = jnp.zeros_like(apim_ref)
    for k in range(SUBLANES + 1):
        pm = jnp.exp(lre * dt * float(S5_T * k))
        pa = lim * dt * float(S5_T * k)
        apre_ref[k:k + 1, :] = pm * jnp.cos(pa)
        apim_ref[k:k + 1, :] = pm * jnp.sin(pa)
    zero = jnp.zeros((LANES, LANES), BF16)
    for s in range(S5_T):
        for t in range(S5_T):
            toep_ref[0, s * LANES:(s + 1) * LANES, t * LANES:(t + 1) * LANES] = kblk[t - s] if t >= s else zero


def s5_operator_inputs(lam_re, lam_im, log_dt, b_re, b_im, c_re, c_im):
    depth = lam_re.shape[0]
    nst = S5_GROUPS * S5_STATE

    def tiled(m):
        return jnp.tile(m.astype(F32).reshape(depth, S5_WIDTH, S5_STATE), (1, 1, S5_GPB))

    lre = lam_re.astype(F32).reshape(depth, 1, nst)
    lim = lam_im.astype(F32).reshape(depth, 1, nst)
    ldt = jnp.repeat(log_dt.astype(F32), S5_STATE, axis=-1).reshape(depth, 1, nst)
    return (lre, lim, ldt, tiled(jnp.swapaxes(b_re, -1, -2)), tiled(jnp.swapaxes(b_im, -1, -2)),
            tiled(c_re), tiled(c_im))


def s5_operators(prep_inputs, layer):
    nst = S5_GROUPS * S5_STATE
    lane_in = pl.BlockSpec((None, 1, S5_SB), lambda g: (layer, 0, g))
    par_in = pl.BlockSpec((None, LANES, S5_SB), lambda g: (layer, g, 0))
    pow_out = pl.BlockSpec((2 * SUBLANES, S5_SB), lambda g: (0, g))

    def out_blk(rows, cols):
        return pl.BlockSpec((1, rows, cols), lambda g: (g, 0, 0))

    return pl.pallas_call(
        _s5_prep_kernel,
        out_shape=[jax.ShapeDtypeStruct((S5_GB, S5_KT, S5_KT), BF16)]
        + [jax.ShapeDtypeStruct((S5_GB, S5_KT, S5_SB), BF16)] * 4
        + [jax.ShapeDtypeStruct((2 * SUBLANES, nst), F32)] * 2,
        grid=(S5_GB,),
        in_specs=[lane_in, lane_in, lane_in, par_in, par_in, par_in, par_in],
        out_specs=[out_blk(S5_KT, S5_KT)] + [out_blk(S5_KT, S5_SB)] * 4 + [pow_out, pow_out],
        compiler_params=_cparams(("parallel",)),
    )(*prep_inputs)


def _gelu_tanh(x):
    c = math.sqrt(2.0 / math.pi)
    return 0.5 * x * (1.0 + jnp.tanh(c * (x + 0.044715 * (x * x * x))))


def _s5_kernel(u_ref, toep_ref, vre_ref, vim_ref, wre_ref, wim_ref, apre_ref, apim_ref, d_ref, y_ref,
               car_re, car_im, x_re, x_im, *, tmc):
    rb = pl.program_id(1)

    @pl.when(rb == 0)
    def _():
        car_re[...] = jnp.zeros_like(car_re)
        car_im[...] = jnp.zeros_like(car_im)

    us = [u_ref[pl.ds(s, tmc, stride=S5_T), :] for s in range(S5_T)]
    ucat = jnp.concatenate([u.astype(BF16) for u in us], axis=1)
    y_re = _dot(ucat, vre_ref[0])
    y_im = _dot(ucat, vim_ref[0])

    row_in_tile = lax.broadcasted_iota(jnp.int32, (tmc, 1), 0) & (SUBLANES - 1)

    def shifted(v, s):
        return jnp.where(row_in_tile >= s, pltpu.roll(v, s, 0), 0.0)

    for s in (1, 2, 4):
        a_r = apre_ref[s:s + 1, :]
        a_i = apim_ref[s:s + 1, :]
        s_re, s_im = shifted(y_re, s), shifted(y_im, s)
        y_re, y_im = y_re + a_r * s_re - a_i * s_im, y_im + a_r * s_im + a_i * s_re
    e_re, e_im = shifted(y_re, 1), shifted(y_im, 1)
    p_re = apre_ref[0:SUBLANES, :]
    p_im = apim_ref[0:SUBLANES, :]
    a8_r = apre_ref[SUBLANES:SUBLANES + 1, :]
    a8_i = apim_ref[SUBLANES:SUBLANES + 1, :]
    xr = car_re[0:1, :]
    xi = car_im[0:1, :]
    for t in range(tmc // SUBLANES):
        rows = slice(t * SUBLANES, (t + 1) * SUBLANES)
        x_re[rows, :] = e_re[rows] + p_re * xr - p_im * xi
        x_im[rows, :] = e_im[rows] + p_re * xi + p_im * xr
        last = (t + 1) * SUBLANES - 1
        xr, xi = (a8_r * xr - a8_i * xi + y_re[last:last + 1], a8_r * xi + a8_i * xr + y_im[last:last + 1])
    car_re[0:1, :] = xr
    car_im[0:1, :] = xi
    y = (_dot(ucat, toep_ref[0]) + _dot_nt(x_re[...].astype(BF16), wre_ref[0])
         + _dot_nt(x_im[...].astype(BF16), wim_ref[0]))
    for t in range(S5_T):
        yt = y[:, t * LANES:(t + 1) * LANES] + d_ref[...] * us[t]
        y_ref[pl.ds(t, tmc, stride=S5_T), :] = _gelu_tanh(yt)


def s5_block(tail, ops, d_skip, layer, tb):
    l = tail.shape[0]
    toep, v_re, v_im, w_re, w_im, ap_re, ap_im = ops
    tmc = tb // S5_T

    def per_gb(cols):
        return pl.BlockSpec((1, S5_KT, cols), lambda g, r: (g, 0, 0))

    pow_blk = pl.BlockSpec((2 * SUBLANES, S5_SB), lambda g, r: (0, g))
    fold = pltpu.VMEM((tmc, S5_SB), F32)
    carry = pltpu.VMEM((SUBLANES, S5_SB), F32)
    return pl.pallas_call(
        functools.partial(_s5_kernel, tmc=tmc),
        out_shape=jax.ShapeDtypeStruct((l, S5_WIDTH), F32),
        grid=(S5_GB, l // tb),
        in_specs=[pl.BlockSpec((tb, LANES), lambda g, r: (r, g)),
                  per_gb(S5_KT), per_gb(S5_SB), per_gb(S5_SB), per_gb(S5_SB), per_gb(S5_SB),
                  pow_blk, pow_blk, pl.BlockSpec((None, 1, LANES), lambda g, r: (layer, 0, g))],
        out_specs=pl.BlockSpec((tb, LANES), lambda g, r: (r, g)),
        scratch_shapes=[carry, carry, fold, fold],
        compiler_params=_cparams(("parallel", "arbitrary")),
    )(tail, toep, v_re, v_im, w_re, w_im, ap_re, ap_im, d_skip)


def _layer_norm_by_column_tiles(z_tile, z_buf, lw_ref, lb_ref, o_ref, ob_ref):
    rows = z_buf.shape[0]
    pivot = None
    s1 = jnp.zeros((rows, LANES), F32)
    s2 = jnp.zeros((rows, LANES), F32)
    for n in range(D_MODEL // LN_TILE):
        cols = slice(n * LN_TILE, (n + 1) * LN_TILE)
        z = z_tile(cols)
        z_buf[:, cols] = z
        if pivot is None:
            pivot = jnp.mean(z, axis=-1, keepdims=True)
        for j in range(LN_TILE // LANES):
            dz = z[:, j * LANES:(j + 1) * LANES] - pivot
            s1 = s1 + dz
            s2 = s2 + dz * dz
    m1 = jnp.sum(s1, axis=-1, keepdims=True) * (1.0 / D_MODEL)
    var = jnp.sum(s2, axis=-1, keepdims=True) * (1.0 / D_MODEL) - m1 * m1
    mu = pivot + m1
    rstd = lax.rsqrt(var + LN_EPS)
    for n in range(D_MODEL // LN_TILE):
        cols = slice(n * LN_TILE, (n + 1) * LN_TILE)
        out = (z_buf[:, cols] - mu) * rstd * lw_ref[:, cols] + lb_ref[:, cols]
        o_ref[:, cols] = out
        ob_ref[:, cols] = out.astype(BF16)


def _outproj_kernel(a1_ref, a2_ref, ys_ref, gw_ref, gb_ref, w1_ref, w2_ref, w3_ref, x_ref, lw_ref, lb_ref,
                    o_ref, ob_ref, z_buf):
    a1, a2 = a1_ref[...], a2_ref[...]
    ys = ys_ref[...]
    a3 = (ys * _sigmoid(_dot(ys.astype(BF16), gw_ref[...]) + gb_ref[...])).astype(BF16)

    def z_tile(cols):
        mix = _dot(a1, w1_ref[:, cols]) + _dot(a2, w2_ref[:, cols]) + _dot(a3, w3_ref[:, cols])
        return ALPHA * x_ref[:, cols] + mix

    _layer_norm_by_column_tiles(z_tile, z_buf, lw_ref, lb_ref, o_ref, ob_ref)


def outproj_block(y_ret, y_m, y_s, glu_w, glu_b, w_out, x, ln_w, ln_b, layer, tm):
    l = x.shape[0]

    def rows(n):
        return pl.BlockSpec((tm, n), lambda i: (i, 0))

    def wrows(n, blk):
        return pl.BlockSpec((None, n, D_MODEL), lambda i: (layer, blk, 0))

    return pl.pallas_call(
        _outproj_kernel,
        out_shape=[jax.ShapeDtypeStruct((l, D_MODEL), F32), jax.ShapeDtypeStruct((l, D_MODEL), BF16)],
        grid=(l // tm,),
        in_specs=[rows(RET_WIDTH), rows(MLSTM_WIDTH), rows(S5_WIDTH),
                  _layer_spec(glu_w, layer), _layer_spec(glu_b, layer), wrows(RET_WIDTH, 0), wrows(MLSTM_WIDTH, 1), wrows(S5_WIDTH, (RET_WIDTH + MLSTM_WIDTH) // S5_WIDTH),
                  rows(D_MODEL), _layer_spec(ln_w, layer), _layer_spec(ln_b, layer)],
        out_specs=[rows(D_MODEL), rows(D_MODEL)],
        scratch_shapes=[pltpu.VMEM((tm, D_MODEL), F32)],
        compiler_params=_cparams(("parallel",)),
    )(y_ret, y_m, y_s, glu_w, glu_b, w_out, w_out, w_out, x, ln_w, ln_b)


def _down_kernel(h_ref, w_ref, x1_ref, r_ref, rb_ref):
    r = ALPHA * x1_ref[...] + _dot(h_ref[...], w_ref[...])
    r_ref[...] = r
    rb_ref[...] = r.astype(BF16)


def down_block(hid, w_down, x1, layer, tm):
    l, ff = hid.shape
    rows = pl.BlockSpec((tm, D_MODEL), lambda i: (i, 0))
    return pl.pallas_call(
        _down_kernel,
        out_shape=[jax.ShapeDtypeStruct((l, D_MODEL), F32), jax.ShapeDtypeStruct((l, D_MODEL), BF16)],
        grid=(l // tm,),
        in_specs=[pl.BlockSpec((tm, ff), lambda i: (i, 0)),
                  pl.BlockSpec((None, ff, D_MODEL), lambda i: (layer, 0, 0), pipeline_mode=pl.Buffered(1)), rows],
        out_specs=[rows, rows],
        compiler_params=_cparams(("parallel",)),
    )(hid, w_down, x1)


def _final_kernel(r_ref, rb_ref, p_ref, wg_ref, wp_ref, lw_ref, lb_ref, o_ref, ob_ref, z_buf):
    rb, pb = rb_ref[...], p_ref[...]

    def z_tile(cols):
        gate = _sigmoid(_dot(rb, wg_ref[:, cols]))
        return r_ref[:, cols] + gate * _dot(pb, wp_ref[:, cols])

    _layer_norm_by_column_tiles(z_tile, z_buf, lw_ref, lb_ref, o_ref, ob_ref)


def final_block(r, rb, p_b, w_gate, w_ple, ln_w, ln_b, layer, tm):
    l = r.shape[0]
    rows = pl.BlockSpec((tm, D_MODEL), lambda i: (i, 0))
    return pl.pallas_call(
        _final_kernel,
        out_shape=[jax.ShapeDtypeStruct((l, D_MODEL), F32), jax.ShapeDtypeStruct((l, D_MODEL), BF16)],
        grid=(l // tm,),
        in_specs=[rows, rows, pl.BlockSpec((None, tm, PLE_DIM), lambda i: (layer, i, 0)),
                  _layer_spec(w_gate, layer), _layer_spec(w_ple, layer), _layer_spec(ln_w, layer),
                  _layer_spec(ln_b, layer)],
        out_specs=[rows, rows],
        scratch_shapes=[pltpu.VMEM((tm, D_MODEL), F32)],
        compiler_params=_cparams(("parallel",)),
    )(r, rb, p_b, w_gate, w_ple, ln_w, ln_b)


def _rows3(v):
    return v.astype(F32).reshape(v.shape[0], 1, -1)


def _tiles(l):
    return {"mm_m": min(1024, l), "tail_m": min(2048, l), "mix": min(512, l), "s5": min(2048, l), "row": min(512, l),
            "down": min(256, l), "in_n": 2 * RET_WIDTH, "up_n": 2048, "rope": min(2048, l)}


def kernel(x, p, positions, w_in, mlstm_conv_w, mlstm_conv_b, mlstm_i_bias, mlstm_f_bias, ret_norm_w, mlstm_norm_w, s5_lambda_re, s5_lambda_im, s5_log_dt, s5_B_re, s5_B_im, s5_C_re, s5_C_im, s5_D, s5_glu_w, s5_glu_b, w_out, ln1_w, ln1_b, w_up, w_down, w_gate, w_ple, ln2_w, ln2_b):
    bsz, l, _ = x.shape
    depth = w_in.shape[0]
    assert bsz == 1
    t = _tiles(l)
    cos_t, sin_t = rope_tables(positions.astype(F32).reshape(l, 1), t["rope"])
    xf = x.reshape(l, D_MODEL).astype(F32)
    xb = None
    p_b = p.reshape(depth, l, PLE_DIM).astype(BF16)

    w_in_t = jnp.transpose(w_in, (2, 0, 1))
    w_in_b = cast_cols_transposed(w_in_t, 0, MAIN_WIDTH, LANES)
    w_tail_b = cast_cols_transposed(w_in_t, MAIN_WIDTH, TAIL_WIDTH, LANES)
    w_out_b, w_up_b, w_down_b = w_out.astype(BF16), w_up.astype(BF16), w_down.astype(BF16)
    w_gate_b, w_ple_b, glu_w_b = w_gate.astype(BF16), w_ple.astype(BF16), s5_glu_w.astype(BF16)
    gate_bias = jnp.concatenate([jnp.zeros((depth, GATE_LANE0), F32), mlstm_i_bias.astype(F32),
                                 mlstm_f_bias.astype(F32)], axis=1).reshape(depth, 1, LANES)
    conv_w = mlstm_conv_w.astype(F32)
    conv_b, ret_nw, mlstm_nw = _rows3(mlstm_conv_b), _rows3(ret_norm_w), _rows3(mlstm_norm_w)
    s5_d, glu_b = _rows3(s5_D), _rows3(s5_glu_b)
    ln1w, ln1b, ln2w, ln2b = _rows3(ln1_w), _rows3(ln1_b), _rows3(ln2_w), _rows3(ln2_b)
    s5_in = s5_operator_inputs(s5_lambda_re, s5_lambda_im, s5_log_dt, s5_B_re, s5_B_im, s5_C_re, s5_C_im)

    for i in range(depth):
        proj, xb = in_projection(xf if xb is None else xb, w_in_b, cos_t, sin_t, i, tm=t["mm_m"], tn=t["in_n"])
        tail = matmul(xb, w_tail_b, i, out_dtype=F32, tm=t["tail_m"], tn=TAIL_WIDTH, act="rotate_tail",
                      b_cols_major=True)
        y_ret = retention_block(proj, ret_nw, i, t["mix"])
        y_m = mlstm_block(proj, tail, conv_w, conv_b, gate_bias, mlstm_nw, i, t["mix"])
        y_s = s5_block(tail, s5_operators(s5_in, i), s5_d, i, t["s5"])
        x1, x1b = outproj_block(y_ret, y_m, y_s, glu_w_b, glu_b, w_out_b, xf, ln1w, ln1b, i, t["row"])
        hid = matmul(x1b, w_up_b, i, out_dtype=BF16, tm=t["mm_m"], tn=t["up_n"], act="relu2")
        r, rb = down_block(hid, w_down_b, x1, i, t["down"])
        xf, xb = final_block(r, rb, p_b, w_gate_b, w_ple_b, ln2w, ln2b, i, t["row"])
    return xf.reshape(bsz, l, D_MODEL)
```

```python
import functools
import math

import jax
import jax.numpy as jnp
import numpy as np
from jax import lax
from jax.experimental import pallas as pl
from jax.experimental.pallas import tpu as pltpu

F32 = jnp.float32
BF16 = jnp.bfloat16

D_MODEL = 2048
DEPTH = 2
HEAD_DIM = 128
RET_HEADS = 6
MLSTM_HEADS = 6
RET_WIDTH = RET_HEADS * HEAD_DIM
MLSTM_WIDTH = MLSTM_HEADS * HEAD_DIM
S5_WIDTH = D_MODEL - RET_WIDTH - MLSTM_WIDTH
S5_GROUP = 16
S5_GROUPS = S5_WIDTH // S5_GROUP
S5_STATE = 64
CONV_WIDTH = 4
CHUNK = 128
PLE_DIM = 256
ROPE_BASE = 10000.0
LN_EPS = 1e-5
ALPHA = (2 * DEPTH) ** 0.25
QK_SCALE = HEAD_DIM ** -0.5

LANES = 128
SUBLANES = 8
MAIN_WIDTH = 4 * RET_WIDTH + 4 * MLSTM_WIDTH
TAIL_COLS = 2 * MLSTM_HEADS + S5_WIDTH
TAIL_WIDTH = 5 * LANES
GATE_LANE0 = LANES - 2 * MLSTM_HEADS
GATE_COLBLK = S5_WIDTH // LANES
ROT_COL_TILE = 0
ROT_GROUP = 256
LN_TILE = 256
CONV_ROWS = 64
S5_T = 16
S5_GB = S5_WIDTH // LANES
S5_GPB = LANES // S5_GROUP
S5_SB = S5_GPB * S5_STATE
S5_KT = S5_T * LANES
VMEM_LIMIT = 56 * 1024 * 1024


def _cparams(sem):
    return pltpu.CompilerParams(dimension_semantics=sem, vmem_limit_bytes=VMEM_LIMIT)


def _mm_kernel(a_ref, b_ref, o_ref, *scratch, nk, act, b_cols_major):
    def finish(r):
        if act == "relu2":
            r = jnp.square(jnp.maximum(r, 0.0))
        elif act == "rotate_tail":
            r = pltpu.roll(r, TAIL_WIDTH - 2 * MLSTM_HEADS, 1)
        o_ref[...] = r.astype(o_ref.dtype)

    if nk == 1:
        finish(_dot_nt(a_ref[...], b_ref[...]) if b_cols_major else _dot(a_ref[...], b_ref[...]))
        return
    acc_ref, = scratch
    k = pl.program_id(2)

    @pl.when(k == 0)
    def _():
        acc_ref[...] = jnp.zeros_like(acc_ref)

    acc_ref[...] += _dot(a_ref[...], b_ref[...])

    @pl.when(k == nk - 1)
    def _():
        finish(acc_ref[...])


def matmul(a, b, layer, *, out_dtype, tm, tn, tk=None, act=None, b_cols_major=False):
    m, kdim = a.shape
    n = b.shape[1] if b_cols_major else b.shape[2]
    tk = kdim if tk is None else tk
    nk = kdim // tk
    assert m % tm == 0 and n % tn == 0 and kdim % tk == 0 and b.shape[2 if b_cols_major else 1] == kdim
    assert nk == 1 or not b_cols_major
    b_spec = (pl.BlockSpec((None, tn, tk), lambda i, j, k: (layer, j, k)) if b_cols_major
              else pl.BlockSpec((None, tk, tn), lambda i, j, k: (layer, k, j)))
    return pl.pallas_call(
        functools.partial(_mm_kernel, nk=nk, act=act, b_cols_major=b_cols_major),
        out_shape=jax.ShapeDtypeStruct((m, n), out_dtype),
        grid=(m // tm, n // tn, nk),
        in_specs=[pl.BlockSpec((tm, tk), lambda i, j, k: (i, k)), b_spec],
        out_specs=pl.BlockSpec((tm, tn), lambda i, j, k: (i, j)),
        scratch_shapes=[] if nk == 1 else [pltpu.VMEM((tm, tn), F32)],
        compiler_params=_cparams(("parallel", "parallel", "arbitrary")),
    )(a, b)


def _in_proj_kernel(a_ref, b_ref, cos_ref, sin_ref, *rest, cast_a):
    j = pl.program_id(1)
    if cast_a:
        o_ref, ab_ref = rest

        @pl.when(j == 0)
        def _():
            ab_ref[...] = a_ref[...].astype(BF16)
        a_src = ab_ref
    else:
        o_ref, = rest
        a_src = a_ref

    @pl.when(j != ROT_COL_TILE)
    def _():
        o_ref[...] = _dot_nt(a_src[...], b_ref[...])

    @pl.when(j == ROT_COL_TILE)
    def _():
        a = a_src[...]
        cos = cos_ref[...]
        sin = sin_ref[...]
        for g in range(o_ref.shape[1] // ROT_GROUP):
            z = _dot_nt(a, b_ref[g * ROT_GROUP:(g + 1) * ROT_GROUP, :])
            for h in range(ROT_GROUP // HEAD_DIM):
                c0 = g * ROT_GROUP + h * HEAD_DIM
                t = z[:, h * HEAD_DIM:(h + 1) * HEAD_DIM]
                rot = t * cos + pltpu.roll(t, HEAD_DIM // 2, 1) * sin
                o_ref[:, c0:c0 + HEAD_DIM] = rot * QK_SCALE if c0 >= RET_WIDTH else rot


def in_projection(a, b, cos_t, sin_t, layer, *, tm, tn):
    m, kdim = a.shape
    n = b.shape[1]
    cast_a = a.dtype != BF16
    assert m % tm == 0 and n % tn == 0 and b.shape[2] == kdim and tn == 2 * RET_WIDTH
    a_blk = pl.BlockSpec((tm, kdim), lambda i, j: (i, 0))
    tab = pl.BlockSpec((tm, LANES), lambda i, j: (i, 0))
    out_shape = [jax.ShapeDtypeStruct((m, n), F32)]
    out_specs = [pl.BlockSpec((tm, tn), lambda i, j: (i, j))]
    if cast_a:
        out_shape.append(jax.ShapeDtypeStruct((m, kdim), BF16))
        out_specs.append(a_blk)
    out = pl.pallas_call(
        functools.partial(_in_proj_kernel, cast_a=cast_a),
        out_shape=out_shape,
        grid=(m // tm, n // tn),
        in_specs=[a_blk, pl.BlockSpec((None, tn, kdim), lambda i, j: (layer, j, 0)), tab, tab],
        out_specs=out_specs,
        compiler_params=_cparams(("parallel", "arbitrary")),
    )(a, b, cos_t, sin_t)
    return (out[0], out[1]) if cast_a else (out[0], a)


def _cast_cols_kernel(x_ref, o_ref, *, col0, ncols_total):
    tc = x_ref.shape[0]
    col = col0 + pl.program_id(0) * tc + lax.broadcasted_iota(jnp.int32, (tc, 1), 0)
    for d in range(x_ref.shape[1]):
        o_ref[d] = jnp.where(col < ncols_total, x_ref[:, d, :], 0.0).astype(o_ref.dtype)


def cast_cols_transposed(w_t, col0, ncols, tc):
    ctot, depth, kdim = w_t.shape
    assert col0 % tc == 0 and ncols % tc == 0
    return pl.pallas_call(
        functools.partial(_cast_cols_kernel, col0=col0, ncols_total=ctot),
        out_shape=jax.ShapeDtypeStruct((depth, ncols, kdim), BF16),
        grid=(ncols // tc,),
        in_specs=[pl.BlockSpec((tc, depth, kdim), lambda i: (col0 // tc + i, 0, 0))],
        out_specs=pl.BlockSpec((depth, tc, kdim), lambda i: (0, i, 0)),
        compiler_params=_cparams(("parallel",)),
    )(w_t)


def _rope_kernel(pos_ref, inv_ref, sign_ref, cos_ref, sin_ref):
    ang = pos_ref[...] * inv_ref[...]
    cos_ref[...] = jnp.cos(ang)
    sin_ref[...] = jnp.sin(ang) * sign_ref[...]


def rope_tables(pos_col, tl):
    l = pos_col.shape[0]
    half = np.arange(0, HEAD_DIM, 2, dtype=np.float32) / np.float32(HEAD_DIM)
    inv = (np.float32(ROPE_BASE) ** (-half)).astype(np.float32)
    inv2 = jnp.asarray(np.concatenate([inv, inv])[None, :])
    sign = jnp.asarray(np.concatenate([-np.ones(64, np.float32), np.ones(64, np.float32)])[None, :])
    row = pl.BlockSpec((1, LANES), lambda i: (0, 0))
    return pl.pallas_call(
        _rope_kernel,
        out_shape=[jax.ShapeDtypeStruct((l, LANES), F32)] * 2,
        grid=(l // tl,),
        in_specs=[pl.BlockSpec((tl, 1), lambda i: (i, 0)), row, row],
        out_specs=[pl.BlockSpec((tl, LANES), lambda i: (i, 0))] * 2,
        compiler_params=_cparams(("parallel",)),
    )(pos_col, inv2, sign)


def _head_norm(y):
    mu = jnp.mean(y, axis=-1, keepdims=True)
    d = y - mu
    var = jnp.mean(d * d, axis=-1, keepdims=True)
    return d * lax.rsqrt(var + LN_EPS)


def _sigmoid(x):
    return 1.0 / (1.0 + jnp.exp(-x))


def _dot_nt(a, b, precision=None):
    return lax.dot_general(a, b, (((1,), (1,)), ((), ())), precision=precision, preferred_element_type=F32)


def _dot(a, b):
    return jnp.dot(a, b, preferred_element_type=F32)


def _head_lanes(h):
    return slice(h * HEAD_DIM, (h + 1) * HEAD_DIM)


def _layer_spec(stacked, layer):
    return pl.BlockSpec((None,) + stacked.shape[1:], lambda *_: (layer, 0, 0))


def _ret_log_gamma(h):
    return float(np.log(np.float32(1.0) - np.float32(2.0) ** np.float32(-5.0 - h)))


def _ret_kernel(q_ref, k_ref, v_ref, g_ref, w_ref, o_ref,
                state_ref, decay_ref, qdec_ref, kdec_ref, qb_buf, s_buf, kdt_buf, intra_buf, incr_buf, *, nchunk):
    rb = pl.program_id(0)

    @pl.when(rb == 0)
    def _():
        state_ref[...] = jnp.zeros_like(state_ref)
        ii = lax.broadcasted_iota(jnp.int32, (CHUNK, CHUNK), 0)
        jj = lax.broadcasted_iota(jnp.int32, (CHUNK, CHUNK), 1)
        rel = (ii - jj).astype(F32)
        idx = ii.astype(F32)
        for h in range(RET_HEADS):
            lg = _ret_log_gamma(h)
            decay_ref[h] = jnp.where(rel >= 0.0, jnp.exp(lg * jnp.maximum(rel, 0.0)), 0.0)
            qdec_ref[h] = jnp.exp(lg * (idx + 1.0))
            kdec_ref[h] = jnp.exp(lg * (CHUNK - 1.0 - idx))

    for c in range(nchunk):
        rows = slice(c * CHUNK, (c + 1) * CHUNK)
        for h in range(RET_HEADS):
            ln = _head_lanes(h)
            kr = k_ref[rows, ln]
            qb = q_ref[rows, ln].astype(BF16)
            qb_buf[rows, ln] = qb
            s_buf[rows, ln] = (_dot_nt(qb, kr.astype(BF16)) * decay_ref[h]).astype(BF16)
            kdt_buf[rows, ln] = jnp.transpose((kr * kdec_ref[h]).astype(BF16))

    for c in range(nchunk):
        rows = slice(c * CHUNK, (c + 1) * CHUNK)
        for h in range(RET_HEADS):
            ln = _head_lanes(h)
            lhs = jnp.concatenate([s_buf[rows, ln], kdt_buf[rows, ln]], axis=0)
            both = _dot(lhs, v_ref[rows, ln].astype(BF16))
            intra_buf[rows, ln] = both[:CHUNK]
            incr_buf[c, h] = both[CHUNK:]

    for c in range(nchunk):
        rows = slice(c * CHUNK, (c + 1) * CHUNK)
        for h in range(RET_HEADS):
            ln = _head_lanes(h)
            st = state_ref[h]
            out = intra_buf[rows, ln] + _dot(qb_buf[rows, ln], st.astype(BF16)) * qdec_ref[h]
            state_ref[h] = st * math.exp(_ret_log_gamma(h) * CHUNK) + incr_buf[c, h]
            g = g_ref[rows, ln]
            y = _head_norm(out) * w_ref[:, ln] * (g * _sigmoid(g))
            o_ref[rows, ln] = y.astype(o_ref.dtype)


def retention_block(proj, norm_w, layer, tb):
    l = proj.shape[0]

    def col(j):
        return pl.BlockSpec((tb, RET_WIDTH), lambda r, j=j: (r, j))

    hsq = pltpu.VMEM((RET_HEADS, CHUNK, CHUNK), F32)
    half = pltpu.VMEM((tb, RET_WIDTH), BF16)
    return pl.pallas_call(
        functools.partial(_ret_kernel, nchunk=tb // CHUNK),
        out_shape=jax.ShapeDtypeStruct((l, RET_WIDTH), BF16),
        grid=(l // tb,),
        in_specs=[col(0), col(1), col(2), col(3), _layer_spec(norm_w, layer)],
        out_specs=pl.BlockSpec((tb, RET_WIDTH), lambda r: (r, 0)),
        scratch_shapes=[hsq, hsq, hsq, hsq, half, half, half, pltpu.VMEM((tb, RET_WIDTH), F32),
                        pltpu.VMEM((tb // CHUNK, RET_HEADS, HEAD_DIM, HEAD_DIM), F32)],
        compiler_params=_cparams(("arbitrary",)),
    )(proj, proj, proj, proj, norm_w)


def _log_sigmoid(x):
    return -(jnp.maximum(-x, 0.0) + jnp.log1p(jnp.exp(-jnp.abs(x))))


def _mlstm_kernel(q_ref, k_ref, v_ref, og_ref, gate_ref, cw_ref, cb_ref, gb_ref, nw_ref,
                  o_ref, c_st, m_st, qbuf, kbuf, qs, ks, gs, r_buf, cmax_buf, bcum_buf, mall_buf, floor_buf,
                  p_buf, u_buf, wq_buf, s_buf, kwt_buf, *, nchunk, tb):
    rb = pl.program_id(0)
    pad = SUBLANES
    nh = MLSTM_HEADS

    @pl.when(rb == 0)
    def _():
        c_st[...] = jnp.zeros_like(c_st)
        m_st[...] = jnp.zeros_like(m_st)
        qbuf[0:pad, :] = jnp.zeros((pad, MLSTM_WIDTH), F32)
        kbuf[0:pad, :] = jnp.zeros((pad, MLSTM_WIDTH), F32)

    qbuf[pad:pad + tb, :] = q_ref[...]
    kbuf[pad:pad + tb, :] = k_ref[...]
    for r0 in range(0, tb, CONV_ROWS):
        for buf, woff, dst, scale in ((qbuf, 0, qs, None), (kbuf, MLSTM_WIDTH, ks, QK_SCALE)):
            wl = slice(woff, woff + MLSTM_WIDTH)
            acc = jnp.broadcast_to(cb_ref[:, wl], (CONV_ROWS, MLSTM_WIDTH))
            for tap in range(CONV_WIDTH):
                off = r0 + pad - (CONV_WIDTH - 1) + tap
                acc = acc + buf[off:off + CONV_ROWS, :] * cw_ref[tap:tap + 1, wl]
            act = acc * _sigmoid(acc)
            dst[r0:r0 + CONV_ROWS, :] = act if scale is None else act * scale
    qbuf[0:pad, :] = q_ref[tb - pad:tb, :]
    kbuf[0:pad, :] = k_ref[tb - pad:tb, :]

    lane_row = lax.broadcasted_iota(jnp.int32, (1, LANES), 1)
    graw = pltpu.roll(gate_ref[...] + gb_ref[...], LANES - GATE_LANE0, 1)
    gs[...] = jnp.where(lane_row < nh, graw, jnp.where(lane_row < 2 * nh, _log_sigmoid(graw), 0.0))

    ii = lax.broadcasted_iota(jnp.int32, (CHUNK, CHUNK), 0)
    jj = lax.broadcasted_iota(jnp.int32, (CHUNK, CHUNK), 1)
    causal = ii >= jj
    tril = causal.astype(F32)
    lane8 = lax.broadcasted_iota(jnp.int32, (SUBLANES, LANES), 1)
    ones_b = jnp.ones((CHUNK, HEAD_DIM), BF16)
    full = (CHUNK, CHUNK)

    def chunk_rows(c):
        return slice(c * CHUNK, (c + 1) * CHUNK)

    def aug_lanes(h):
        return slice(h * 2 * HEAD_DIM, (h + 1) * 2 * HEAD_DIM)

    for c in range(nchunk):
        rows = chunk_rows(c)
        g = gs[rows, :]
        gcum = jnp.dot(tril, g, precision=lax.Precision.HIGHEST, preferred_element_type=F32)
        bcum = pltpu.roll(gcum, LANES - nh, 1)
        r8 = jnp.transpose(g - bcum)[0:SUBLANES, :]
        cmax = r8
        for sh in (1, 2, 4, 8, 16, 32, 64):
            cmax = jnp.maximum(cmax, jnp.where(lane8 >= sh, pltpu.roll(cmax, sh, 1), -jnp.inf))
        r_buf[c] = r8
        cmax_buf[rows, :] = jnp.transpose(
            jnp.concatenate([cmax, jnp.zeros((CHUNK - SUBLANES, LANES), F32)], axis=0))
        bcum_buf[rows, :] = bcum

    m_prev = m_st[0:1, :]
    m_prevs, keep_rows = [], []
    for c in range(nchunk):
        rows = chunk_rows(c)
        m_all = jnp.maximum(cmax_buf[rows, :], m_prev)
        floor_buf[rows, :] = jnp.exp(-(bcum_buf[rows, :] + m_all))
        mall_buf[rows, :] = m_all
        m_last = m_all[CHUNK - 1:CHUNK, :]
        m_prevs.append(m_prev)
        keep_rows.append(jnp.exp(m_prev - m_last))
        m_prev = bcum_buf[c * CHUNK + CHUNK - 1:(c + 1) * CHUNK, :] + m_last
    m_st[0:1, :] = m_prev

    for c in range(nchunk):
        rows = chunk_rows(c)
        for h in range(nh):
            ln = _head_lanes(h)
            qc = qs[rows, ln]
            kb = ks[rows, ln].astype(BF16)
            m_bc = jnp.broadcast_to(mall_buf[rows, h:h + 1], full)
            dmat = jnp.exp(jnp.where(causal, r_buf[c, h:h + 1, :] - m_bc, -jnp.inf))
            s_buf[rows, ln] = (_dot_nt(qc.astype(BF16), kb) * dmat).astype(BF16)
            kwt_buf[rows, ln] = (jnp.transpose(kb).astype(F32) * dmat[CHUNK - 1:CHUNK, :]).astype(BF16)
            wq_buf[rows, ln] = (jnp.exp(m_prevs[c][:, h:h + 1] - m_bc) * qc).astype(BF16)

    for c in range(nchunk):
        rows = chunk_rows(c)
        for h in range(nh):
            ln = _head_lanes(h)
            v_aug = jnp.concatenate([v_ref[rows, ln].astype(BF16), ones_b], axis=1)
            both = _dot(jnp.concatenate([s_buf[rows, ln], kwt_buf[rows, ln]], axis=0), v_aug)
            p_buf[rows, aug_lanes(h)] = both[:CHUNK]
            u_buf[c, h] = both[CHUNK:]

    for c in range(nchunk):
        rows = chunk_rows(c)
        for h in range(nh):
            ln = _head_lanes(h)
            cs = c_st[h]
            res = p_buf[rows, aug_lanes(h)] + _dot(wq_buf[rows, ln], cs.astype(BF16))
            c_st[h] = keep_rows[c][:, h:h + 1] * cs + u_buf[c, h]
            den = res[:, HEAD_DIM + h:HEAD_DIM + h + 1]
            inv = 1.0 / jnp.maximum(jnp.abs(den), floor_buf[rows, h:h + 1])
            y = _head_norm(res[:, :HEAD_DIM] * inv) * nw_ref[:, ln] * _sigmoid(og_ref[rows, ln])
            o_ref[rows, ln] = y.astype(o_ref.dtype)


def mlstm_block(proj, tail, conv_w, conv_b, gate_bias, norm_w, layer, tb):
    l = proj.shape[0]
    base = (4 * RET_WIDTH) // MLSTM_WIDTH

    def col(j):
        return pl.BlockSpec((tb, MLSTM_WIDTH), lambda r, j=j: (r, base + j))

    nchunk = tb // CHUNK
    wide = pltpu.VMEM((tb, MLSTM_WIDTH), F32)
    hist = pltpu.VMEM((tb + SUBLANES, MLSTM_WIDTH), F32)
    narrow = pltpu.VMEM((tb, LANES), F32)
    half = pltpu.VMEM((tb, MLSTM_WIDTH), BF16)
    rows8 = pltpu.VMEM((nchunk, SUBLANES, LANES), F32)
    return pl.pallas_call(
        functools.partial(_mlstm_kernel, nchunk=nchunk, tb=tb),
        out_shape=jax.ShapeDtypeStruct((l, MLSTM_WIDTH), BF16),
        grid=(l // tb,),
        in_specs=[col(0), col(1), col(2), col(3),
                  pl.BlockSpec((tb, LANES), lambda r: (r, GATE_COLBLK)),
                  _layer_spec(conv_w, layer), _layer_spec(conv_b, layer), _layer_spec(gate_bias, layer),
                  _layer_spec(norm_w, layer)],
        out_specs=pl.BlockSpec((tb, MLSTM_WIDTH), lambda r: (r, 0)),
        scratch_shapes=[pltpu.VMEM((MLSTM_HEADS, HEAD_DIM, 2 * HEAD_DIM), F32), pltpu.VMEM((SUBLANES, LANES), F32),
                        hist, hist, wide, wide, narrow, rows8, narrow, narrow, narrow, narrow,
                        pltpu.VMEM((tb, 2 * MLSTM_WIDTH), F32),
                        pltpu.VMEM((nchunk, MLSTM_HEADS, HEAD_DIM, 2 * HEAD_DIM), F32),
                        half, half, half],
        compiler_params=_cparams(("arbitrary",)),
    )(proj, proj, proj, proj, tail, conv_w, conv_b, gate_bias, norm_w)


def _s5_prep_kernel(lre_ref, lim_ref, ldt_ref, btre_ref, btim_ref, ctre_ref, ctim_ref,
                    toep_ref, vre_ref, vim_ref, wre_ref, wim_ref, apre_ref, apim_ref):
    lre = lre_ref[...]
    lim = lim_ref[...]
    dt = jnp.exp(ldt_ref[...])
    mag = jnp.exp(lre * dt)
    ang = lim * dt
    zr = mag * jnp.cos(ang) - 1.0
    zi = mag * jnp.sin(ang)
    den = lre * lre + lim * lim
    w_re = (zr * lre + zi * lim) / den
    w_im = (zi * lre - zr * lim) / den
    row_g = lax.broadcasted_iota(jnp.int32, (LANES, S5_SB), 0) // S5_GROUP
    col_g = lax.broadcasted_iota(jnp.int32, (LANES, S5_SB), 1) // S5_STATE
    same = row_g == col_g
    bt_re = btre_ref[...]
    bt_im = btim_ref[...]
    bb_re = jnp.where(same, w_re * bt_re - w_im * bt_im, 0.0)
    bb_im = jnp.where(same, w_re * bt_im + w_im * bt_re, 0.0)
    ct_re = jnp.where(same, ctre_ref[...], 0.0)
    ct_im = jnp.where(same, ctim_ref[...], 0.0)
    hp = lax.Precision.HIGHEST
    kblk = []
    for d in range(S5_T + 1):
        pm = jnp.exp(lre * dt * float(d))
        pa = lim * dt * float(d)
        p_re = pm * jnp.cos(pa)
        p_im = pm * jnp.sin(pa)
        if d < S5_T:
            ab_re = p_re * bb_re - p_im * bb_im
            ab_im = p_re * bb_im + p_im * bb_re
            srow = slice((S5_T - 1 - d) * LANES, (S5_T - d) * LANES)
            vre_ref[0, srow, :] = ab_re.astype(BF16)
            vim_ref[0, srow, :] = ab_im.astype(BF16)
            kblk.append((_dot_nt(ab_re, ct_re, hp) - _dot_nt(ab_im, ct_im, hp)).astype(BF16))
        if d >= 1:
            trow = slice((d - 1) * LANES, d * LANES)
            wre_ref[0, trow, :] = (ct_re * p_re - ct_im * p_im).astype(BF16)
            wim_ref[0, trow, :] = (-(ct_re * p_im + ct_im * p_re)).astype(BF16)
    apre_ref[...] = jnp.zeros_like(apre_ref)
    apim_ref[...] = jnp.zeros_like(apim_ref)
    for k in range(SUBLANES + 1):
        pm = jnp.exp(lre * dt * float(S5_T * k))
        pa = lim * dt * float(S5_T * k)
        apre_ref[k:k + 1, :] = pm * jnp.cos(pa)
        apim_ref[k:k + 1, :] = pm * jnp.sin(pa)
    zero = jnp.zeros((LANES, LANES), BF16)
    for s in range(S5_T):
        for t in range(S5_T):
            toep_ref[0, s * LANES:(s + 1) * LANES, t * LANES:(t + 1) * LANES] = kblk[t - s] if t >= s else zero


def s5_operator_inputs(lam_re, lam_im, log_dt, b_re, b_im, c_re, c_im):
    depth = lam_re.shape[0]
    nst = S5_GROUPS * S5_STATE

    def tiled(m):
        return jnp.tile(m.astype(F32).reshape(depth, S5_WIDTH, S5_STATE), (1, 1, S5_GPB))

    lre = lam_re.astype(F32).reshape(depth, 1, nst)
    lim = lam_im.astype(F32).reshape(depth, 1, nst)
    ldt = jnp.repeat(log_dt.astype(F32), S5_STATE, axis=-1).reshape(depth, 1, nst)
    return (lre, lim, ldt, tiled(jnp.swapaxes(b_re, -1, -2)), tiled(jnp.swapaxes(b_im, -1, -2)),
            tiled(c_re), tiled(c_im))


def s5_operators(prep_inputs, layer):
    nst = S5_GROUPS * S5_STATE
    lane_in = pl.BlockSpec((None, 1, S5_SB), lambda g: (layer, 0, g))
    par_in = pl.BlockSpec((None, LANES, S5_SB), lambda g: (layer, g, 0))
    pow_out = pl.BlockSpec((2 * SUBLANES, S5_SB), lambda g: (0, g))

    def out_blk(rows, cols):
        return pl.BlockSpec((1, rows, cols), lambda g: (g, 0, 0))

    return pl.pallas_call(
        _s5_prep_kernel,
        out_shape=[jax.ShapeDtypeStruct((S5_GB, S5_KT, S5_KT), BF16)]
        + [jax.ShapeDtypeStruct((S5_GB, S5_KT, S5_SB), BF16)] * 4
        + [jax.ShapeDtypeStruct((2 * SUBLANES, nst), F32)] * 2,
        grid=(S5_GB,),
        in_specs=[lane_in, lane_in, lane_in, par_in, par_in, par_in, par_in],
        out_specs=[out_blk(S5_KT, S5_KT)] + [out_blk(S5_KT, S5_SB)] * 4 + [pow_out, pow_out],
        compiler_params=_cparams(("parallel",)),
    )(*prep_inputs)


def _gelu_tanh(x):
    c = math.sqrt(2.0 / math.pi)
    return 0.5 * x * (1.0 + jnp.tanh(c * (x + 0.044715 * (x * x * x))))


def _s5_kernel(u_ref, toep_ref, vre_ref, vim_ref, wre_ref, wim_ref, apre_ref, apim_ref, d_ref, y_ref,
               car_re, car_im, x_re, x_im, *, tmc):
    rb = pl.program_id(1)

    @pl.when(rb == 0)
    def _():
        car_re[...] = jnp.zeros_like(car_re)
        car_im[...] = jnp.zeros_like(car_im)

    us = [u_ref[pl.ds(s, tmc, stride=S5_T), :] for s in range(S5_T)]
    ucat = jnp.concatenate([u.astype(BF16) for u in us], axis=1)
    y_re = _dot(ucat, vre_ref[0])
    y_im = _dot(ucat, vim_ref[0])

    row_in_tile = lax.broadcasted_iota(jnp.int32, (tmc, 1), 0) & (SUBLANES - 1)

    def shifted(v, s):
        return jnp.where(row_in_tile >= s, pltpu.roll(v, s, 0), 0.0)

    for s in (1, 2, 4):
        a_r = apre_ref[s:s + 1, :]
        a_i = apim_ref[s:s + 1, :]
        s_re, s_im = shifted(y_re, s), shifted(y_im, s)
        y_re, y_im = y_re + a_r * s_re - a_i * s_im, y_im + a_r * s_im + a_i * s_re
    e_re, e_im = shifted(y_re, 1), shifted(y_im, 1)
    p_re = apre_ref[0:SUBLANES, :]
    p_im = apim_ref[0:SUBLANES, :]
    a8_r = apre_ref[SUBLANES:SUBLANES + 1, :]
    a8_i = apim_ref[SUBLANES:SUBLANES + 1, :]
    xr = car_re[0:1, :]
    xi = car_im[0:1, :]
    for t in range(tmc // SUBLANES):
        rows = slice(t * SUBLANES, (t + 1) * SUBLANES)
        x_re[rows, :] = e_re[rows] + p_re * xr - p_im * xi
        x_im[rows, :] = e_im[rows] + p_re * xi + p_im * xr
        last = (t + 1) * SUBLANES - 1
        xr, xi = (a8_r * xr - a8_i * xi + y_re[last:last + 1], a8_r * xi + a8_i * xr + y_im[last:last + 1])
    car_re[0:1, :] = xr
    car_im[0:1, :] = xi
    y = (_dot(ucat, toep_ref[0]) + _dot_nt(x_re[...].astype(BF16), wre_ref[0])
         + _dot_nt(x_im[...].astype(BF16), wim_ref[0]))
    for t in range(S5_T):
        yt = y[:, t * LANES:(t + 1) * LANES] + d_ref[...] * us[t]
        y_ref[pl.ds(t, tmc, stride=S5_T), :] = _gelu_tanh(yt)


def s5_block(tail, ops, d_skip, layer, tb):
    l = tail.shape[0]
    toep, v_re, v_im, w_re, w_im, ap_re, ap_im = ops
    tmc = tb // S5_T

    def per_gb(cols):
        return pl.BlockSpec((1, S5_KT, cols), lambda g, r: (g, 0, 0))

    pow_blk = pl.BlockSpec((2 * SUBLANES, S5_SB), lambda g, r: (0, g))
    fold = pltpu.VMEM((tmc, S5_SB), F32)
    carry = pltpu.VMEM((SUBLANES, S5_SB), F32)
    return pl.pallas_call(
        functools.partial(_s5_kernel, tmc=tmc),
        out_shape=jax.ShapeDtypeStruct((l, S5_WIDTH), F32),
        grid=(S5_GB, l // tb),
        in_specs=[pl.BlockSpec((tb, LANES), lambda g, r: (r, g)),
                  per_gb(S5_KT), per_gb(S5_SB), per_gb(S5_SB), per_gb(S5_SB), per_gb(S5_SB),
                  pow_blk, pow_blk, pl.BlockSpec((None, 1, LANES), lambda g, r: (layer, 0, g))],
        out_specs=pl.BlockSpec((tb, LANES), lambda g, r: (r, g)),
        scratch_shapes=[carry, carry, fold, fold],
        compiler_params=_cparams(("parallel", "arbitrary")),
    )(tail, toep, v_re, v_im, w_re, w_im, ap_re, ap_im, d_skip)


def _layer_norm_by_column_tiles(z_tile, z_buf, lw_ref, lb_ref, o_ref, ob_ref):
    rows = z_buf.shape[0]
    pivot = None
    s1 = jnp.zeros((rows, LANES), F32)
    s2 = jnp.zeros((rows, LANES), F32)
    for n in range(D_MODEL // LN_TILE):
        cols = slice(n * LN_TILE, (n + 1) * LN_TILE)
        z = z_tile(cols)
        z_buf[:, cols] = z
        if pivot is None:
            pivot = jnp.mean(z, axis=-1, keepdims=True)
        for j in range(LN_TILE // LANES):
            dz = z[:, j * LANES:(j + 1) * LANES] - pivot
            s1 = s1 + dz
            s2 = s2 + dz * dz
    m1 = jnp.sum(s1, axis=-1, keepdims=True) * (1.0 / D_MODEL)
    var = jnp.sum(s2, axis=-1, keepdims=True) * (1.0 / D_MODEL) - m1 * m1
    mu = pivot + m1
    rstd = lax.rsqrt(var + LN_EPS)
    for n in range(D_MODEL // LN_TILE):
        cols = slice(n * LN_TILE, (n + 1) * LN_TILE)
        out = (z_buf[:, cols] - mu) * rstd * lw_ref[:, cols] + lb_ref[:, cols]
        o_ref[:, cols] = out
        ob_ref[:, cols] = out.astype(BF16)


def _outproj_kernel(a1_ref, a2_ref, ys_ref, gw_ref, gb_ref, w1_ref, w2_ref, w3_ref, x_ref, lw_ref, lb_ref,
                    o_ref, ob_ref, z_buf):
    a1, a2 = a1_ref[...], a2_ref[...]
    ys = ys_ref[...]
    a3 = (ys * _sigmoid(_dot(ys.astype(BF16), gw_ref[...]) + gb_ref[...])).astype(BF16)

    def z_tile(cols):
        mix = _dot(a1, w1_ref[:, cols]) + _dot(a2, w2_ref[:, cols]) + _dot(a3, w3_ref[:, cols])
        return ALPHA * x_ref[:, cols] + mix

    _layer_norm_by_column_tiles(z_tile, z_buf, lw_ref, lb_ref, o_ref, ob_ref)


def outproj_block(y_ret, y_m, y_s, glu_w, glu_b, w_out, x, ln_w, ln_b, layer, tm):
    l = x.shape[0]

    def rows(n):
        return pl.BlockSpec((tm, n), lambda i: (i, 0))

    def wrows(n, blk):
        return pl.BlockSpec((None, n, D_MODEL), lambda i: (layer, blk, 0))

    return pl.pallas_call(
        _outproj_kernel,
        out_shape=[jax.ShapeDtypeStruct((l, D_MODEL), F32), jax.ShapeDtypeStruct((l, D_MODEL), BF16)],
        grid=(l // tm,),
        in_specs=[rows(RET_WIDTH), rows(MLSTM_WIDTH), rows(S5_WIDTH),
                  _layer_spec(glu_w, layer), _layer_spec(glu_b, layer), wrows(RET_WIDTH, 0), wrows(MLSTM_WIDTH, 1), wrows(S5_WIDTH, (RET_WIDTH + MLSTM_WIDTH) // S5_WIDTH),
                  rows(D_MODEL), _layer_spec(ln_w, layer), _layer_spec(ln_b, layer)],
        out_specs=[rows(D_MODEL), rows(D_MODEL)],
        scratch_shapes=[pltpu.VMEM((tm, D_MODEL), F32)],
        compiler_params=_cparams(("parallel",)),
    )(y_ret, y_m, y_s, glu_w, glu_b, w_out, w_out, w_out, x, ln_w, ln_b)


def _down_kernel(h_ref, w_ref, x1_ref, r_ref, rb_ref):
    r = ALPHA * x1_ref[...] + _dot(h_ref[...], w_ref[...])
    r_ref[...] = r
    rb_ref[...] = r.astype(BF16)


def down_block(hid, w_down, x1, layer, tm):
    l, ff = hid.shape
    rows = pl.BlockSpec((tm, D_MODEL), lambda i: (i, 0))
    return pl.pallas_call(
        _down_kernel,
        out_shape=[jax.ShapeDtypeStruct((l, D_MODEL), F32), jax.ShapeDtypeStruct((l, D_MODEL), BF16)],
        grid=(l // tm,),
        in_specs=[pl.BlockSpec((tm, ff), lambda i: (i, 0)),
                  pl.BlockSpec((None, ff, D_MODEL), lambda i: (layer, 0, 0), pipeline_mode=pl.Buffered(1)), rows],
        out_specs=[rows, rows],
        compiler_params=_cparams(("parallel",)),
    )(hid, w_down, x1)


def _final_kernel(r_ref, rb_ref, p_ref, wg_ref, wp_ref, lw_ref, lb_ref, o_ref, ob_ref, z_buf):
    rb, pb = rb_ref[...], p_ref[...]

    def z_tile(cols):
        gate = _sigmoid(_dot(rb, wg_ref[:, cols]))
        return r_ref[:, cols] + gate * _dot(pb, wp_ref[:, cols])

    _layer_norm_by_column_tiles(z_tile, z_buf, lw_ref, lb_ref, o_ref, ob_ref)


def final_block(r, rb, p_b, w_gate, w_ple, ln_w, ln_b, layer, tm):
    l = r.shape[0]
    rows = pl.BlockSpec((tm, D_MODEL), lambda i: (i, 0))
    return pl.pallas_call(
        _final_kernel,
        out_shape=[jax.ShapeDtypeStruct((l, D_MODEL), F32), jax.ShapeDtypeStruct((l, D_MODEL), BF16)],
        grid=(l // tm,),
        in_specs=[rows, rows, pl.BlockSpec((None, tm, PLE_DIM), lambda i: (layer, i, 0)),
                  _layer_spec(w_gate, layer), _layer_spec(w_ple, layer), _layer_spec(ln_w, layer),
                  _layer_spec(ln_b, layer)],
        out_specs=[rows, rows],
        scratch_shapes=[pltpu.VMEM((tm, D_MODEL), F32)],
        compiler_params=_cparams(("parallel",)),
    )(r, rb, p_b, w_gate, w_ple, ln_w, ln_b)


def _rows3(v):
    return v.astype(F32).reshape(v.shape[0], 1, -1)


def _tiles(l):
    return {"mm_m": min(1024, l), "tail_m": min(2048, l), "mix": min(512, l), "s5": min(4096, l), "row": min(512, l),
            "down": min(256, l), "in_n": 2 * RET_WIDTH, "up_n": 2048, "rope": min(2048, l)}


def kernel(x, p, positions, w_in, mlstm_conv_w, mlstm_conv_b, mlstm_i_bias, mlstm_f_bias, ret_norm_w, mlstm_norm_w, s5_lambda_re, s5_lambda_im, s5_log_dt, s5_B_re, s5_B_im, s5_C_re, s5_C_im, s5_D, s5_glu_w, s5_glu_b, w_out, ln1_w, ln1_b, w_up, w_down, w_gate, w_ple, ln2_w, ln2_b):
    bsz, l, _ = x.shape
    depth = w_in.shape[0]
    assert bsz == 1
    t = _tiles(l)
    cos_t, sin_t = rope_tables(positions.astype(F32).reshape(l, 1), t["rope"])
    xf = x.reshape(l, D_MODEL).astype(F32)
    xb = None
    p_b = p.reshape(depth, l, PLE_DIM).astype(BF16)

    w_in_t = jnp.transpose(w_in, (2, 0, 1))
    w_in_b = cast_cols_transposed(w_in_t, 0, MAIN_WIDTH, LANES)
    w_tail_b = cast_cols_transposed(w_in_t, MAIN_WIDTH, TAIL_WIDTH, LANES)
    w_out_b, w_up_b, w_down_b = w_out.astype(BF16), w_up.astype(BF16), w_down.astype(BF16)
    w_gate_b, w_ple_b, glu_w_b = w_gate.astype(BF16), w_ple.astype(BF16), s5_glu_w.astype(BF16)
    gate_bias = jnp.concatenate([jnp.zeros((depth, GATE_LANE0), F32), mlstm_i_bias.astype(F32),
                                 mlstm_f_bias.astype(F32)], axis=1).reshape(depth, 1, LANES)
    conv_w = mlstm_conv_w.astype(F32)
    conv_b, ret_nw, mlstm_nw = _rows3(mlstm_conv_b), _rows3(ret_norm_w), _rows3(mlstm_norm_w)
    s5_d, glu_b = _rows3(s5_D), _rows3(s5_glu_b)
    ln1w, ln1b, ln2w, ln2b = _rows3(ln1_w), _rows3(ln1_b), _rows3(ln2_w), _rows3(ln2_b)
    s5_in = s5_operator_inputs(s5_lambda_re, s5_lambda_im, s5_log_dt, s5_B_re, s5_B_im, s5_C_re, s5_C_im)

    for i in range(depth):
        proj, xb = in_projection(xf if xb is None else xb, w_in_b, cos_t, sin_t, i, tm=t["mm_m"], tn=t["in_n"])
        tail = matmul(xb, w_tail_b, i, out_dtype=F32, tm=t["tail_m"], tn=TAIL_WIDTH, act="rotate_tail",
                      b_cols_major=True)
        y_ret = retention_block(proj, ret_nw, i, t["mix"])
        y_m = mlstm_block(proj, tail, conv_w, conv_b, gate_bias, mlstm_nw, i, t["mix"])
        y_s = s5_block(tail, s5_operators(s5_in, i), s5_d, i, t["s5"])
        x1, x1b = outproj_block(y_ret, y_m, y_s, glu_w_b, glu_b, w_out_b, xf, ln1w, ln1b, i, t["row"])
        hid = matmul(x1b, w_up_b, i, out_dtype=BF16, tm=t["mm_m"], tn=t["up_n"], act="relu2")
        r, rb = down_block(hid, w_down_b, x1, i, t["down"])
        xf, xb = final_block(r, rb, p_b, w_gate_b, w_ple_b, ln2w, ln2b, i, t["row"])
    return xf.reshape(bsz, l, D_MODEL)
```

```python
import functools
import math

import jax
import jax.numpy as jnp
import numpy as np
from jax import lax
from jax.experimental import pallas as pl
from jax.experimental.pallas import tpu as pltpu

F32 = jnp.float32
BF16 = jnp.bfloat16

D_MODEL = 2048
DEPTH = 2
HEAD_DIM = 128
RET_HEADS = 6
MLSTM_HEADS = 6
RET_WIDTH = RET_HEADS * HEAD_DIM
MLSTM_WIDTH = MLSTM_HEADS * HEAD_DIM
S5_WIDTH = D_MODEL - RET_WIDTH - MLSTM_WIDTH
S5_GROUP = 16
S5_GROUPS = S5_WIDTH // S5_GROUP
S5_STATE = 64
CONV_WIDTH = 4
CHUNK = 128
PLE_DIM = 256
ROPE_BASE = 10000.0
LN_EPS = 1e-5
ALPHA = (2 * DEPTH) ** 0.25
QK_SCALE = HEAD_DIM ** -0.5

LANES = 128
SUBLANES = 8
MAIN_WIDTH = 4 * RET_WIDTH + 4 * MLSTM_WIDTH
TAIL_COLS = 2 * MLSTM_HEADS + S5_WIDTH
TAIL_WIDTH = 5 * LANES
GATE_LANE0 = LANES - 2 * MLSTM_HEADS
GATE_COLBLK = S5_WIDTH // LANES
ROT_COL_TILE = 0
ROT_GROUP = 256
LN_TILE = 256
CONV_ROWS = 64
S5_T = 16
S5_TGROUP = 4
S5_GB = S5_WIDTH // LANES
S5_GPB = LANES // S5_GROUP
S5_SB = S5_GPB * S5_STATE
S5_KT = S5_T * LANES
VMEM_LIMIT = 56 * 1024 * 1024


def _cparams(sem):
    return pltpu.CompilerParams(dimension_semantics=sem, vmem_limit_bytes=VMEM_LIMIT)


def _mm_kernel(a_ref, b_ref, o_ref, *scratch, nk, act, b_cols_major):
    def finish(r):
        if act == "relu2":
            r = jnp.square(jnp.maximum(r, 0.0))
        elif act == "rotate_tail":
            r = pltpu.roll(r, TAIL_WIDTH - 2 * MLSTM_HEADS, 1)
        o_ref[...] = r.astype(o_ref.dtype)

    if nk == 1:
        finish(_dot_nt(a_ref[...], b_ref[...]) if b_cols_major else _dot(a_ref[...], b_ref[...]))
        return
    acc_ref, = scratch
    k = pl.program_id(2)

    @pl.when(k == 0)
    def _():
        acc_ref[...] = jnp.zeros_like(acc_ref)

    acc_ref[...] += _dot(a_ref[...], b_ref[...])

    @pl.when(k == nk - 1)
    def _():
        finish(acc_ref[...])


def matmul(a, b, layer, *, out_dtype, tm, tn, tk=None, act=None, b_cols_major=False):
    m, kdim = a.shape
    n = b.shape[1] if b_cols_major else b.shape[2]
    tk = kdim if tk is None else tk
    nk = kdim // tk
    assert m % tm == 0 and n % tn == 0 and kdim % tk == 0 and b.shape[2 if b_cols_major else 1] == kdim
    assert nk == 1 or not b_cols_major
    b_spec = (pl.BlockSpec((None, tn, tk), lambda i, j, k: (layer, j, k)) if b_cols_major
              else pl.BlockSpec((None, tk, tn), lambda i, j, k: (layer, k, j)))
    return pl.pallas_call(
        functools.partial(_mm_kernel, nk=nk, act=act, b_cols_major=b_cols_major),
        out_shape=jax.ShapeDtypeStruct((m, n), out_dtype),
        grid=(m // tm, n // tn, nk),
        in_specs=[pl.BlockSpec((tm, tk), lambda i, j, k: (i, k)), b_spec],
        out_specs=pl.BlockSpec((tm, tn), lambda i, j, k: (i, j)),
        scratch_shapes=[] if nk == 1 else [pltpu.VMEM((tm, tn), F32)],
        compiler_params=_cparams(("parallel", "parallel", "arbitrary")),
    )(a, b)


def _in_proj_kernel(a_ref, b_ref, cos_ref, sin_ref, *rest, cast_a):
    j = pl.program_id(1)
    if cast_a:
        o_ref, ab_ref = rest

        @pl.when(j == 0)
        def _():
            ab_ref[...] = a_ref[...].astype(BF16)
        a_src = ab_ref
    else:
        o_ref, = rest
        a_src = a_ref

    @pl.when(j != ROT_COL_TILE)
    def _():
        o_ref[...] = _dot_nt(a_src[...], b_ref[...])

    @pl.when(j == ROT_COL_TILE)
    def _():
        a = a_src[...]
        cos = cos_ref[...]
        sin = sin_ref[...]
        for g in range(o_ref.shape[1] // ROT_GROUP):
            z = _dot_nt(a, b_ref[g * ROT_GROUP:(g + 1) * ROT_GROUP, :])
            for h in range(ROT_GROUP // HEAD_DIM):
                c0 = g * ROT_GROUP + h * HEAD_DIM
                t = z[:, h * HEAD_DIM:(h + 1) * HEAD_DIM]
                rot = t * cos + pltpu.roll(t, HEAD_DIM // 2, 1) * sin
                o_ref[:, c0:c0 + HEAD_DIM] = rot * QK_SCALE if c0 >= RET_WIDTH else rot


def in_projection(a, b, cos_t, sin_t, layer, *, tm, tn):
    m, kdim = a.shape
    n = b.shape[1]
    cast_a = a.dtype != BF16
    assert m % tm == 0 and n % tn == 0 and b.shape[2] == kdim and tn == 2 * RET_WIDTH
    a_blk = pl.BlockSpec((tm, kdim), lambda i, j: (i, 0))
    tab = pl.BlockSpec((tm, LANES), lambda i, j: (i, 0))
    out_shape = [jax.ShapeDtypeStruct((m, n), F32)]
    out_specs = [pl.BlockSpec((tm, tn), lambda i, j: (i, j))]
    if cast_a:
        out_shape.append(jax.ShapeDtypeStruct((m, kdim), BF16))
        out_specs.append(a_blk)
    out = pl.pallas_call(
        functools.partial(_in_proj_kernel, cast_a=cast_a),
        out_shape=out_shape,
        grid=(m // tm, n // tn),
        in_specs=[a_blk, pl.BlockSpec((None, tn, kdim), lambda i, j: (layer, j, 0)), tab, tab],
        out_specs=out_specs,
        compiler_params=_cparams(("parallel", "arbitrary")),
    )(a, b, cos_t, sin_t)
    return (out[0], out[1]) if cast_a else (out[0], a)


def _cast_cols_kernel(x_ref, o_ref, *, col0, ncols_total):
    tc = x_ref.shape[0]
    col = col0 + pl.program_id(0) * tc + lax.broadcasted_iota(jnp.int32, (tc, 1), 0)
    for d in range(x_ref.shape[1]):
        o_ref[d] = jnp.where(col < ncols_total, x_ref[:, d, :], 0.0).astype(o_ref.dtype)


def cast_cols_transposed(w_t, col0, ncols, tc):
    ctot, depth, kdim = w_t.shape
    assert col0 % tc == 0 and ncols % tc == 0
    return pl.pallas_call(
        functools.partial(_cast_cols_kernel, col0=col0, ncols_total=ctot),
        out_shape=jax.ShapeDtypeStruct((depth, ncols, kdim), BF16),
        grid=(ncols // tc,),
        in_specs=[pl.BlockSpec((tc, depth, kdim), lambda i: (col0 // tc + i, 0, 0))],
        out_specs=pl.BlockSpec((depth, tc, kdim), lambda i: (0, i, 0)),
        compiler_params=_cparams(("parallel",)),
    )(w_t)


def _rope_kernel(pos_ref, inv_ref, sign_ref, cos_ref, sin_ref):
    ang = pos_ref[...] * inv_ref[...]
    cos_ref[...] = jnp.cos(ang)
    sin_ref[...] = jnp.sin(ang) * sign_ref[...]


def rope_tables(pos_col, tl):
    l = pos_col.shape[0]
    half = np.arange(0, HEAD_DIM, 2, dtype=np.float32) / np.float32(HEAD_DIM)
    inv = (np.float32(ROPE_BASE) ** (-half)).astype(np.float32)
    inv2 = jnp.asarray(np.concatenate([inv, inv])[None, :])
    sign = jnp.asarray(np.concatenate([-np.ones(64, np.float32), np.ones(64, np.float32)])[None, :])
    row = pl.BlockSpec((1, LANES), lambda i: (0, 0))
    return pl.pallas_call(
        _rope_kernel,
        out_shape=[jax.ShapeDtypeStruct((l, LANES), F32)] * 2,
        grid=(l // tl,),
        in_specs=[pl.BlockSpec((tl, 1), lambda i: (i, 0)), row, row],
        out_specs=[pl.BlockSpec((tl, LANES), lambda i: (i, 0))] * 2,
        compiler_params=_cparams(("parallel",)),
    )(pos_col, inv2, sign)


def _head_norm(y):
    mu = jnp.mean(y, axis=-1, keepdims=True)
    d = y - mu
    var = jnp.mean(d * d, axis=-1, keepdims=True)
    return d * lax.rsqrt(var + LN_EPS)


def _sigmoid(x):
    return 1.0 / (1.0 + jnp.exp(-x))


def _dot_nt(a, b, precision=None):
    return lax.dot_general(a, b, (((1,), (1,)), ((), ())), precision=precision, preferred_element_type=F32)


def _dot(a, b):
    return jnp.dot(a, b, preferred_element_type=F32)


def _head_lanes(h):
    return slice(h * HEAD_DIM, (h + 1) * HEAD_DIM)


def _layer_spec(stacked, layer):
    return pl.BlockSpec((None,) + stacked.shape[1:], lambda *_: (layer, 0, 0))


def _ret_log_gamma(h):
    return float(np.log(np.float32(1.0) - np.float32(2.0) ** np.float32(-5.0 - h)))


def _ret_kernel(q_ref, k_ref, v_ref, g_ref, w_ref, o_ref,
                state_ref, decay_ref, qdec_ref, kdec_ref, qb_buf, s_buf, kdt_buf, intra_buf, incr_buf, *, nchunk):
    rb = pl.program_id(0)

    @pl.when(rb == 0)
    def _():
        state_ref[...] = jnp.zeros_like(state_ref)
        ii = lax.broadcasted_iota(jnp.int32, (CHUNK, CHUNK), 0)
        jj = lax.broadcasted_iota(jnp.int32, (CHUNK, CHUNK), 1)
        rel = (ii - jj).astype(F32)
        idx = ii.astype(F32)
        for h in range(RET_HEADS):
            lg = _ret_log_gamma(h)
            decay_ref[h] = jnp.where(rel >= 0.0, jnp.exp(lg * jnp.maximum(rel, 0.0)), 0.0)
            qdec_ref[h] = jnp.exp(lg * (idx + 1.0))
            kdec_ref[h] = jnp.exp(lg * (CHUNK - 1.0 - idx))

    for c in range(nchunk):
        rows = slice(c * CHUNK, (c + 1) * CHUNK)
        for h in range(RET_HEADS):
            ln = _head_lanes(h)
            kr = k_ref[rows, ln]
            qb = q_ref[rows, ln].astype(BF16)
            qb_buf[rows, ln] = qb
            s_buf[rows, ln] = (_dot_nt(qb, kr.astype(BF16)) * decay_ref[h]).astype(BF16)
            kdt_buf[rows, ln] = jnp.transpose((kr * kdec_ref[h]).astype(BF16))

    for c in range(nchunk):
        rows = slice(c * CHUNK, (c + 1) * CHUNK)
        for h in range(RET_HEADS):
            ln = _head_lanes(h)
            lhs = jnp.concatenate([s_buf[rows, ln], kdt_buf[rows, ln]], axis=0)
            both = _dot(lhs, v_ref[rows, ln].astype(BF16))
            intra_buf[rows, ln] = both[:CHUNK]
            incr_buf[c, h] = both[CHUNK:]

    for c in range(nchunk):
        rows = slice(c * CHUNK, (c + 1) * CHUNK)
        for h in range(RET_HEADS):
            ln = _head_lanes(h)
            st = state_ref[h]
            out = intra_buf[rows, ln] + _dot(qb_buf[rows, ln], st.astype(BF16)) * qdec_ref[h]
            state_ref[h] = st * math.exp(_ret_log_gamma(h) * CHUNK) + incr_buf[c, h]
            g = g_ref[rows, ln]
            y = _head_norm(out) * w_ref[:, ln] * (g * _sigmoid(g))
            o_ref[rows, ln] = y.astype(o_ref.dtype)


def retention_block(proj, norm_w, layer, tb):
    l = proj.shape[0]

    def col(j):
        return pl.BlockSpec((tb, RET_WIDTH), lambda r, j=j: (r, j))

    hsq = pltpu.VMEM((RET_HEADS, CHUNK, CHUNK), F32)
    half = pltpu.VMEM((tb, RET_WIDTH), BF16)
    return pl.pallas_call(
        functools.partial(_ret_kernel, nchunk=tb // CHUNK),
        out_shape=jax.ShapeDtypeStruct((l, RET_WIDTH), BF16),
        grid=(l // tb,),
        in_specs=[col(0), col(1), col(2), col(3), _layer_spec(norm_w, layer)],
        out_specs=pl.BlockSpec((tb, RET_WIDTH), lambda r: (r, 0)),
        scratch_shapes=[hsq, hsq, hsq, hsq, half, half, half, pltpu.VMEM((tb, RET_WIDTH), F32),
                        pltpu.VMEM((tb // CHUNK, RET_HEADS, HEAD_DIM, HEAD_DIM), F32)],
        compiler_params=_cparams(("arbitrary",)),
    )(proj, proj, proj, proj, norm_w)


def _log_sigmoid(x):
    return -(jnp.maximum(-x, 0.0) + jnp.log1p(jnp.exp(-jnp.abs(x))))


def _mlstm_kernel(q_ref, k_ref, v_ref, og_ref, gate_ref, cw_ref, cb_ref, gb_ref, nw_ref,
                  o_ref, c_st, m_st, qbuf, kbuf, qs, ks, gs, r_buf, cmax_buf, bcum_buf, mall_buf, floor_buf,
                  p_buf, u_buf, wq_buf, s_buf, kwt_buf, *, nchunk, tb):
    rb = pl.program_id(0)
    pad = SUBLANES
    nh = MLSTM_HEADS

    @pl.when(rb == 0)
    def _():
        c_st[...] = jnp.zeros_like(c_st)
        m_st[...] = jnp.zeros_like(m_st)
        qbuf[0:pad, :] = jnp.zeros((pad, MLSTM_WIDTH), F32)
        kbuf[0:pad, :] = jnp.zeros((pad, MLSTM_WIDTH), F32)

    qbuf[pad:pad + tb, :] = q_ref[...]
    kbuf[pad:pad + tb, :] = k_ref[...]
    for r0 in range(0, tb, CONV_ROWS):
        for buf, woff, dst, scale in ((qbuf, 0, qs, None), (kbuf, MLSTM_WIDTH, ks, QK_SCALE)):
            wl = slice(woff, woff + MLSTM_WIDTH)
            acc = jnp.broadcast_to(cb_ref[:, wl], (CONV_ROWS, MLSTM_WIDTH))
            for tap in range(CONV_WIDTH):
                off = r0 + pad - (CONV_WIDTH - 1) + tap
                acc = acc + buf[off:off + CONV_ROWS, :] * cw_ref[tap:tap + 1, wl]
            act = acc * _sigmoid(acc)
            dst[r0:r0 + CONV_ROWS, :] = act if scale is None else act * scale
    qbuf[0:pad, :] = q_ref[tb - pad:tb, :]
    kbuf[0:pad, :] = k_ref[tb - pad:tb, :]

    lane_row = lax.broadcasted_iota(jnp.int32, (1, LANES), 1)
    graw = pltpu.roll(gate_ref[...] + gb_ref[...], LANES - GATE_LANE0, 1)
    gs[...] = jnp.where(lane_row < nh, graw, jnp.where(lane_row < 2 * nh, _log_sigmoid(graw), 0.0))

    ii = lax.broadcasted_iota(jnp.int32, (CHUNK, CHUNK), 0)
    jj = lax.broadcasted_iota(jnp.int32, (CHUNK, CHUNK), 1)
    causal = ii >= jj
    tril = causal.astype(F32)
    lane8 = lax.broadcasted_iota(jnp.int32, (SUBLANES, LANES), 1)
    ones_b = jnp.ones((CHUNK, HEAD_DIM), BF16)
    full = (CHUNK, CHUNK)

    def chunk_rows(c):
        return slice(c * CHUNK, (c + 1) * CHUNK)

    def aug_lanes(h):
        return slice(h * 2 * HEAD_DIM, (h + 1) * 2 * HEAD_DIM)

    for c in range(nchunk):
        rows = chunk_rows(c)
        g = gs[rows, :]
        gcum = jnp.dot(tril, g, precision=lax.Precision.HIGHEST, preferred_element_type=F32)
        bcum = pltpu.roll(gcum, LANES - nh, 1)
        r8 = jnp.transpose(g - bcum)[0:SUBLANES, :]
        cmax = r8
        for sh in (1, 2, 4, 8, 16, 32, 64):
            cmax = jnp.maximum(cmax, jnp.where(lane8 >= sh, pltpu.roll(cmax, sh, 1), -jnp.inf))
        r_buf[c] = r8
        cmax_buf[rows, :] = jnp.transpose(
            jnp.concatenate([cmax, jnp.zeros((CHUNK - SUBLANES, LANES), F32)], axis=0))
        bcum_buf[rows, :] = bcum

    m_prev = m_st[0:1, :]
    m_prevs, keep_rows = [], []
    for c in range(nchunk):
        rows = chunk_rows(c)
        m_all = jnp.maximum(cmax_buf[rows, :], m_prev)
        floor_buf[rows, :] = jnp.exp(-(bcum_buf[rows, :] + m_all))
        mall_buf[rows, :] = m_all
        m_last = m_all[CHUNK - 1:CHUNK, :]
        m_prevs.append(m_prev)
        keep_rows.append(jnp.exp(m_prev - m_last))
        m_prev = bcum_buf[c * CHUNK + CHUNK - 1:(c + 1) * CHUNK, :] + m_last
    m_st[0:1, :] = m_prev

    for c in range(nchunk):
        rows = chunk_rows(c)
        for h in range(nh):
            ln = _head_lanes(h)
            qc = qs[rows, ln]
            kb = ks[rows, ln].astype(BF16)
            m_bc = jnp.broadcast_to(mall_buf[rows, h:h + 1], full)
            dmat = jnp.exp(jnp.where(causal, r_buf[c, h:h + 1, :] - m_bc, -jnp.inf))
            s_buf[rows, ln] = (_dot_nt(qc.astype(BF16), kb) * dmat).astype(BF16)
            kwt_buf[rows, ln] = (jnp.transpose(kb).astype(F32) * dmat[CHUNK - 1:CHUNK, :]).astype(BF16)
            wq_buf[rows, ln] = (jnp.exp(m_prevs[c][:, h:h + 1] - m_bc) * qc).astype(BF16)

    for c in range(nchunk):
        rows = chunk_rows(c)
        for h in range(nh):
            ln = _head_lanes(h)
            v_aug = jnp.concatenate([v_ref[rows, ln].astype(BF16), ones_b], axis=1)
            both = _dot(jnp.concatenate([s_buf[rows, ln], kwt_buf[rows, ln]], axis=0), v_aug)
            p_buf[rows, aug_lanes(h)] = both[:CHUNK]
            u_buf[c, h] = both[CHUNK:]

    for c in range(nchunk):
        rows = chunk_rows(c)
        for h in range(nh):
            ln = _head_lanes(h)
            cs = c_st[h]
            res = p_buf[rows, aug_lanes(h)] + _dot(wq_buf[rows, ln], cs.astype(BF16))
            c_st[h] = keep_rows[c][:, h:h + 1] * cs + u_buf[c, h]
            den = res[:, HEAD_DIM + h:HEAD_DIM + h + 1]
            inv = 1.0 / jnp.maximum(jnp.abs(den), floor_buf[rows, h:h + 1])
            y = _head_norm(res[:, :HEAD_DIM] * inv) * nw_ref[:, ln] * _sigmoid(og_ref[rows, ln])
            o_ref[rows, ln] = y.astype(o_ref.dtype)


def mlstm_block(proj, tail, conv_w, conv_b, gate_bias, norm_w, layer, tb):
    l = proj.shape[0]
    base = (4 * RET_WIDTH) // MLSTM_WIDTH

    def col(j):
        return pl.BlockSpec((tb, MLSTM_WIDTH), lambda r, j=j: (r, base + j))

    nchunk = tb // CHUNK
    wide = pltpu.VMEM((tb, MLSTM_WIDTH), F32)
    hist = pltpu.VMEM((tb + SUBLANES, MLSTM_WIDTH), F32)
    narrow = pltpu.VMEM((tb, LANES), F32)
    half = pltpu.VMEM((tb, MLSTM_WIDTH), BF16)
    rows8 = pltpu.VMEM((nchunk, SUBLANES, LANES), F32)
    return pl.pallas_call(
        functools.partial(_mlstm_kernel, nchunk=nchunk, tb=tb),
        out_shape=jax.ShapeDtypeStruct((l, MLSTM_WIDTH), BF16),
        grid=(l // tb,),
        in_specs=[col(0), col(1), col(2), col(3),
                  pl.BlockSpec((tb, LANES), lambda r: (r, GATE_COLBLK)),
                  _layer_spec(conv_w, layer), _layer_spec(conv_b, layer), _layer_spec(gate_bias, layer),
                  _layer_spec(norm_w, layer)],
        out_specs=pl.BlockSpec((tb, MLSTM_WIDTH), lambda r: (r, 0)),
        scratch_shapes=[pltpu.VMEM((MLSTM_HEADS, HEAD_DIM, 2 * HEAD_DIM), F32), pltpu.VMEM((SUBLANES, LANES), F32),
                        hist, hist, wide, wide, narrow, rows8, narrow, narrow, narrow, narrow,
                        pltpu.VMEM((tb, 2 * MLSTM_WIDTH), F32),
                        pltpu.VMEM((nchunk, MLSTM_HEADS, HEAD_DIM, 2 * HEAD_DIM), F32),
                        half, half, half],
        compiler_params=_cparams(("arbitrary",)),
    )(proj, proj, proj, proj, tail, conv_w, conv_b, gate_bias, norm_w)


def _s5_prep_kernel(lre_ref, lim_ref, ldt_ref, btre_ref, btim_ref, ctre_ref, ctim_ref,
                    toep_ref, vre_ref, vim_ref, wre_ref, wim_ref, apre_ref, apim_ref):
    lre = lre_ref[...]
    lim = lim_ref[...]
    dt = jnp.exp(ldt_ref[...])
    mag = jnp.exp(lre * dt)
    ang = lim * dt
    zr = mag * jnp.cos(ang) - 1.0
    zi = mag * jnp.sin(ang)
    den = lre * lre + lim * lim
    w_re = (zr * lre + zi * lim) / den
    w_im = (zi * lre - zr * lim) / den
    row_g = lax.broadcasted_iota(jnp.int32, (LANES, S5_SB), 0) // S5_GROUP
    col_g = lax.broadcasted_iota(jnp.int32, (LANES, S5_SB), 1) // S5_STATE
    same = row_g == col_g
    bt_re = btre_ref[...]
    bt_im = btim_ref[...]
    bb_re = jnp.where(same, w_re * bt_re - w_im * bt_im, 0.0)
    bb_im = jnp.where(same, w_re * bt_im + w_im * bt_re, 0.0)
    ct_re = jnp.where(same, ctre_ref[...], 0.0)
    ct_im = jnp.where(same, ctim_ref[...], 0.0)
    hp = lax.Precision.HIGHEST
    kblk = []
    for d in range(S5_T + 1):
        pm = jnp.exp(lre * dt * float(d))
        pa = lim * dt * float(d)
        p_re = pm * jnp.cos(pa)
        p_im = pm * jnp.sin(pa)
        if d < S5_T:
            ab_re = p_re * bb_re - p_im * bb_im
            ab_im = p_re * bb_im + p_im * bb_re
            srow = slice((S5_T - 1 - d) * LANES, (S5_T - d) * LANES)
            vre_ref[0, srow, :] = ab_re.astype(BF16)
            vim_ref[0, srow, :] = ab_im.astype(BF16)
            kblk.append((_dot_nt(ab_re, ct_re, hp) - _dot_nt(ab_im, ct_im, hp)).astype(BF16))
        if d >= 1:
            trow = slice((d - 1) * LANES, d * LANES)
            wre_ref[0, trow, :] = (ct_re * p_re - ct_im * p_im).astype(BF16)
            wim_ref[0, trow, :] = (-(ct_re * p_im + ct_im * p_re)).astype(BF16)
    apre_ref[...] = jnp.zeros_like(apre_ref)
    apim_ref[...] = jnp.zeros_like(apim_ref)
    for k in range(SUBLANES + 1):
        pm = jnp.exp(lre * dt * float(S5_T * k))
        pa = lim * dt * float(S5_T * k)
        apre_ref[k:k + 1, :] = pm * jnp.cos(pa)
        apim_ref[k:k + 1, :] = pm * jnp.sin(pa)
    zero = jnp.zeros((LANES, LANES), BF16)
    for s in range(S5_T):
        for t in range(S5_T):
            toep_ref[0, s * LANES:(s + 1) * LANES, t * LANES:(t + 1) * LANES] = kblk[t - s] if t >= s else zero


def s5_operator_inputs(lam_re, lam_im, log_dt, b_re, b_im, c_re, c_im):
    depth = lam_re.shape[0]
    nst = S5_GROUPS * S5_STATE

    def tiled(m):
        return jnp.tile(m.astype(F32).reshape(depth, S5_WIDTH, S5_STATE), (1, 1, S5_GPB))

    lre = lam_re.astype(F32).reshape(depth, 1, nst)
    lim = lam_im.astype(F32).reshape(depth, 1, nst)
    ldt = jnp.repeat(log_dt.astype(F32), S5_STATE, axis=-1).reshape(depth, 1, nst)
    return (lre, lim, ldt, tiled(jnp.swapaxes(b_re, -1, -2)), tiled(jnp.swapaxes(b_im, -1, -2)),
            tiled(c_re), tiled(c_im))


def s5_operators(prep_inputs, layer):
    nst = S5_GROUPS * S5_STATE
    lane_in = pl.BlockSpec((None, 1, S5_SB), lambda g: (layer, 0, g))
    par_in = pl.BlockSpec((None, LANES, S5_SB), lambda g: (layer, g, 0))
    pow_out = pl.BlockSpec((2 * SUBLANES, S5_SB), lambda g: (0, g))

    def out_blk(rows, cols):
        return pl.BlockSpec((1, rows, cols), lambda g: (g, 0, 0))

    return pl.pallas_call(
        _s5_prep_kernel,
        out_shape=[jax.ShapeDtypeStruct((S5_GB, S5_KT, S5_KT), BF16)]
        + [jax.ShapeDtypeStruct((S5_GB, S5_KT, S5_SB), BF16)] * 4
        + [jax.ShapeDtypeStruct((2 * SUBLANES, nst), F32)] * 2,
        grid=(S5_GB,),
        in_specs=[lane_in, lane_in, lane_in, par_in, par_in, par_in, par_in],
        out_specs=[out_blk(S5_KT, S5_KT)] + [out_blk(S5_KT, S5_SB)] * 4 + [pow_out, pow_out],
        compiler_params=_cparams(("parallel",)),
    )(*prep_inputs)


def _gelu_tanh(x):
    c = math.sqrt(2.0 / math.pi)
    return 0.5 * x * (1.0 + jnp.tanh(c * (x + 0.044715 * (x * x * x))))


def _s5_kernel(u_ref, toep_ref, vre_ref, vim_ref, wre_ref, wim_ref, apre_ref, apim_ref, d_ref, y_ref,
               car_re, car_im, x_re, x_im, *, tmc):
    rb = pl.program_id(1)

    @pl.when(rb == 0)
    def _():
        car_re[...] = jnp.zeros_like(car_re)
        car_im[...] = jnp.zeros_like(car_im)

    us = [u_ref[pl.ds(s, tmc, stride=S5_T), :] for s in range(S5_T)]
    ucat = jnp.concatenate([u.astype(BF16) for u in us], axis=1)
    y_re = _dot(ucat, vre_ref[0])
    y_im = _dot(ucat, vim_ref[0])

    row_in_tile = lax.broadcasted_iota(jnp.int32, (tmc, 1), 0) & (SUBLANES - 1)

    def shifted(v, s):
        return jnp.where(row_in_tile >= s, pltpu.roll(v, s, 0), 0.0)

    for s in (1, 2, 4):
        a_r = apre_ref[s:s + 1, :]
        a_i = apim_ref[s:s + 1, :]
        s_re, s_im = shifted(y_re, s), shifted(y_im, s)
        y_re, y_im = y_re + a_r * s_re - a_i * s_im, y_im + a_r * s_im + a_i * s_re
    e_re, e_im = shifted(y_re, 1), shifted(y_im, 1)
    p_re = apre_ref[0:SUBLANES, :]
    p_im = apim_ref[0:SUBLANES, :]
    a8_r = apre_ref[SUBLANES:SUBLANES + 1, :]
    a8_i = apim_ref[SUBLANES:SUBLANES + 1, :]
    xr = car_re[0:1, :]
    xi = car_im[0:1, :]
    for t in range(tmc // SUBLANES):
        rows = slice(t * SUBLANES, (t + 1) * SUBLANES)
        x_re[rows, :] = e_re[rows] + p_re * xr - p_im * xi
        x_im[rows, :] = e_im[rows] + p_re * xi + p_im * xr
        last = (t + 1) * SUBLANES - 1
        xr, xi = (a8_r * xr - a8_i * xi + y_re[last:last + 1], a8_r * xi + a8_i * xr + y_im[last:last + 1])
    car_re[0:1, :] = xr
    car_im[0:1, :] = xi
    xb_re = x_re[...].astype(BF16)
    xb_im = x_im[...].astype(BF16)
    for g in range(S5_T // S5_TGROUP):
        kin = (g + 1) * S5_TGROUP * LANES
        cols = slice(g * S5_TGROUP * LANES, kin)
        y = (_dot(ucat[:, :kin], toep_ref[0, :kin, cols]) + _dot_nt(xb_re, wre_ref[0, cols, :])
             + _dot_nt(xb_im, wim_ref[0, cols, :]))
        for j in range(S5_TGROUP):
            t = g * S5_TGROUP + j
            yt = y[:, j * LANES:(j + 1) * LANES] + d_ref[...] * us[t]
            y_ref[pl.ds(t, tmc, stride=S5_T), :] = _gelu_tanh(yt)


def s5_block(tail, ops, d_skip, layer, tb):
    l = tail.shape[0]
    toep, v_re, v_im, w_re, w_im, ap_re, ap_im = ops
    tmc = tb // S5_T

    def per_gb(cols):
        return pl.BlockSpec((1, S5_KT, cols), lambda g, r: (g, 0, 0))

    pow_blk = pl.BlockSpec((2 * SUBLANES, S5_SB), lambda g, r: (0, g))
    fold = pltpu.VMEM((tmc, S5_SB), F32)
    carry = pltpu.VMEM((SUBLANES, S5_SB), F32)
    return pl.pallas_call(
        functools.partial(_s5_kernel, tmc=tmc),
        out_shape=jax.ShapeDtypeStruct((l, S5_WIDTH), F32),
        grid=(S5_GB, l // tb),
        in_specs=[pl.BlockSpec((tb, LANES), lambda g, r: (r, g)),
                  per_gb(S5_KT), per_gb(S5_SB), per_gb(S5_SB), per_gb(S5_SB), per_gb(S5_SB),
                  pow_blk, pow_blk, pl.BlockSpec((None, 1, LANES), lambda g, r: (layer, 0, g))],
        out_specs=pl.BlockSpec((tb, LANES), lambda g, r: (r, g)),
        scratch_shapes=[carry, carry, fold, fold],
        compiler_params=_cparams(("parallel", "arbitrary")),
    )(tail, toep, v_re, v_im, w_re, w_im, ap_re, ap_im, d_skip)


def _layer_norm_by_column_tiles(z_tile, z_buf, lw_ref, lb_ref, o_ref, ob_ref):
    rows = z_buf.shape[0]
    pivot = None
    s1 = jnp.zeros((rows, LANES), F32)
    s2 = jnp.zeros((rows, LANES), F32)
    for n in range(D_MODEL // LN_TILE):
        cols = slice(n * LN_TILE, (n + 1) * LN_TILE)
        z = z_tile(cols)
        z_buf[:, cols] = z
        if pivot is None:
            pivot = jnp.mean(z, axis=-1, keepdims=True)
        for j in range(LN_TILE // LANES):
            dz = z[:, j * LANES:(j + 1) * LANES] - pivot
            s1 = s1 + dz
            s2 = s2 + dz * dz
    m1 = jnp.sum(s1, axis=-1, keepdims=True) * (1.0 / D_MODEL)
    var = jnp.sum(s2, axis=-1, keepdims=True) * (1.0 / D_MODEL) - m1 * m1
    mu = pivot + m1
    rstd = lax.rsqrt(var + LN_EPS)
    for n in range(D_MODEL // LN_TILE):
        cols = slice(n * LN_TILE, (n + 1) * LN_TILE)
        out = (z_buf[:, cols] - mu) * rstd * lw_ref[:, cols] + lb_ref[:, cols]
        o_ref[:, cols] = out
        ob_ref[:, cols] = out.astype(BF16)


def _outproj_kernel(a1_ref, a2_ref, ys_ref, gw_ref, gb_ref, w1_ref, w2_ref, w3_ref, x_ref, lw_ref, lb_ref,
                    o_ref, ob_ref, z_buf):
    a1, a2 = a1_ref[...], a2_ref[...]
    ys = ys_ref[...]
    a3 = (ys * _sigmoid(_dot(ys.astype(BF16), gw_ref[...]) + gb_ref[...])).astype(BF16)

    def z_tile(cols):
        mix = _dot(a1, w1_ref[:, cols]) + _dot(a2, w2_ref[:, cols]) + _dot(a3, w3_ref[:, cols])
        return ALPHA * x_ref[:, cols] + mix

    _layer_norm_by_column_tiles(z_tile, z_buf, lw_ref, lb_ref, o_ref, ob_ref)


def outproj_block(y_ret, y_m, y_s, glu_w, glu_b, w_out, x, ln_w, ln_b, layer, tm):
    l = x.shape[0]

    def rows(n):
        return pl.BlockSpec((tm, n), lambda i: (i, 0))

    def wrows(n, blk):
        return pl.BlockSpec((None, n, D_MODEL), lambda i: (layer, blk, 0))

    return pl.pallas_call(
        _outproj_kernel,
        out_shape=[jax.ShapeDtypeStruct((l, D_MODEL), F32), jax.ShapeDtypeStruct((l, D_MODEL), BF16)],
        grid=(l // tm,),
        in_specs=[rows(RET_WIDTH), rows(MLSTM_WIDTH), rows(S5_WIDTH),
                  _layer_spec(glu_w, layer), _layer_spec(glu_b, layer), wrows(RET_WIDTH, 0), wrows(MLSTM_WIDTH, 1), wrows(S5_WIDTH, (RET_WIDTH + MLSTM_WIDTH) // S5_WIDTH),
                  rows(D_MODEL), _layer_spec(ln_w, layer), _layer_spec(ln_b, layer)],
        out_specs=[rows(D_MODEL), rows(D_MODEL)],
        scratch_shapes=[pltpu.VMEM((tm, D_MODEL), F32)],
        compiler_params=_cparams(("parallel",)),
    )(y_ret, y_m, y_s, glu_w, glu_b, w_out, w_out, w_out, x, ln_w, ln_b)


def _down_kernel(h_ref, w_ref, x1_ref, r_ref, rb_ref):
    r = ALPHA * x1_ref[...] + _dot(h_ref[...], w_ref[...])
    r_ref[...] = r
    rb_ref[...] = r.astype(BF16)


def down_block(hid, w_down, x1, layer, tm):
    l, ff = hid.shape
    rows = pl.BlockSpec((tm, D_MODEL), lambda i: (i, 0))
    return pl.pallas_call(
        _down_kernel,
        out_shape=[jax.ShapeDtypeStruct((l, D_MODEL), F32), jax.ShapeDtypeStruct((l, D_MODEL), BF16)],
        grid=(l // tm,),
        in_specs=[pl.BlockSpec((tm, ff), lambda i: (i, 0)),
                  pl.BlockSpec((None, ff, D_MODEL), lambda i: (layer, 0, 0), pipeline_mode=pl.Buffered(1)), rows],
        out_specs=[rows, rows],
        compiler_params=_cparams(("parallel",)),
    )(hid, w_down, x1)


def _final_kernel(r_ref, rb_ref, p_ref, wg_ref, wp_ref, lw_ref, lb_ref, o_ref, ob_ref, z_buf):
    rb, pb = rb_ref[...], p_ref[...]

    def z_tile(cols):
        gate = _sigmoid(_dot(rb, wg_ref[:, cols]))
        return r_ref[:, cols] + gate * _dot(pb, wp_ref[:, cols])

    _layer_norm_by_column_tiles(z_tile, z_buf, lw_ref, lb_ref, o_ref, ob_ref)


def final_block(r, rb, p_b, w_gate, w_ple, ln_w, ln_b, layer, tm):
    l = r.shape[0]
    rows = pl.BlockSpec((tm, D_MODEL), lambda i: (i, 0))
    return pl.pallas_call(
        _final_kernel,
        out_shape=[jax.ShapeDtypeStruct((l, D_MODEL), F32), jax.ShapeDtypeStruct((l, D_MODEL), BF16)],
        grid=(l // tm,),
        in_specs=[rows, rows, pl.BlockSpec((None, tm, PLE_DIM), lambda i: (layer, i, 0)),
                  _layer_spec(w_gate, layer), _layer_spec(w_ple, layer), _layer_spec(ln_w, layer),
                  _layer_spec(ln_b, layer)],
        out_specs=[rows, rows],
        scratch_shapes=[pltpu.VMEM((tm, D_MODEL), F32)],
        compiler_params=_cparams(("parallel",)),
    )(r, rb, p_b, w_gate, w_ple, ln_w, ln_b)


def _rows3(v):
    return v.astype(F32).reshape(v.shape[0], 1, -1)


def _tiles(l):
    return {"mm_m": min(1024, l), "tail_m": min(2048, l), "mix": min(512, l), "s5": min(4096, l), "row": min(512, l),
            "down": min(256, l), "in_n": 2 * RET_WIDTH, "up_n": 2048, "rope": min(2048, l)}


def kernel(x, p, positions, w_in, mlstm_conv_w, mlstm_conv_b, mlstm_i_bias, mlstm_f_bias, ret_norm_w, mlstm_norm_w, s5_lambda_re, s5_lambda_im, s5_log_dt, s5_B_re, s5_B_im, s5_C_re, s5_C_im, s5_D, s5_glu_w, s5_glu_b, w_out, ln1_w, ln1_b, w_up, w_down, w_gate, w_ple, ln2_w, ln2_b):
    bsz, l, _ = x.shape
    depth = w_in.shape[0]
    assert bsz == 1
    t = _tiles(l)
    cos_t, sin_t = rope_tables(positions.astype(F32).reshape(l, 1), t["rope"])
    xf = x.reshape(l, D_MODEL).astype(F32)
    xb = None
    p_b = p.reshape(depth, l, PLE_DIM).astype(BF16)

    w_in_t = jnp.transpose(w_in, (2, 0, 1))
    w_in_b = cast_cols_transposed(w_in_t, 0, MAIN_WIDTH, LANES)
    w_tail_b = cast_cols_transposed(w_in_t, MAIN_WIDTH, TAIL_WIDTH, LANES)
    w_out_b, w_up_b, w_down_b = w_out.astype(BF16), w_up.astype(BF16), w_down.astype(BF16)
    w_gate_b, w_ple_b, glu_w_b = w_gate.astype(BF16), w_ple.astype(BF16), s5_glu_w.astype(BF16)
    gate_bias = jnp.concatenate([jnp.zeros((depth, GATE_LANE0), F32), mlstm_i_bias.astype(F32),
                                 mlstm_f_bias.astype(F32)], axis=1).reshape(depth, 1, LANES)
    conv_w = mlstm_conv_w.astype(F32)
    conv_b, ret_nw, mlstm_nw = _rows3(mlstm_conv_b), _rows3(ret_norm_w), _rows3(mlstm_norm_w)
    s5_d, glu_b = _rows3(s5_D), _rows3(s5_glu_b)
    ln1w, ln1b, ln2w, ln2b = _rows3(ln1_w), _rows3(ln1_b), _rows3(ln2_w), _rows3(ln2_b)
    s5_in = s5_operator_inputs(s5_lambda_re, s5_lambda_im, s5_log_dt, s5_B_re, s5_B_im, s5_C_re, s5_C_im)

    for i in range(depth):
        proj, xb = in_projection(xf if xb is None else xb, w_in_b, cos_t, sin_t, i, tm=t["mm_m"], tn=t["in_n"])
        tail = matmul(xb, w_tail_b, i, out_dtype=F32, tm=t["tail_m"], tn=TAIL_WIDTH, act="rotate_tail",
                      b_cols_major=True)
        y_ret = retention_block(proj, ret_nw, i, t["mix"])
        y_m = mlstm_block(proj, tail, conv_w, conv_b, gate_bias, mlstm_nw, i, t["mix"])
        y_s = s5_block(tail, s5_operators(s5_in, i), s5_d, i, t["s5"])
        x1, x1b = outproj_block(y_ret, y_m, y_s, glu_w_b, glu_b, w_out_b, xf, ln1w, ln1b, i, t["row"])
        hid = matmul(x1b, w_up_b, i, out_dtype=BF16, tm=t["mm_m"], tn=t["up_n"], act="relu2")
        r, rb = down_block(hid, w_down_b, x1, i, t["down"])
        xf, xb = final_block(r, rb, p_b, w_gate_b, w_ple_b, ln2w, ln2b, i, t["row"])
    return xf.reshape(bsz, l, D_MODEL)
```

```python
import functools
import math

import jax
import jax.numpy as jnp
import numpy as np
from jax import lax
from jax.experimental import pallas as pl
from jax.experimental.pallas import tpu as pltpu

F32 = jnp.float32
BF16 = jnp.bfloat16

D_MODEL = 2048
DEPTH = 2
HEAD_DIM = 128
RET_HEADS = 6
MLSTM_HEADS = 6
RET_WIDTH = RET_HEADS * HEAD_DIM
MLSTM_WIDTH = MLSTM_HEADS * HEAD_DIM
S5_WIDTH = D_MODEL - RET_WIDTH - MLSTM_WIDTH
S5_GROUP = 16
S5_GROUPS = S5_WIDTH // S5_GROUP
S5_STATE = 64
CONV_WIDTH = 4
CHUNK = 128
PLE_DIM = 256
ROPE_BASE = 10000.0
LN_EPS = 1e-5
ALPHA = (2 * DEPTH) ** 0.25
QK_SCALE = HEAD_DIM ** -0.5

LANES = 128
SUBLANES = 8
MAIN_WIDTH = 4 * RET_WIDTH + 4 * MLSTM_WIDTH
TAIL_COLS = 2 * MLSTM_HEADS + S5_WIDTH
TAIL_WIDTH = 5 * LANES
GATE_LANE0 = LANES - 2 * MLSTM_HEADS
GATE_COLBLK = S5_WIDTH // LANES
ROT_COL_TILE = 0
ROT_GROUP = 256
LN_TILE = 256
CONV_ROWS = 64
S5_T = 16
S5_TGROUP = 2
S5_GB = S5_WIDTH // LANES
S5_GPB = LANES // S5_GROUP
S5_SB = S5_GPB * S5_STATE
S5_KT = S5_T * LANES
VMEM_LIMIT = 56 * 1024 * 1024


def _cparams(sem):
    return pltpu.CompilerParams(dimension_semantics=sem, vmem_limit_bytes=VMEM_LIMIT)


def _mm_kernel(a_ref, b_ref, o_ref, *scratch, nk, act, b_cols_major):
    def finish(r):
        if act == "relu2":
            r = jnp.square(jnp.maximum(r, 0.0))
        elif act == "rotate_tail":
            r = pltpu.roll(r, TAIL_WIDTH - 2 * MLSTM_HEADS, 1)
        o_ref[...] = r.astype(o_ref.dtype)

    if nk == 1:
        finish(_dot_nt(a_ref[...], b_ref[...]) if b_cols_major else _dot(a_ref[...], b_ref[...]))
        return
    acc_ref, = scratch
    k = pl.program_id(2)

    @pl.when(k == 0)
    def _():
        acc_ref[...] = jnp.zeros_like(acc_ref)

    acc_ref[...] += _dot(a_ref[...], b_ref[...])

    @pl.when(k == nk - 1)
    def _():
        finish(acc_ref[...])


def matmul(a, b, layer, *, out_dtype, tm, tn, tk=None, act=None, b_cols_major=False):
    m, kdim = a.shape
    n = b.shape[1] if b_cols_major else b.shape[2]
    tk = kdim if tk is None else tk
    nk = kdim // tk
    assert m % tm == 0 and n % tn == 0 and kdim % tk == 0 and b.shape[2 if b_cols_major else 1] == kdim
    assert nk == 1 or not b_cols_major
    b_spec = (pl.BlockSpec((None, tn, tk), lambda i, j, k: (layer, j, k)) if b_cols_major
              else pl.BlockSpec((None, tk, tn), lambda i, j, k: (layer, k, j)))
    return pl.pallas_call(
        functools.partial(_mm_kernel, nk=nk, act=act, b_cols_major=b_cols_major),
        out_shape=jax.ShapeDtypeStruct((m, n), out_dtype),
        grid=(m // tm, n // tn, nk),
        in_specs=[pl.BlockSpec((tm, tk), lambda i, j, k: (i, k)), b_spec],
        out_specs=pl.BlockSpec((tm, tn), lambda i, j, k: (i, j)),
        scratch_shapes=[] if nk == 1 else [pltpu.VMEM((tm, tn), F32)],
        compiler_params=_cparams(("parallel", "parallel", "arbitrary")),
    )(a, b)


def _in_proj_kernel(a_ref, b_ref, cos_ref, sin_ref, *rest, cast_a):
    j = pl.program_id(1)
    if cast_a:
        o_ref, ab_ref = rest

        @pl.when(j == 0)
        def _():
            ab_ref[...] = a_ref[...].astype(BF16)
        a_src = ab_ref
    else:
        o_ref, = rest
        a_src = a_ref

    @pl.when(j != ROT_COL_TILE)
    def _():
        o_ref[...] = _dot_nt(a_src[...], b_ref[...])

    @pl.when(j == ROT_COL_TILE)
    def _():
        a = a_src[...]
        cos = cos_ref[...]
        sin = sin_ref[...]
        for g in range(o_ref.shape[1] // ROT_GROUP):
            z = _dot_nt(a, b_ref[g * ROT_GROUP:(g + 1) * ROT_GROUP, :])
            for h in range(ROT_GROUP // HEAD_DIM):
                c0 = g * ROT_GROUP + h * HEAD_DIM
                t = z[:, h * HEAD_DIM:(h + 1) * HEAD_DIM]
                rot = t * cos + pltpu.roll(t, HEAD_DIM // 2, 1) * sin
                o_ref[:, c0:c0 + HEAD_DIM] = rot * QK_SCALE if c0 >= RET_WIDTH else rot


def in_projection(a, b, cos_t, sin_t, layer, *, tm, tn):
    m, kdim = a.shape
    n = b.shape[1]
    cast_a = a.dtype != BF16
    assert m % tm == 0 and n % tn == 0 and b.shape[2] == kdim and tn == 2 * RET_WIDTH
    a_blk = pl.BlockSpec((tm, kdim), lambda i, j: (i, 0))
    tab = pl.BlockSpec((tm, LANES), lambda i, j: (i, 0))
    out_shape = [jax.ShapeDtypeStruct((m, n), F32)]
    out_specs = [pl.BlockSpec((tm, tn), lambda i, j: (i, j))]
    if cast_a:
        out_shape.append(jax.ShapeDtypeStruct((m, kdim), BF16))
        out_specs.append(a_blk)
    out = pl.pallas_call(
        functools.partial(_in_proj_kernel, cast_a=cast_a),
        out_shape=out_shape,
        grid=(m // tm, n // tn),
        in_specs=[a_blk, pl.BlockSpec((None, tn, kdim), lambda i, j: (layer, j, 0)), tab, tab],
        out_specs=out_specs,
        compiler_params=_cparams(("parallel", "arbitrary")),
    )(a, b, cos_t, sin_t)
    return (out[0], out[1]) if cast_a else (out[0], a)


def _cast_cols_kernel(x_ref, o_ref, *, col0, ncols_total):
    tc = x_ref.shape[0]
    col = col0 + pl.program_id(0) * tc + lax.broadcasted_iota(jnp.int32, (tc, 1), 0)
    for d in range(x_ref.shape[1]):
        o_ref[d] = jnp.where(col < ncols_total, x_ref[:, d, :], 0.0).astype(o_ref.dtype)


def cast_cols_transposed(w_t, col0, ncols, tc):
    ctot, depth, kdim = w_t.shape
    assert col0 % tc == 0 and ncols % tc == 0
    return pl.pallas_call(
        functools.partial(_cast_cols_kernel, col0=col0, ncols_total=ctot),
        out_shape=jax.ShapeDtypeStruct((depth, ncols, kdim), BF16),
        grid=(ncols // tc,),
        in_specs=[pl.BlockSpec((tc, depth, kdim), lambda i: (col0 // tc + i, 0, 0))],
        out_specs=pl.BlockSpec((depth, tc, kdim), lambda i: (0, i, 0)),
        compiler_params=_cparams(("parallel",)),
    )(w_t)


def _rope_kernel(pos_ref, inv_ref, sign_ref, cos_ref, sin_ref):
    ang = pos_ref[...] * inv_ref[...]
    cos_ref[...] = jnp.cos(ang)
    sin_ref[...] = jnp.sin(ang) * sign_ref[...]


def rope_tables(pos_col, tl):
    l = pos_col.shape[0]
    half = np.arange(0, HEAD_DIM, 2, dtype=np.float32) / np.float32(HEAD_DIM)
    inv = (np.float32(ROPE_BASE) ** (-half)).astype(np.float32)
    inv2 = jnp.asarray(np.concatenate([inv, inv])[None, :])
    sign = jnp.asarray(np.concatenate([-np.ones(64, np.float32), np.ones(64, np.float32)])[None, :])
    row = pl.BlockSpec((1, LANES), lambda i: (0, 0))
    return pl.pallas_call(
        _rope_kernel,
        out_shape=[jax.ShapeDtypeStruct((l, LANES), F32)] * 2,
        grid=(l // tl,),
        in_specs=[pl.BlockSpec((tl, 1), lambda i: (i, 0)), row, row],
        out_specs=[pl.BlockSpec((tl, LANES), lambda i: (i, 0))] * 2,
        compiler_params=_cparams(("parallel",)),
    )(pos_col, inv2, sign)


def _head_norm(y):
    mu = jnp.mean(y, axis=-1, keepdims=True)
    d = y - mu
    var = jnp.mean(d * d, axis=-1, keepdims=True)
    return d * lax.rsqrt(var + LN_EPS)


def _sigmoid(x):
    return 1.0 / (1.0 + jnp.exp(-x))


def _dot_nt(a, b, precision=None):
    return lax.dot_general(a, b, (((1,), (1,)), ((), ())), precision=precision, preferred_element_type=F32)


def _dot(a, b):
    return jnp.dot(a, b, preferred_element_type=F32)


def _head_lanes(h):
    return slice(h * HEAD_DIM, (h + 1) * HEAD_DIM)


def _layer_spec(stacked, layer):
    return pl.BlockSpec((None,) + stacked.shape[1:], lambda *_: (layer, 0, 0))


def _ret_log_gamma(h):
    return float(np.log(np.float32(1.0) - np.float32(2.0) ** np.float32(-5.0 - h)))


def _ret_kernel(q_ref, k_ref, v_ref, g_ref, w_ref, o_ref,
                state_ref, decay_ref, qdec_ref, kdec_ref, qb_buf, s_buf, kdt_buf, intra_buf, incr_buf, *, nchunk):
    rb = pl.program_id(0)

    @pl.when(rb == 0)
    def _():
        state_ref[...] = jnp.zeros_like(state_ref)
        ii = lax.broadcasted_iota(jnp.int32, (CHUNK, CHUNK), 0)
        jj = lax.broadcasted_iota(jnp.int32, (CHUNK, CHUNK), 1)
        rel = (ii - jj).astype(F32)
        idx = ii.astype(F32)
        for h in range(RET_HEADS):
            lg = _ret_log_gamma(h)
            decay_ref[h] = jnp.where(rel >= 0.0, jnp.exp(lg * jnp.maximum(rel, 0.0)), 0.0)
            qdec_ref[h] = jnp.exp(lg * (idx + 1.0))
            kdec_ref[h] = jnp.exp(lg * (CHUNK - 1.0 - idx))

    for c in range(nchunk):
        rows = slice(c * CHUNK, (c + 1) * CHUNK)
        for h in range(RET_HEADS):
            ln = _head_lanes(h)
            kr = k_ref[rows, ln]
            qb = q_ref[rows, ln].astype(BF16)
            qb_buf[rows, ln] = qb
            s_buf[rows, ln] = (_dot_nt(qb, kr.astype(BF16)) * decay_ref[h]).astype(BF16)
            kdt_buf[rows, ln] = jnp.transpose((kr * kdec_ref[h]).astype(BF16))

    for c in range(nchunk):
        rows = slice(c * CHUNK, (c + 1) * CHUNK)
        for h in range(RET_HEADS):
            ln = _head_lanes(h)
            lhs = jnp.concatenate([s_buf[rows, ln], kdt_buf[rows, ln]], axis=0)
            both = _dot(lhs, v_ref[rows, ln].astype(BF16))
            intra_buf[rows, ln] = both[:CHUNK]
            incr_buf[c, h] = both[CHUNK:]

    for c in range(nchunk):
        rows = slice(c * CHUNK, (c + 1) * CHUNK)
        for h in range(RET_HEADS):
            ln = _head_lanes(h)
            st = state_ref[h]
            out = intra_buf[rows, ln] + _dot(qb_buf[rows, ln], st.astype(BF16)) * qdec_ref[h]
            state_ref[h] = st * math.exp(_ret_log_gamma(h) * CHUNK) + incr_buf[c, h]
            g = g_ref[rows, ln]
            y = _head_norm(out) * w_ref[:, ln] * (g * _sigmoid(g))
            o_ref[rows, ln] = y.astype(o_ref.dtype)


def retention_block(proj, norm_w, layer, tb):
    l = proj.shape[0]

    def col(j):
        return pl.BlockSpec((tb, RET_WIDTH), lambda r, j=j: (r, j))

    hsq = pltpu.VMEM((RET_HEADS, CHUNK, CHUNK), F32)
    half = pltpu.VMEM((tb, RET_WIDTH), BF16)
    return pl.pallas_call(
        functools.partial(_ret_kernel, nchunk=tb // CHUNK),
        out_shape=jax.ShapeDtypeStruct((l, RET_WIDTH), BF16),
        grid=(l // tb,),
        in_specs=[col(0), col(1), col(2), col(3), _layer_spec(norm_w, layer)],
        out_specs=pl.BlockSpec((tb, RET_WIDTH), lambda r: (r, 0)),
        scratch_shapes=[hsq, hsq, hsq, hsq, half, half, half, pltpu.VMEM((tb, RET_WIDTH), F32),
                        pltpu.VMEM((tb // CHUNK, RET_HEADS, HEAD_DIM, HEAD_DIM), F32)],
        compiler_params=_cparams(("arbitrary",)),
    )(proj, proj, proj, proj, norm_w)


def _log_sigmoid(x):
    return -(jnp.maximum(-x, 0.0) + jnp.log1p(jnp.exp(-jnp.abs(x))))


def _mlstm_kernel(q_ref, k_ref, v_ref, og_ref, gate_ref, cw_ref, cb_ref, gb_ref, nw_ref,
                  o_ref, c_st, m_st, qbuf, kbuf, qs, ks, gs, r_buf, cmax_buf, bcum_buf, mall_buf, floor_buf,
                  p_buf, u_buf, wq_buf, s_buf, kwt_buf, *, nchunk, tb):
    rb = pl.program_id(0)
    pad = SUBLANES
    nh = MLSTM_HEADS

    @pl.when(rb == 0)
    def _():
        c_st[...] = jnp.zeros_like(c_st)
        m_st[...] = jnp.zeros_like(m_st)
        qbuf[0:pad, :] = jnp.zeros((pad, MLSTM_WIDTH), F32)
        kbuf[0:pad, :] = jnp.zeros((pad, MLSTM_WIDTH), F32)

    qbuf[pad:pad + tb, :] = q_ref[...]
    kbuf[pad:pad + tb, :] = k_ref[...]
    for r0 in range(0, tb, CONV_ROWS):
        for buf, woff, dst, scale in ((qbuf, 0, qs, None), (kbuf, MLSTM_WIDTH, ks, QK_SCALE)):
            wl = slice(woff, woff + MLSTM_WIDTH)
            acc = jnp.broadcast_to(cb_ref[:, wl], (CONV_ROWS, MLSTM_WIDTH))
            for tap in range(CONV_WIDTH):
                off = r0 + pad - (CONV_WIDTH - 1) + tap
                acc = acc + buf[off:off + CONV_ROWS, :] * cw_ref[tap:tap + 1, wl]
            act = acc * _sigmoid(acc)
            dst[r0:r0 + CONV_ROWS, :] = act if scale is None else act * scale
    qbuf[0:pad, :] = q_ref[tb - pad:tb, :]
    kbuf[0:pad, :] = k_ref[tb - pad:tb, :]

    lane_row = lax.broadcasted_iota(jnp.int32, (1, LANES), 1)
    graw = pltpu.roll(gate_ref[...] + gb_ref[...], LANES - GATE_LANE0, 1)
    gs[...] = jnp.where(lane_row < nh, graw, jnp.where(lane_row < 2 * nh, _log_sigmoid(graw), 0.0))

    ii = lax.broadcasted_iota(jnp.int32, (CHUNK, CHUNK), 0)
    jj = lax.broadcasted_iota(jnp.int32, (CHUNK, CHUNK), 1)
    causal = ii >= jj
    tril = causal.astype(F32)
    lane8 = lax.broadcasted_iota(jnp.int32, (SUBLANES, LANES), 1)
    ones_b = jnp.ones((CHUNK, HEAD_DIM), BF16)
    full = (CHUNK, CHUNK)

    def chunk_rows(c):
        return slice(c * CHUNK, (c + 1) * CHUNK)

    def aug_lanes(h):
        return slice(h * 2 * HEAD_DIM, (h + 1) * 2 * HEAD_DIM)

    for c in range(nchunk):
        rows = chunk_rows(c)
        g = gs[rows, :]
        gcum = jnp.dot(tril, g, precision=lax.Precision.HIGHEST, preferred_element_type=F32)
        bcum = pltpu.roll(gcum, LANES - nh, 1)
        r8 = jnp.transpose(g - bcum)[0:SUBLANES, :]
        cmax = r8
        for sh in (1, 2, 4, 8, 16, 32, 64):
            cmax = jnp.maximum(cmax, jnp.where(lane8 >= sh, pltpu.roll(cmax, sh, 1), -jnp.inf))
        r_buf[c] = r8
        cmax_buf[rows, :] = jnp.transpose(
            jnp.concatenate([cmax, jnp.zeros((CHUNK - SUBLANES, LANES), F32)], axis=0))
        bcum_buf[rows, :] = bcum

    m_prev = m_st[0:1, :]
    m_prevs, keep_rows = [], []
    for c in range(nchunk):
        rows = chunk_rows(c)
        m_all = jnp.maximum(cmax_buf[rows, :], m_prev)
        floor_buf[rows, :] = jnp.exp(-(bcum_buf[rows, :] + m_all))
        mall_buf[rows, :] = m_all
        m_last = m_all[CHUNK - 1:CHUNK, :]
        m_prevs.append(m_prev)
        keep_rows.append(jnp.exp(m_prev - m_last))
        m_prev = bcum_buf[c * CHUNK + CHUNK - 1:(c + 1) * CHUNK, :] + m_last
    m_st[0:1, :] = m_prev

    for c in range(nchunk):
        rows = chunk_rows(c)
        for h in range(nh):
            ln = _head_lanes(h)
            qc = qs[rows, ln]
            kb = ks[rows, ln].astype(BF16)
            m_bc = jnp.broadcast_to(mall_buf[rows, h:h + 1], full)
            dmat = jnp.exp(jnp.where(causal, r_buf[c, h:h + 1, :] - m_bc, -jnp.inf))
            s_buf[rows, ln] = (_dot_nt(qc.astype(BF16), kb) * dmat).astype(BF16)
            kwt_buf[rows, ln] = (jnp.transpose(kb).astype(F32) * dmat[CHUNK - 1:CHUNK, :]).astype(BF16)
            wq_buf[rows, ln] = (jnp.exp(m_prevs[c][:, h:h + 1] - m_bc) * qc).astype(BF16)

    for c in range(nchunk):
        rows = chunk_rows(c)
        for h in range(nh):
            ln = _head_lanes(h)
            v_aug = jnp.concatenate([v_ref[rows, ln].astype(BF16), ones_b], axis=1)
            both = _dot(jnp.concatenate([s_buf[rows, ln], kwt_buf[rows, ln]], axis=0), v_aug)
            p_buf[rows, aug_lanes(h)] = both[:CHUNK]
            u_buf[c, h] = both[CHUNK:]

    for c in range(nchunk):
        rows = chunk_rows(c)
        for h in range(nh):
            ln = _head_lanes(h)
            cs = c_st[h]
            res = p_buf[rows, aug_lanes(h)] + _dot(wq_buf[rows, ln], cs.astype(BF16))
            c_st[h] = keep_rows[c][:, h:h + 1] * cs + u_buf[c, h]
            den = res[:, HEAD_DIM + h:HEAD_DIM + h + 1]
            inv = 1.0 / jnp.maximum(jnp.abs(den), floor_buf[rows, h:h + 1])
            y = _head_norm(res[:, :HEAD_DIM] * inv) * nw_ref[:, ln] * _sigmoid(og_ref[rows, ln])
            o_ref[rows, ln] = y.astype(o_ref.dtype)


def mlstm_block(proj, tail, conv_w, conv_b, gate_bias, norm_w, layer, tb):
    l = proj.shape[0]
    base = (4 * RET_WIDTH) // MLSTM_WIDTH

    def col(j):
        return pl.BlockSpec((tb, MLSTM_WIDTH), lambda r, j=j: (r, base + j))

    nchunk = tb // CHUNK
    wide = pltpu.VMEM((tb, MLSTM_WIDTH), F32)
    hist = pltpu.VMEM((tb + SUBLANES, MLSTM_WIDTH), F32)
    narrow = pltpu.VMEM((tb, LANES), F32)
    half = pltpu.VMEM((tb, MLSTM_WIDTH), BF16)
    rows8 = pltpu.VMEM((nchunk, SUBLANES, LANES), F32)
    return pl.pallas_call(
        functools.partial(_mlstm_kernel, nchunk=nchunk, tb=tb),
        out_shape=jax.ShapeDtypeStruct((l, MLSTM_WIDTH), BF16),
        grid=(l // tb,),
        in_specs=[col(0), col(1), col(2), col(3),
                  pl.BlockSpec((tb, LANES), lambda r: (r, GATE_COLBLK)),
                  _layer_spec(conv_w, layer), _layer_spec(conv_b, layer), _layer_spec(gate_bias, layer),
                  _layer_spec(norm_w, layer)],
        out_specs=pl.BlockSpec((tb, MLSTM_WIDTH), lambda r: (r, 0)),
        scratch_shapes=[pltpu.VMEM((MLSTM_HEADS, HEAD_DIM, 2 * HEAD_DIM), F32), pltpu.VMEM((SUBLANES, LANES), F32),
                        hist, hist, wide, wide, narrow, rows8, narrow, narrow, narrow, narrow,
                        pltpu.VMEM((tb, 2 * MLSTM_WIDTH), F32),
                        pltpu.VMEM((nchunk, MLSTM_HEADS, HEAD_DIM, 2 * HEAD_DIM), F32),
                        half, half, half],
        compiler_params=_cparams(("arbitrary",)),
    )(proj, proj, proj, proj, tail, conv_w, conv_b, gate_bias, norm_w)


def _s5_prep_kernel(lre_ref, lim_ref, ldt_ref, btre_ref, btim_ref, ctre_ref, ctim_ref,
                    toep_ref, vre_ref, vim_ref, wre_ref, wim_ref, apre_ref, apim_ref):
    lre = lre_ref[...]
    lim = lim_ref[...]
    dt = jnp.exp(ldt_ref[...])
    mag = jnp.exp(lre * dt)
    ang = lim * dt
    zr = mag * jnp.cos(ang) - 1.0
    zi = mag * jnp.sin(ang)
    den = lre * lre + lim * lim
    w_re = (zr * lre + zi * lim) / den
    w_im = (zi * lre - zr * lim) / den
    row_g = lax.broadcasted_iota(jnp.int32, (LANES, S5_SB), 0) // S5_GROUP
    col_g = lax.broadcasted_iota(jnp.int32, (LANES, S5_SB), 1) // S5_STATE
    same = row_g == col_g
    bt_re = btre_ref[...]
    bt_im = btim_ref[...]
    bb_re = jnp.where(same, w_re * bt_re - w_im * bt_im, 0.0)
    bb_im = jnp.where(same, w_re * bt_im + w_im * bt_re, 0.0)
    ct_re = jnp.where(same, ctre_ref[...], 0.0)
    ct_im = jnp.where(same, ctim_ref[...], 0.0)
    hp = lax.Precision.HIGHEST
    kblk = []
    for d in range(S5_T + 1):
        pm = jnp.exp(lre * dt * float(d))
        pa = lim * dt * float(d)
        p_re = pm * jnp.cos(pa)
        p_im = pm * jnp.sin(pa)
        if d < S5_T:
            ab_re = p_re * bb_re - p_im * bb_im
            ab_im = p_re * bb_im + p_im * bb_re
            srow = slice((S5_T - 1 - d) * LANES, (S5_T - d) * LANES)
            vre_ref[0, srow, :] = ab_re.astype(BF16)
            vim_ref[0, srow, :] = ab_im.astype(BF16)
            kblk.append((_dot_nt(ab_re, ct_re, hp) - _dot_nt(ab_im, ct_im, hp)).astype(BF16))
        if d >= 1:
            trow = slice((d - 1) * LANES, d * LANES)
            wre_ref[0, trow, :] = (ct_re * p_re - ct_im * p_im).astype(BF16)
            wim_ref[0, trow, :] = (-(ct_re * p_im + ct_im * p_re)).astype(BF16)
    apre_ref[...] = jnp.zeros_like(apre_ref)
    apim_ref[...] = jnp.zeros_like(apim_ref)
    for k in range(SUBLANES + 1):
        pm = jnp.exp(lre * dt * float(S5_T * k))
        pa = lim * dt * float(S5_T * k)
        apre_ref[k:k + 1, :] = pm * jnp.cos(pa)
        apim_ref[k:k + 1, :] = pm * jnp.sin(pa)
    zero = jnp.zeros((LANES, LANES), BF16)
    for s in range(S5_T):
        for t in range(S5_T):
            toep_ref[0, s * LANES:(s + 1) * LANES, t * LANES:(t + 1) * LANES] = kblk[t - s] if t >= s else zero


def s5_operator_inputs(lam_re, lam_im, log_dt, b_re, b_im, c_re, c_im):
    depth = lam_re.shape[0]
    nst = S5_GROUPS * S5_STATE

    def tiled(m):
        return jnp.tile(m.astype(F32).reshape(depth, S5_WIDTH, S5_STATE), (1, 1, S5_GPB))

    lre = lam_re.astype(F32).reshape(depth, 1, nst)
    lim = lam_im.astype(F32).reshape(depth, 1, nst)
    ldt = jnp.repeat(log_dt.astype(F32), S5_STATE, axis=-1).reshape(depth, 1, nst)
    return (lre, lim, ldt, tiled(jnp.swapaxes(b_re, -1, -2)), tiled(jnp.swapaxes(b_im, -1, -2)),
            tiled(c_re), tiled(c_im))


def s5_operators(prep_inputs, layer):
    nst = S5_GROUPS * S5_STATE
    lane_in = pl.BlockSpec((None, 1, S5_SB), lambda g: (layer, 0, g))
    par_in = pl.BlockSpec((None, LANES, S5_SB), lambda g: (layer, g, 0))
    pow_out = pl.BlockSpec((2 * SUBLANES, S5_SB), lambda g: (0, g))

    def out_blk(rows, cols):
        return pl.BlockSpec((1, rows, cols), lambda g: (g, 0, 0))

    return pl.pallas_call(
        _s5_prep_kernel,
        out_shape=[jax.ShapeDtypeStruct((S5_GB, S5_KT, S5_KT), BF16)]
        + [jax.ShapeDtypeStruct((S5_GB, S5_KT, S5_SB), BF16)] * 4
        + [jax.ShapeDtypeStruct((2 * SUBLANES, nst), F32)] * 2,
        grid=(S5_GB,),
        in_specs=[lane_in, lane_in, lane_in, par_in, par_in, par_in, par_in],
        out_specs=[out_blk(S5_KT, S5_KT)] + [out_blk(S5_KT, S5_SB)] * 4 + [pow_out, pow_out],
        compiler_params=_cparams(("parallel",)),
    )(*prep_inputs)


def _gelu_tanh(x):
    c = math.sqrt(2.0 / math.pi)
    return 0.5 * x * (1.0 + jnp.tanh(c * (x + 0.044715 * (x * x * x))))


def _s5_kernel(u_ref, toep_ref, vre_ref, vim_ref, wre_ref, wim_ref, apre_ref, apim_ref, d_ref, y_ref,
               car_re, car_im, x_re, x_im, *, tmc):
    rb = pl.program_id(1)

    @pl.when(rb == 0)
    def _():
        car_re[...] = jnp.zeros_like(car_re)
        car_im[...] = jnp.zeros_like(car_im)

    us = [u_ref[pl.ds(s, tmc, stride=S5_T), :] for s in range(S5_T)]
    ucat = jnp.concatenate([u.astype(BF16) for u in us], axis=1)
    y_re = _dot(ucat, vre_ref[0])
    y_im = _dot(ucat, vim_ref[0])

    row_in_tile = lax.broadcasted_iota(jnp.int32, (tmc, 1), 0) & (SUBLANES - 1)

    def shifted(v, s):
        return jnp.where(row_in_tile >= s, pltpu.roll(v, s, 0), 0.0)

    for s in (1, 2, 4):
        a_r = apre_ref[s:s + 1, :]
        a_i = apim_ref[s:s + 1, :]
        s_re, s_im = shifted(y_re, s), shifted(y_im, s)
        y_re, y_im = y_re + a_r * s_re - a_i * s_im, y_im + a_r * s_im + a_i * s_re
    e_re, e_im = shifted(y_re, 1), shifted(y_im, 1)
    p_re = apre_ref[0:SUBLANES, :]
    p_im = apim_ref[0:SUBLANES, :]
    a8_r = apre_ref[SUBLANES:SUBLANES + 1, :]
    a8_i = apim_ref[SUBLANES:SUBLANES + 1, :]
    xr = car_re[0:1, :]
    xi = car_im[0:1, :]
    for t in range(tmc // SUBLANES):
        rows = slice(t * SUBLANES, (t + 1) * SUBLANES)
        x_re[rows, :] = e_re[rows] + p_re * xr - p_im * xi
        x_im[rows, :] = e_im[rows] + p_re * xi + p_im * xr
        last = (t + 1) * SUBLANES - 1
        xr, xi = (a8_r * xr - a8_i * xi + y_re[last:last + 1], a8_r * xi + a8_i * xr + y_im[last:last + 1])
    car_re[0:1, :] = xr
    car_im[0:1, :] = xi
    xb_re = x_re[...].astype(BF16)
    xb_im = x_im[...].astype(BF16)
    for g in range(S5_T // S5_TGROUP):
        kin = (g + 1) * S5_TGROUP * LANES
        cols = slice(g * S5_TGROUP * LANES, kin)
        y = (_dot(ucat[:, :kin], toep_ref[0, :kin, cols]) + _dot_nt(xb_re, wre_ref[0, cols, :])
             + _dot_nt(xb_im, wim_ref[0, cols, :]))
        for j in range(S5_TGROUP):
            t = g * S5_TGROUP + j
            yt = y[:, j * LANES:(j + 1) * LANES] + d_ref[...] * us[t]
            y_ref[pl.ds(t, tmc, stride=S5_T), :] = _gelu_tanh(yt)


def s5_block(tail, ops, d_skip, layer, tb):
    l = tail.shape[0]
    toep, v_re, v_im, w_re, w_im, ap_re, ap_im = ops
    tmc = tb // S5_T

    def per_gb(cols):
        return pl.BlockSpec((1, S5_KT, cols), lambda g, r: (g, 0, 0))

    pow_blk = pl.BlockSpec((2 * SUBLANES, S5_SB), lambda g, r: (0, g))
    fold = pltpu.VMEM((tmc, S5_SB), F32)
    carry = pltpu.VMEM((SUBLANES, S5_SB), F32)
    return pl.pallas_call(
        functools.partial(_s5_kernel, tmc=tmc),
        out_shape=jax.ShapeDtypeStruct((l, S5_WIDTH), F32),
        grid=(S5_GB, l // tb),
        in_specs=[pl.BlockSpec((tb, LANES), lambda g, r: (r, g)),
                  per_gb(S5_KT), per_gb(S5_SB), per_gb(S5_SB), per_gb(S5_SB), per_gb(S5_SB),
                  pow_blk, pow_blk, pl.BlockSpec((None, 1, LANES), lambda g, r: (layer, 0, g))],
        out_specs=pl.BlockSpec((tb, LANES), lambda g, r: (r, g)),
        scratch_shapes=[carry, carry, fold, fold],
        compiler_params=_cparams(("parallel", "arbitrary")),
    )(tail, toep, v_re, v_im, w_re, w_im, ap_re, ap_im, d_skip)


def _layer_norm_by_column_tiles(z_tile, z_buf, lw_ref, lb_ref, o_ref, ob_ref):
    rows = z_buf.shape[0]
    pivot = None
    s1 = jnp.zeros((rows, LANES), F32)
    s2 = jnp.zeros((rows, LANES), F32)
    for n in range(D_MODEL // LN_TILE):
        cols = slice(n * LN_TILE, (n + 1) * LN_TILE)
        z = z_tile(cols)
        z_buf[:, cols] = z
        if pivot is None:
            pivot = jnp.mean(z, axis=-1, keepdims=True)
        for j in range(LN_TILE // LANES):
            dz = z[:, j * LANES:(j + 1) * LANES] - pivot
            s1 = s1 + dz
            s2 = s2 + dz * dz
    m1 = jnp.sum(s1, axis=-1, keepdims=True) * (1.0 / D_MODEL)
    var = jnp.sum(s2, axis=-1, keepdims=True) * (1.0 / D_MODEL) - m1 * m1
    mu = pivot + m1
    rstd = lax.rsqrt(var + LN_EPS)
    for n in range(D_MODEL // LN_TILE):
        cols = slice(n * LN_TILE, (n + 1) * LN_TILE)
        out = (z_buf[:, cols] - mu) * rstd * lw_ref[:, cols] + lb_ref[:, cols]
        o_ref[:, cols] = out
        ob_ref[:, cols] = out.astype(BF16)


def _outproj_kernel(a1_ref, a2_ref, ys_ref, gw_ref, gb_ref, w1_ref, w2_ref, w3_ref, x_ref, lw_ref, lb_ref,
                    o_ref, ob_ref, z_buf):
    a1, a2 = a1_ref[...], a2_ref[...]
    ys = ys_ref[...]
    a3 = (ys * _sigmoid(_dot(ys.astype(BF16), gw_ref[...]) + gb_ref[...])).astype(BF16)

    def z_tile(cols):
        mix = _dot(a1, w1_ref[:, cols]) + _dot(a2, w2_ref[:, cols]) + _dot(a3, w3_ref[:, cols])
        return ALPHA * x_ref[:, cols] + mix

    _layer_norm_by_column_tiles(z_tile, z_buf, lw_ref, lb_ref, o_ref, ob_ref)


def outproj_block(y_ret, y_m, y_s, glu_w, glu_b, w_out, x, ln_w, ln_b, layer, tm):
    l = x.shape[0]

    def rows(n):
        return pl.BlockSpec((tm, n), lambda i: (i, 0))

    def wrows(n, blk):
        return pl.BlockSpec((None, n, D_MODEL), lambda i: (layer, blk, 0))

    return pl.pallas_call(
        _outproj_kernel,
        out_shape=[jax.ShapeDtypeStruct((l, D_MODEL), F32), jax.ShapeDtypeStruct((l, D_MODEL), BF16)],
        grid=(l // tm,),
        in_specs=[rows(RET_WIDTH), rows(MLSTM_WIDTH), rows(S5_WIDTH),
                  _layer_spec(glu_w, layer), _layer_spec(glu_b, layer), wrows(RET_WIDTH, 0), wrows(MLSTM_WIDTH, 1), wrows(S5_WIDTH, (RET_WIDTH + MLSTM_WIDTH) // S5_WIDTH),
                  rows(D_MODEL), _layer_spec(ln_w, layer), _layer_spec(ln_b, layer)],
        out_specs=[rows(D_MODEL), rows(D_MODEL)],
        scratch_shapes=[pltpu.VMEM((tm, D_MODEL), F32)],
        compiler_params=_cparams(("parallel",)),
    )(y_ret, y_m, y_s, glu_w, glu_b, w_out, w_out, w_out, x, ln_w, ln_b)


def _down_kernel(h_ref, w_ref, x1_ref, r_ref, rb_ref):
    r = ALPHA * x1_ref[...] + _dot(h_ref[...], w_ref[...])
    r_ref[...] = r
    rb_ref[...] = r.astype(BF16)


def down_block(hid, w_down, x1, layer, tm):
    l, ff = hid.shape
    rows = pl.BlockSpec((tm, D_MODEL), lambda i: (i, 0))
    return pl.pallas_call(
        _down_kernel,
        out_shape=[jax.ShapeDtypeStruct((l, D_MODEL), F32), jax.ShapeDtypeStruct((l, D_MODEL), BF16)],
        grid=(l // tm,),
        in_specs=[pl.BlockSpec((tm, ff), lambda i: (i, 0)),
                  pl.BlockSpec((None, ff, D_MODEL), lambda i: (layer, 0, 0), pipeline_mode=pl.Buffered(1)), rows],
        out_specs=[rows, rows],
        compiler_params=_cparams(("parallel",)),
    )(hid, w_down, x1)


def _final_kernel(r_ref, rb_ref, p_ref, wg_ref, wp_ref, lw_ref, lb_ref, o_ref, ob_ref, z_buf):
    rb, pb = rb_ref[...], p_ref[...]

    def z_tile(cols):
        gate = _sigmoid(_dot(rb, wg_ref[:, cols]))
        return r_ref[:, cols] + gate * _dot(pb, wp_ref[:, cols])

    _layer_norm_by_column_tiles(z_tile, z_buf, lw_ref, lb_ref, o_ref, ob_ref)


def final_block(r, rb, p_b, w_gate, w_ple, ln_w, ln_b, layer, tm):
    l = r.shape[0]
    rows = pl.BlockSpec((tm, D_MODEL), lambda i: (i, 0))
    return pl.pallas_call(
        _final_kernel,
        out_shape=[jax.ShapeDtypeStruct((l, D_MODEL), F32), jax.ShapeDtypeStruct((l, D_MODEL), BF16)],
        grid=(l // tm,),
        in_specs=[rows, rows, pl.BlockSpec((None, tm, PLE_DIM), lambda i: (layer, i, 0)),
                  _layer_spec(w_gate, layer), _layer_spec(w_ple, layer), _layer_spec(ln_w, layer),
                  _layer_spec(ln_b, layer)],
        out_specs=[rows, rows],
        scratch_shapes=[pltpu.VMEM((tm, D_MODEL), F32)],
        compiler_params=_cparams(("parallel",)),
    )(r, rb, p_b, w_gate, w_ple, ln_w, ln_b)


def _rows3(v):
    return v.astype(F32).reshape(v.shape[0], 1, -1)


def _tiles(l):
    return {"mm_m": min(1024, l), "tail_m": min(2048, l), "mix": min(512, l), "s5": min(4096, l), "row": min(512, l),
            "down": min(256, l), "in_n": 2 * RET_WIDTH, "up_n": 2048, "rope": min(2048, l)}


def kernel(x, p, positions, w_in, mlstm_conv_w, mlstm_conv_b, mlstm_i_bias, mlstm_f_bias, ret_norm_w, mlstm_norm_w, s5_lambda_re, s5_lambda_im, s5_log_dt, s5_B_re, s5_B_im, s5_C_re, s5_C_im, s5_D, s5_glu_w, s5_glu_b, w_out, ln1_w, ln1_b, w_up, w_down, w_gate, w_ple, ln2_w, ln2_b):
    bsz, l, _ = x.shape
    depth = w_in.shape[0]
    assert bsz == 1
    t = _tiles(l)
    cos_t, sin_t = rope_tables(positions.astype(F32).reshape(l, 1), t["rope"])
    xf = x.reshape(l, D_MODEL).astype(F32)
    xb = None
    p_b = p.reshape(depth, l, PLE_DIM).astype(BF16)

    w_in_t = jnp.transpose(w_in, (2, 0, 1))
    w_in_b = cast_cols_transposed(w_in_t, 0, MAIN_WIDTH, LANES)
    w_tail_b = cast_cols_transposed(w_in_t, MAIN_WIDTH, TAIL_WIDTH, LANES)
    w_out_b, w_up_b, w_down_b = w_out.astype(BF16), w_up.astype(BF16), w_down.astype(BF16)
    w_gate_b, w_ple_b, glu_w_b = w_gate.astype(BF16), w_ple.astype(BF16), s5_glu_w.astype(BF16)
    gate_bias = jnp.concatenate([jnp.zeros((depth, GATE_LANE0), F32), mlstm_i_bias.astype(F32),
                                 mlstm_f_bias.astype(F32)], axis=1).reshape(depth, 1, LANES)
    conv_w = mlstm_conv_w.astype(F32)
    conv_b, ret_nw, mlstm_nw = _rows3(mlstm_conv_b), _rows3(ret_norm_w), _rows3(mlstm_norm_w)
    s5_d, glu_b = _rows3(s5_D), _rows3(s5_glu_b)
    ln1w, ln1b, ln2w, ln2b = _rows3(ln1_w), _rows3(ln1_b), _rows3(ln2_w), _rows3(ln2_b)
    s5_in = s5_operator_inputs(s5_lambda_re, s5_lambda_im, s5_log_dt, s5_B_re, s5_B_im, s5_C_re, s5_C_im)

    for i in range(depth):
        proj, xb = in_projection(xf if xb is None else xb, w_in_b, cos_t, sin_t, i, tm=t["mm_m"], tn=t["in_n"])
        tail = matmul(xb, w_tail_b, i, out_dtype=F32, tm=t["tail_m"], tn=TAIL_WIDTH, act="rotate_tail",
                      b_cols_major=True)
        y_ret = retention_block(proj, ret_nw, i, t["mix"])
        y_m = mlstm_block(proj, tail, conv_w, conv_b, gate_bias, mlstm_nw, i, t["mix"])
        y_s = s5_block(tail, s5_operators(s5_in, i), s5_d, i, t["s5"])
        x1, x1b = outproj_block(y_ret, y_m, y_s, glu_w_b, glu_b, w_out_b, xf, ln1w, ln1b, i, t["row"])
        hid = matmul(x1b, w_up_b, i, out_dtype=BF16, tm=t["mm_m"], tn=t["up_n"], act="relu2")
        r, rb = down_block(hid, w_down_b, x1, i, t["down"])
        xf, xb = final_block(r, rb, p_b, w_gate_b, w_ple_b, ln2w, ln2b, i, t["row"])
    return xf.reshape(bsz, l, D_MODEL)
```
